```python
import math
import jax, jax.numpy as jnp
from jax import lax
import numpy as np

D_MODEL = 1024
BATCH = 16
SEQ = 2048
DEPTH = 2
DEC_BATCH = 4
DEC_SEQ = 4096
PAST_LEN = 128

HEAD_DIM = 64
A_HEADS = 4
A_QK_DIM = HEAD_DIM // 2
A_WIDTH = A_HEADS * HEAD_DIM
B_HEADS = 8
B_KV_HEADS = 2
B_WIDTH = B_HEADS * HEAD_DIM
C_WIDTH = D_MODEL - A_WIDTH - B_WIDTH
D_MIX = A_WIDTH + B_WIDTH + C_WIDTH
B_KV_WIDTH = B_KV_HEADS * HEAD_DIM
IN_SPLIT_WIDTHS = (A_WIDTH, A_WIDTH, A_WIDTH,
                   B_WIDTH, B_KV_WIDTH, B_KV_WIDTH,
                   C_WIDTH, C_WIDTH, C_WIDTH)
IN_COLS = 3 * A_WIDTH + B_WIDTH + 2 * B_KV_WIDTH + 3 * C_WIDTH
CONV_WIDTH = 3
N_GROUPS = 4
EXPERTS_PER_GROUP = 8
N_EXPERTS = N_GROUPS * EXPERTS_PER_GROUP
TOP_K = 2
D_EXPERT = 512
EXPERT_BLOCK = 256
Q_BLOCK = 128
GRID_W = 64
ROPE_THETA = 10000.0
EPS = 1e-6

kernel_name = "hymba_style_diffattn_gqa_shortconv_hmoe_encoder"


def _rmsnorm(x, g):
    x32 = x.astype(jnp.float32)
    y = x32 * lax.rsqrt(jnp.mean(x32 * x32, axis=-1, keepdims=True) + EPS)
    return (y * g.astype(jnp.float32)).astype(x.dtype)


def _rope_angles(pos, dim):
    freqs = ROPE_THETA ** (-jnp.arange(0, dim, 2, dtype=jnp.float32) / dim)
    ang = pos.astype(jnp.float32)[:, None] * freqs[None, :]
    return jnp.cos(ang), jnp.sin(ang)


def _rope(x, cos, sin):
    x32 = x.astype(jnp.float32)
    half = x.shape[-1] // 2
    x1, x2 = x32[..., :half], x32[..., half:]
    return jnp.concatenate([x1 * cos - x2 * sin, x1 * sin + x2 * cos], axis=-1).astype(x.dtype)


def _sweep(fn, q):
    b, s = q.shape[0], q.shape[1]
    nb = s // Q_BLOCK
    qb = jnp.moveaxis(q.reshape((b, nb, Q_BLOCK) + q.shape[2:]), 1, 0)
    out = jnp.moveaxis(lax.map(fn, qb), 0, 1)
    return out.reshape((b, s) + out.shape[3:])


def _diff_attention(q, k, v, lam_vecs, subln_g, lambda_init, cos, sin):
    b, s, h, dv = v.shape
    c = cos[:, None, None, :]
    sn = sin[:, None, None, :]
    q = _rope(q, c, sn)
    k = _rope(k, c, sn)
    lv = lam_vecs.astype(jnp.float32)
    lam = jnp.exp(jnp.sum(lv[0] * lv[1])) - jnp.exp(jnp.sum(lv[2] * lv[3])) + lambda_init
    scale = A_QK_DIM ** -0.5

    def block(qb):
        sc = jnp.einsum('bqhcd,bkhcd->bhcqk', qb, k).astype(jnp.float32) * scale
        p = jax.nn.softmax(sc, axis=-1)
        a = (p[:, :, 0] - lam * p[:, :, 1]).astype(v.dtype)
        return jnp.einsum('bhqk,bkhd->bqhd', a, v)

    o = _sweep(block, q)
    o = _rmsnorm(o, subln_g) * (1.0 - lambda_init)
    return o.reshape(b, s, h * dv)


def _axial_rope(x, cos_r, sin_r, cos_c, sin_c):
    half = x.shape[-1] // 2
    xr = _rope(x[..., :half], cos_r[:, None, :], sin_r[:, None, :])
    xc = _rope(x[..., half:], cos_c[:, None, :], sin_c[:, None, :])
    return jnp.concatenate([xr, xc], axis=-1)


def _gqa_axial(q, k, v, qn, kn, cos_r, sin_r, cos_c, sin_c):
    b, s, h, d = q.shape
    q = _axial_rope(_rmsnorm(q, qn), cos_r, sin_r, cos_c, sin_c)
    k = _axial_rope(_rmsnorm(k, kn), cos_r, sin_r, cos_c, sin_c)
    qg = q.reshape(b, s, B_KV_HEADS, B_HEADS // B_KV_HEADS, d)
    scale = d ** -0.5

    def block(qb):
        sc = jnp.einsum('bqngd,bknd->bngqk', qb, k).astype(jnp.float32) * scale
        p = jax.nn.softmax(sc, axis=-1).astype(v.dtype)
        return jnp.einsum('bngqk,bknd->bqngd', p, v)

    return _sweep(block, qg).reshape(b, s, h * d)


def _short_conv(gate_b, gate_c, hc, w):
    u = gate_c * hc
    y = lax.conv_general_dilated(u, w[:, None, :].astype(u.dtype), window_strides=(1,),
                                 padding=((1, 1),), dimension_numbers=('NWC', 'WIO', 'NWC'),
                                 feature_group_count=u.shape[-1])
    return gate_b * y


def _hier_moe(h, wg, bg, we, be, w1, w3, w2):
    bsz, s, d = h.shape
    xt = h.reshape(-1, d)
    n = xt.shape[0]
    g_logits = (xt @ wg + bg).astype(jnp.float32)
    p_group = jax.nn.softmax(g_logits, axis=-1)
    _, gidx = lax.top_k(g_logits, 1)
    p_sel = jnp.take_along_axis(p_group, gidx, axis=1)
    e_logits = (xt @ we + be).astype(jnp.float32).reshape(n, N_GROUPS, EXPERTS_PER_GROUP)
    sel_idx = jnp.broadcast_to(gidx[:, :, None], (n, 1, EXPERTS_PER_GROUP))
    sel = jnp.take_along_axis(e_logits, sel_idx, axis=1)[:, 0]
    top_vals, top_idx = lax.top_k(sel, TOP_K)
    gates = jax.nn.softmax(top_vals, axis=-1) * p_sel
    eid = gidx * EXPERTS_PER_GROUP + top_idx
    nk = n * TOP_K
    flat_e = eid.reshape(-1)
    flat_g = gates.reshape(-1)
    flat_tok = jnp.arange(nk, dtype=jnp.int32) // TOP_K
    order = jnp.argsort(flat_e)
    e_s = flat_e[order]
    tok_s = flat_tok[order]
    g_s = flat_g[order]
    counts = jnp.bincount(flat_e, length=N_EXPERTS)
    starts = jnp.cumsum(counts) - counts
    pcounts = (counts + EXPERT_BLOCK - 1) // EXPERT_BLOCK * EXPERT_BLOCK
    pends = jnp.cumsum(pcounts)
    pstarts = pends - pcounts
    dest = pstarts[e_s] + jnp.arange(nk, dtype=jnp.int32) - starts[e_s]
    cap = (nk + N_EXPERTS * (EXPERT_BLOCK - 1) + EXPERT_BLOCK - 1) // EXPERT_BLOCK * EXPERT_BLOCK
    nb = cap // EXPERT_BLOCK
    buf = jnp.zeros((cap, d), xt.dtype).at[dest].set(xt[tok_s])
    block_e = jnp.minimum(jnp.searchsorted(pends, jnp.arange(nb, dtype=pends.dtype) * EXPERT_BLOCK,
                                           side='right'), N_EXPERTS - 1)

    def expert_block(args):
        xb, e = args
        return (jax.nn.silu(xb @ w1[e]) * (xb @ w3[e])) @ w2[e]

    ybuf = lax.map(expert_block, (buf.reshape(nb, EXPERT_BLOCK, d), block_e)).reshape(cap, d)
    contrib = ybuf[dest] * g_s[:, None].astype(xt.dtype)
    out = jnp.zeros_like(xt).at[tok_s].add(contrib)
    return out.reshape(bsz, s, d)


def _trunk(x, c, w_ada, b_ada, norm1, norm2, w_in, w_out, diff_lambda, diff_subln, q_norm, k_norm,
           conv_w, router_group_w, router_group_b, router_expert_w, router_expert_b,
           expert_w1, expert_w3, expert_w2, final_norm):
    b, s, _ = x.shape
    rows = s // GRID_W
    pos = jnp.arange(s)
    cos1, sin1 = _rope_angles(pos, A_QK_DIM)
    row_pos = jnp.repeat(jnp.arange(rows), GRID_W)
    col_pos = jnp.tile(jnp.arange(GRID_W), rows)
    cos_r, sin_r = _rope_angles(row_pos, HEAD_DIM // 2)
    cos_c, sin_c = _rope_angles(col_pos, HEAD_DIM // 2)
    split_idx = [int(v) for v in np.cumsum(IN_SPLIT_WIDTHS)[:-1]]
    for l in range(DEPTH):
        lambda_init = 0.8 - 0.6 * math.exp(-0.3 * l)
        mod = (jax.nn.silu(c) @ w_ada[l] + b_ada[l])[:, None, :]
        sh1, sc1, g1, sh2, sc2, g2 = jnp.split(mod, 6, axis=-1)
        h = _rmsnorm(x, norm1[l]) * (1.0 + sc1) + sh1
        proj = h @ w_in[l]
        aq, ak, av, bq, bk, bv, cb, cc, ch = jnp.split(proj, split_idx, axis=-1)
        out_a = _diff_attention(aq.reshape(b, s, A_HEADS, 2, A_QK_DIM),
                                ak.reshape(b, s, A_HEADS, 2, A_QK_DIM),
                                av.reshape(b, s, A_HEADS, HEAD_DIM),
                                diff_lambda[l], diff_subln[l], lambda_init, cos1, sin1)
        out_b = _gqa_axial(bq.reshape(b, s, B_HEADS, HEAD_DIM),
                           bk.reshape(b, s, B_KV_HEADS, HEAD_DIM),
                           bv.reshape(b, s, B_KV_HEADS, HEAD_DIM),
                           q_norm[l], k_norm[l], cos_r, sin_r, cos_c, sin_c)
        out_c = _short_conv(cb, cc, ch, conv_w[l])
        mix = jnp.concatenate([out_a, out_b, out_c], axis=-1)
        x = x + g1 * (mix @ w_out[l])
        h = _rmsnorm(x, norm2[l]) * (1.0 + sc2) + sh2
        x = x + g2 * _hier_moe(h, router_group_w[l], router_group_b[l], router_expert_w[l],
                               router_expert_b[l], expert_w1[l], expert_w3[l], expert_w2[l])
    return _rmsnorm(x, final_norm)


def setup_inputs(seed: int = 0) -> dict:
    key = jax.random.key(seed)
    ks = jax.random.split(key, 24)
    f32 = jnp.float32
    nrm = lambda k, shape: jax.random.normal(k, shape, f32)
    d = D_MODEL
    return {
        "x_prompt": nrm(ks[0], (BATCH, SEQ, d)),
        "x_sample": nrm(ks[1], (DEC_BATCH, DEC_SEQ, d)),
        "c_prompt": nrm(ks[2], (BATCH, d)),
        "c_sample": nrm(ks[3], (DEC_BATCH, d)),
        "w_ada": nrm(ks[4], (DEPTH, d, 6 * d)) * (0.5 * d ** -0.5),
        "b_ada": nrm(ks[5], (DEPTH, 6 * d)) * 0.02,
        "norm1": 1.0 + 0.05 * nrm(ks[6], (DEPTH, d)),
        "norm2": 1.0 + 0.05 * nrm(ks[7], (DEPTH, d)),
        "w_in": nrm(ks[8], (DEPTH, d, IN_COLS)) * d ** -0.5,
        "w_out": nrm(ks[9], (DEPTH, D_MIX, d)) * D_MIX ** -0.5,
        "diff_lambda": nrm(ks[10], (DEPTH, 4, A_QK_DIM)) * 0.1,
        "diff_subln": 1.0 + 0.05 * nrm(ks[11], (DEPTH, HEAD_DIM)),
        "q_norm": 1.0 + 0.05 * nrm(ks[12], (DEPTH, HEAD_DIM)),
        "k_norm": 1.0 + 0.05 * nrm(ks[13], (DEPTH, HEAD_DIM)),
        "conv_w": nrm(ks[14], (DEPTH, CONV_WIDTH, C_WIDTH)) * CONV_WIDTH ** -0.5,
        "router_group_w": nrm(ks[15], (DEPTH, d, N_GROUPS)) * d ** -0.5,
        "router_group_b": nrm(ks[16], (DEPTH, N_GROUPS)) * 0.01,
        "router_expert_w": nrm(ks[17], (DEPTH, d, N_EXPERTS)) * d ** -0.5,
        "router_expert_b": nrm(ks[18], (DEPTH, N_EXPERTS)) * 0.01,
        "expert_w1": nrm(ks[19], (DEPTH, N_EXPERTS, d, D_EXPERT)) * d ** -0.5,
        "expert_w3": nrm(ks[20], (DEPTH, N_EXPERTS, d, D_EXPERT)) * d ** -0.5,
        "expert_w2": nrm(ks[21], (DEPTH, N_EXPERTS, D_EXPERT, d)) * D_EXPERT ** -0.5,
        "final_norm": 1.0 + 0.05 * nrm(ks[22], (d,)),
    }


def reference(x_prompt, x_sample, c_prompt, c_sample, w_ada, b_ada, norm1, norm2, w_in, w_out,
              diff_lambda, diff_subln, q_norm, k_norm, conv_w, router_group_w, router_group_b,
              router_expert_w, router_expert_b, expert_w1, expert_w3, expert_w2, final_norm):
    y_prompt = _trunk(x_prompt, c_prompt, w_ada, b_ada, norm1, norm2, w_in, w_out, diff_lambda,
                      diff_subln, q_norm, k_norm, conv_w, router_group_w, router_group_b,
                      router_expert_w, router_expert_b, expert_w1, expert_w3, expert_w2, final_norm)
    y_sample = _trunk(x_sample, c_sample, w_ada, b_ada, norm1, norm2, w_in, w_out, diff_lambda,
                      diff_subln, q_norm, k_norm, conv_w, router_group_w, router_group_b,
                      router_expert_w, router_expert_b, expert_w1, expert_w3, expert_w2, final_norm)
    return (y_prompt, y_sample)
```

```python
import functools
import math

import numpy as np
import jax
import jax.numpy as jnp
from jax import lax
from jax.experimental import pallas as pl
from jax.experimental.pallas import tpu as pltpu

F32 = jnp.float32
BF16 = jnp.bfloat16

D_MODEL = 1024
HEAD_DIM = 64
A_HEADS = 4
A_QK_DIM = 32
A_WIDTH = 256
B_HEADS = 8
B_KV_HEADS = 2
B_WIDTH = 512
B_KV_WIDTH = 128
C_WIDTH = 256
IN_COLS = 2304
N_GROUPS = 4
EXPERTS_PER_GROUP = 8
N_EXPERTS = 32
D_EXPERT = 512
GRID_W = 64
ROPE_THETA = 10000.0
EPS = 1e-6
LANES = 128
LOG2E = 1.4426950408889634

VMEM_LIMIT = 56 * 1024 * 1024

TOKEN_TILE = 256
Q_TILE = 256
KV_TILE = 512
CONV_TILE = 512
EXPERT_ROWS = 256
DMA_TILE = 512


def _cparams(sem):
    return pltpu.CompilerParams(dimension_semantics=sem, vmem_limit_bytes=VMEM_LIMIT)


def _ada_kernel(c_ref, w_ref, b_ref, o_ref):
    c = c_ref[...]
    s = c * jax.nn.sigmoid(c)
    o_ref[...] = jnp.dot(s, w_ref[...], precision=lax.Precision.HIGHEST,
                         preferred_element_type=F32) + b_ref[...]


def _ada_mod(c_all, w_ada, b_ada):
    depth, d, six_d = w_ada.shape
    bt = c_all.shape[0]
    nt = six_d // d
    return pl.pallas_call(
        _ada_kernel,
        grid=(depth, nt),
        in_specs=[pl.BlockSpec((bt, d), lambda l, j: (0, 0)),
                  pl.BlockSpec((None, d, d), lambda l, j: (l, 0, j)),
                  pl.BlockSpec((None, 1, d), lambda l, j: (l, 0, j))],
        out_specs=pl.BlockSpec((None, bt, d), lambda l, j: (l, 0, j)),
        out_shape=jax.ShapeDtypeStruct((depth, bt, six_d), F32),
        compiler_params=_cparams(("arbitrary", "arbitrary")),
        name="ada_mod",
    )(c_all, w_ada, b_ada.reshape(depth, 1, six_d))


def _swap16(x):
    lane = lax.broadcasted_iota(jnp.int32, x.shape, 1)
    return jnp.where((lane % 32) < 16, pltpu.roll(x, LANES - 16, 1), pltpu.roll(x, 16, 1))


def _rope128(x, cos, sin_signed):
    return x * cos + _swap16(x) * sin_signed


def _head_rms(x, gain):
    lane = lax.broadcasted_iota(jnp.int32, x.shape, 1)
    lo = lane < HEAD_DIM
    sq = x * x
    s_lo = jnp.sum(jnp.where(lo, sq, 0.0), axis=-1, keepdims=True)
    s_hi = jnp.sum(jnp.where(lo, 0.0, sq), axis=-1, keepdims=True)
    ms = jnp.where(lo, s_lo, s_hi) * (1.0 / HEAD_DIM)
    return x * lax.rsqrt(ms + EPS) * gain


def _rms_mod(x, gain, scale, shift):
    ms = jnp.mean(x * x, axis=-1, keepdims=True)
    return (x * lax.rsqrt(ms + EPS) * gain) * (1.0 + scale) + shift


def _inproj_kernel(with_moe, *refs):
    if with_moe:
        (x_ref, y_ref, gt_ref, modp_ref, mod_ref, g_ref, w_ref, qn_ref, kn_ref, ca_ref, sa_ref, cb_ref, sb_ref,
         xo_ref, qa_ref, ka_ref, va_ref, qb_ref, kb_ref, vb_ref, cp_ref) = refs
    else:
        (x_ref, mod_ref, g_ref, w_ref, qn_ref, kn_ref, ca_ref, sa_ref, cb_ref, sb_ref,
         qa_ref, ka_ref, va_ref, qb_ref, kb_ref, vb_ref, cp_ref) = refs
    d = D_MODEL
    x = x_ref[...]
    if with_moe:
        gt = gt_ref[...]
        x = x + modp_ref[5:6, :] * (gt[:, 0:1] * y_ref[:, :d] + gt[:, 1:2] * y_ref[:, d:])
        xo_ref[...] = x
    h = _rms_mod(x, g_ref[...], mod_ref[1:2, :], mod_ref[0:1, :])
    proj = jnp.dot(h.astype(BF16), w_ref[...], preferred_element_type=F32)

    ca, sa, cb, sb = ca_ref[...], sa_ref[...], cb_ref[...], sb_ref[...]
    qa_scale = (A_QK_DIM ** -0.5) * LOG2E
    qb_scale = (HEAD_DIM ** -0.5) * LOG2E
    for i in range(A_WIDTH // LANES):
        sl = slice(i * LANES, (i + 1) * LANES)
        qa_ref[:, sl] = (_rope128(proj[:, sl], ca, sa) * qa_scale).astype(BF16)
        ka_ref[:, sl] = _rope128(proj[:, A_WIDTH + i * LANES:A_WIDTH + (i + 1) * LANES], ca, sa).astype(BF16)
    va_ref[...] = proj[:, 2 * A_WIDTH:3 * A_WIDTH].astype(BF16)
    o = 3 * A_WIDTH
    for j in range(B_WIDTH // LANES):
        q = _head_rms(proj[:, o + j * LANES:o + (j + 1) * LANES], qn_ref[...])
        qb_ref[:, j * LANES:(j + 1) * LANES] = (_rope128(q, cb, sb) * qb_scale).astype(BF16)
    o += B_WIDTH
    k = _head_rms(proj[:, o:o + B_KV_WIDTH], kn_ref[...])
    kb_ref[...] = _rope128(k, cb, sb).astype(BF16)
    o += B_KV_WIDTH
    vb_ref[...] = proj[:, o:o + B_KV_WIDTH].astype(BF16)
    o += B_KV_WIDTH
    cp_ref[:, :C_WIDTH] = proj[:, o:o + C_WIDTH]
    cp_ref[:, C_WIDTH:] = proj[:, o + C_WIDTH:o + 2 * C_WIDTH] * proj[:, o + 2 * C_WIDTH:o + 3 * C_WIDTH]


def _inproj(x, moe, mod_l, g1, w, qn, kn, tabs, seq):
    n, d = x.shape
    tm = TOKEN_TILE
    tiles_per_seq = seq // tm
    row = lambda i: (i, 0)
    const = lambda i: (0, 0)
    pos = lambda i: (i % tiles_per_seq, 0)
    in_specs = [pl.BlockSpec((tm, d), row)]
    args = [x]
    if moe is not None:
        y2, gates, mod_prev = moe
        in_specs += [pl.BlockSpec((tm, 2 * d), row), pl.BlockSpec((tm, 2), row),
                     pl.BlockSpec((None, 6, d), lambda i: (i // tiles_per_seq, 0, 0))]
        args += [y2, gates, mod_prev]
    in_specs += [pl.BlockSpec((None, 6, d), lambda i: (i // tiles_per_seq, 0, 0)),
                 pl.BlockSpec((1, d), const),
                 pl.BlockSpec((d, IN_COLS), const),
                 pl.BlockSpec((1, LANES), const), pl.BlockSpec((1, LANES), const)]
    in_specs += [pl.BlockSpec((tm, LANES), pos)] * 4
    args += [mod_l, g1, w, qn, kn, *tabs]
    widths = [(A_WIDTH, BF16), (A_WIDTH, BF16), (A_WIDTH, BF16), (B_WIDTH, BF16), (B_KV_WIDTH, BF16),
              (B_KV_WIDTH, BF16), (2 * C_WIDTH, F32)]
    out_specs = [pl.BlockSpec((tm, wd), row) for wd, _ in widths]
    out_shape = [jax.ShapeDtypeStruct((n, wd), dt) for wd, dt in widths]
    if moe is not None:
        out_specs = [pl.BlockSpec((tm, d), row)] + out_specs
        out_shape = [jax.ShapeDtypeStruct((n, d), F32)] + out_shape
    outs = pl.pallas_call(
        functools.partial(_inproj_kernel, moe is not None),
        grid=(n // tm,),
        in_specs=in_specs, out_specs=out_specs, out_shape=out_shape,
        compiler_params=_cparams(("arbitrary",)),
        name="inproj",
    )(*args)
    if moe is None:
        outs = [x] + list(outs)
    return outs


def _flash(qs, k_ref, v_ref, col, seq):
    rows = qs.shape[0]
    tk = KV_TILE
    sl = slice(col * LANES, (col + 1) * LANES)

    def body(c, carry):
        m, l, acc = carry
        r0 = pl.multiple_of(c * tk, tk)
        k = k_ref[pl.ds(r0, tk), sl]
        v = v_ref[pl.ds(r0, tk), sl]
        s = lax.dot_general(qs, k, (((1,), (1,)), ((), ())), preferred_element_type=F32)
        m_new = jnp.maximum(m, jnp.max(s, axis=-1, keepdims=True))
        alpha = jnp.exp2(m - m_new)
        p = jnp.exp2(s - m_new)
        l = alpha * l + jnp.sum(p, axis=-1, keepdims=True)
        acc = alpha * acc + jnp.dot(p.astype(BF16), v, preferred_element_type=F32)
        return m_new, l, acc

    init = (jnp.full((rows, 1), -jnp.inf, F32), jnp.zeros((rows, 1), F32), jnp.zeros((rows, LANES), F32))
    _, l, acc = lax.fori_loop(0, seq // tk, body, init)
    return acc, l


def _diff_attn_kernel(lambda_init, seq, q_ref, k_ref, v_ref, lam_ref, g_ref, o_ref):
    tq = q_ref.shape[0]
    lv = lam_ref[...]
    lam = (jnp.exp(jnp.sum(lv[0:1] * lv[1:2], keepdims=True))
           - jnp.exp(jnp.sum(lv[2:3] * lv[3:4], keepdims=True)) + lambda_init)
    lane = lax.broadcasted_iota(jnp.int32, (tq, LANES), 1)
    lo = lane < HEAD_DIM
    for i in range(A_WIDTH // LANES):
        q = q_ref[:, i * LANES:(i + 1) * LANES]
        zero = jnp.zeros_like(q)
        qs = jnp.concatenate([jnp.where(lane // A_QK_DIM == j, q, zero) for j in range(4)], axis=0)
        acc, l = _flash(qs, k_ref, v_ref, i, seq)
        o = acc / l
        o_even = o[0:tq] - lam * o[tq:2 * tq]
        o_odd = o[2 * tq:3 * tq] - lam * o[3 * tq:4 * tq]
        oo = jnp.where(lo, o_even, o_odd)
        o_ref[:, i * LANES:(i + 1) * LANES] = (_head_rms(oo, g_ref[...]) * (1.0 - lambda_init)).astype(BF16)


def _diff_attn(qa, ka, va, lam_vecs, subln_tiled, lambda_init, batch, seq):
    tq = Q_TILE
    q3, k3, v3 = (a.reshape(batch, seq, A_WIDTH) for a in (qa, ka, va))
    out = pl.pallas_call(
        functools.partial(_diff_attn_kernel, lambda_init, seq),
        grid=(batch, seq // tq),
        in_specs=[pl.BlockSpec((None, tq, A_WIDTH), lambda b, i: (b, i, 0)),
                  pl.BlockSpec((None, seq, A_WIDTH), lambda b, i: (b, 0, 0)),
                  pl.BlockSpec((None, seq, A_WIDTH), lambda b, i: (b, 0, 0)),
                  pl.BlockSpec((4, A_QK_DIM), lambda b, i: (0, 0)),
                  pl.BlockSpec((1, LANES), lambda b, i: (0, 0))],
        out_specs=pl.BlockSpec((None, tq, A_WIDTH), lambda b, i: (b, i, 0)),
        out_shape=jax.ShapeDtypeStruct((batch, seq, A_WIDTH), BF16),
        compiler_params=_cparams(("arbitrary", "arbitrary")),
        name="diff_attn",
    )(q3, k3, v3, lam_vecs, subln_tiled)
    return out.reshape(batch * seq, A_WIDTH)


def _gqa_kernel(seq, q_ref, k_ref, v_ref, o_ref):
    tq = q_ref.shape[0]
    lane = lax.broadcasted_iota(jnp.int32, (tq, LANES), 1)
    lo = lane < HEAD_DIM
    for j in range(B_WIDTH // LANES):
        q = q_ref[:, j * LANES:(j + 1) * LANES]
        zero = jnp.zeros_like(q)
        qs = jnp.concatenate([jnp.where(lo, q, zero), jnp.where(lo, zero, q)], axis=0)
        acc, l = _flash(qs, k_ref, v_ref, 0, seq)
        o = acc / l
        o_ref[:, j * LANES:(j + 1) * LANES] = jnp.where(lo, o[0:tq], o[tq:2 * tq]).astype(BF16)


def _gqa_attn(qb, kb, vb, batch, seq):
    tq = Q_TILE
    q3 = qb.reshape(batch, seq, B_WIDTH)
    k3 = kb.reshape(batch, seq, B_KV_WIDTH)
    v3 = vb.reshape(batch, seq, B_KV_WIDTH)
    out = pl.pallas_call(
        functools.partial(_gqa_kernel, seq),
        grid=(batch, seq // tq),
        in_specs=[pl.BlockSpec((None, tq, B_WIDTH), lambda b, i: (b, i, 0)),
                  pl.BlockSpec((None, seq, B_KV_WIDTH), lambda b, i: (b, 0, 0)),
                  pl.BlockSpec((None, seq, B_KV_WIDTH), lambda b, i: (b, 0, 0))],
        out_specs=pl.BlockSpec((None, tq, B_WIDTH), lambda b, i: (b, i, 0)),
        out_shape=jax.ShapeDtypeStruct((batch, seq, B_WIDTH), BF16),
        compiler_params=_cparams(("arbitrary", "arbitrary")),
        name="gqa_attn",
    )(q3, k3, v3)
    return out.reshape(batch * seq, B_WIDTH)


def _conv_kernel(cp_ref, prev_ref, next_ref, w_ref, o_ref):
    i = pl.program_id(1)
    last = pl.num_programs(1) - 1
    tc = cp_ref.shape[0]
    gate = cp_ref[:, :C_WIDTH]
    u = cp_ref[:, C_WIDTH:]
    row = lax.broadcasted_iota(jnp.int32, u.shape, 0)
    before = jnp.where(i == 0, 0.0, prev_ref[7:8, C_WIDTH:])
    after = jnp.where(i == last, 0.0, next_ref[0:1, C_WIDTH:])
    u_prev = jnp.where(row == 0, before, pltpu.roll(u, 1, 0))
    u_next = jnp.where(row == tc - 1, after, pltpu.roll(u, tc - 1, 0))
    w = w_ref[...]
    o_ref[...] = (gate * (w[0:1] * u_prev + w[1:2] * u + w[2:3] * u_next)).astype(BF16)


def _short_conv(cp, conv_w, batch, seq):
    tc = CONV_TILE
    sub = 8
    cp3 = cp.reshape(batch, seq, 2 * C_WIDTH)
    nsub = seq // sub
    out = pl.pallas_call(
        _conv_kernel,
        grid=(batch, seq // tc),
        in_specs=[pl.BlockSpec((None, tc, 2 * C_WIDTH), lambda b, i: (b, i, 0)),
                  pl.BlockSpec((None, sub, 2 * C_WIDTH),
                               lambda b, i: (b, jnp.maximum(i * (tc // sub) - 1, 0), 0)),
                  pl.BlockSpec((None, sub, 2 * C_WIDTH),
                               lambda b, i: (b, jnp.minimum((i + 1) * (tc // sub), nsub - 1), 0)),
                  pl.BlockSpec((3, C_WIDTH), lambda b, i: (0, 0))],
        out_specs=pl.BlockSpec((None, tc, C_WIDTH), lambda b, i: (b, i, 0)),
        out_shape=jax.ShapeDtypeStruct((batch, seq, C_WIDTH), BF16),
        compiler_params=_cparams(("arbitrary", "arbitrary")),
        name="short_conv",
    )(cp3, cp3, cp3, conv_w)
    return out.reshape(batch * seq, C_WIDTH)


def _outproj_kernel(oa_ref, ob_ref, oc_ref, x_ref, mod_ref, g_ref, wa_ref, wb_ref, wc_ref,
                    rhi_ref, rlo_ref, rb_ref, tri_ref,
                    x1_ref, h_ref, eid_ref, gate_ref, rank_ref, cnt_ref, run_ref):
    i = pl.program_id(0)
    tm = x_ref.shape[0]

    @pl.when(i == 0)
    def _():
        run_ref[...] = jnp.zeros_like(run_ref)

    mix = (jnp.dot(oa_ref[...], wa_ref[...], preferred_element_type=F32)
           + jnp.dot(ob_ref[...], wb_ref[...], preferred_element_type=F32)
           + jnp.dot(oc_ref[...], wc_ref[...], preferred_element_type=F32))
    x1 = x_ref[...] + mod_ref[2:3, :] * mix
    x1_ref[...] = x1
    h = _rms_mod(x1, g_ref[...], mod_ref[4:5, :], mod_ref[3:4, :])
    h_ref[...] = h

    hi = h.astype(BF16)
    lo = (h - hi.astype(F32)).astype(BF16)
    logits = (jnp.dot(hi, rhi_ref[...], preferred_element_type=F32)
              + jnp.dot(hi, rlo_ref[...], preferred_element_type=F32)
              + jnp.dot(lo, rhi_ref[...], preferred_element_type=F32)) + rb_ref[...]
    lt = logits.T
    r8 = lax.broadcasted_iota(jnp.int32, (8, tm), 0)
    neg = -jnp.inf
    gl = jnp.where(r8 < N_GROUPS, lt[0:8], neg)
    gmax = jnp.max(gl, axis=0, keepdims=True)
    gidx = jnp.min(jnp.where(gl == gmax, r8, 8), axis=0, keepdims=True)
    p_sel = 1.0 / jnp.sum(jnp.exp(gl - gmax), axis=0, keepdims=True)
    sel = jnp.zeros((8, tm), F32)
    for g in range(N_GROUPS):
        sel = sel + jnp.where(gidx == g, lt[8 + 8 * g:16 + 8 * g], 0.0)
    v1 = jnp.max(sel, axis=0, keepdims=True)
    i1 = jnp.min(jnp.where(sel == v1, r8, 8), axis=0, keepdims=True)
    sel2 = jnp.where(r8 == i1, neg, sel)
    v2 = jnp.max(sel2, axis=0, keepdims=True)
    i2 = jnp.min(jnp.where(sel2 == v2, r8, 8), axis=0, keepdims=True)
    e = jnp.exp(v2 - v1)
    gate_ref[0:1, :] = p_sel / (1.0 + e)
    gate_ref[1:2, :] = p_sel * e / (1.0 + e)
    eid0 = gidx * EXPERTS_PER_GROUP + i1
    eid1 = gidx * EXPERTS_PER_GROUP + i2
    eid_ref[0:1, :] = eid0
    eid_ref[1:2, :] = eid1

    r32 = lax.broadcasted_iota(jnp.int32, (N_EXPERTS, tm), 0)
    oh0 = r32 == eid0
    oh1 = r32 == eid1
    cnt = jnp.where(oh0, 1.0, 0.0) + jnp.where(oh1, 1.0, 0.0)
    before = jnp.dot(cnt.astype(BF16), tri_ref[...], preferred_element_type=F32) + run_ref[:, 0:1]
    rank_ref[0:1, :] = jnp.sum(jnp.where(oh0, before, 0.0), axis=0, keepdims=True).astype(jnp.int32)
    rank_ref[1:2, :] = jnp.sum(jnp.where(oh1, before, 0.0), axis=0, keepdims=True).astype(jnp.int32)
    run = run_ref[...] + jnp.sum(cnt, axis=1, keepdims=True)
    run_ref[...] = run
    cnt_ref[...] = run


def _outproj(oa, ob, oc, x, mod_l, g2, wa, wb, wc, rhi, rlo, rb, tri, seq):
    n, d = x.shape
    tm = TOKEN_TILE
    tiles_per_seq = seq // tm
    row = lambda i: (i, 0)
    col = lambda i: (0, i)
    const = lambda i: (0, 0)
    return pl.pallas_call(
        _outproj_kernel,
        grid=(n // tm,),
        in_specs=[pl.BlockSpec((tm, A_WIDTH), row), pl.BlockSpec((tm, B_WIDTH), row),
                  pl.BlockSpec((tm, C_WIDTH), row), pl.BlockSpec((tm, d), row),
                  pl.BlockSpec((None, 6, d), lambda i: (i // tiles_per_seq, 0, 0)),
                  pl.BlockSpec((1, d), const),
                  pl.BlockSpec((A_WIDTH, d), const), pl.BlockSpec((B_WIDTH, d), const),
                  pl.BlockSpec((C_WIDTH, d), const),
                  pl.BlockSpec((d, LANES), const), pl.BlockSpec((d, LANES), const),
                  pl.BlockSpec((1, LANES), const), pl.BlockSpec((tm, tm), const)],
        out_specs=[pl.BlockSpec((tm, d), row), pl.BlockSpec((tm, d), row),
                   pl.BlockSpec((2, tm), col), pl.BlockSpec((2, tm), col), pl.BlockSpec((2, tm), col),
                   pl.BlockSpec((N_EXPERTS, LANES), const)],
        out_shape=[jax.ShapeDtypeStruct((n, d), F32), jax.ShapeDtypeStruct((n, d), F32),
                   jax.ShapeDtypeStruct((2, n), jnp.int32), jax.ShapeDtypeStruct((2, n), F32),
                   jax.ShapeDtypeStruct((2, n), jnp.int32),
                   jax.ShapeDtypeStruct((N_EXPERTS, LANES), F32)],
        scratch_shapes=[pltpu.VMEM((N_EXPERTS, LANES), F32)],
        compiler_params=_cparams(("arbitrary",)),
        name="outproj_router",
    )(oa, ob, oc, x, mod_l, g2, wa, wb, wc, rhi, rlo, rb, tri)


def _row_copy_wait(src_ref, dst_ref, sem, rows):
    pltpu.make_async_copy(src_ref.at[pl.ds(0, rows)], dst_ref.at[pl.ds(0, rows)], sem).wait()


def _dispatch_kernel(dest_ref, h_ref, buf_in_ref, buf_ref, sem):
    del buf_in_ref
    td = dest_ref.shape[1]
    base = pl.program_id(0) * td

    def body(t, carry):
        for k in range(2):
            pltpu.make_async_copy(h_ref.at[pl.ds(base + t, 1)],
                                  buf_ref.at[pl.ds(dest_ref[k, t], 1)], sem).start()
        return carry

    lax.fori_loop(0, td, body, 0)
    _row_copy_wait(h_ref, buf_ref, sem, 2 * td)


def _dispatch(dest, h, cap):
    n, d = h.shape
    td = DMA_TILE
    return pl.pallas_call(
        _dispatch_kernel,
        grid=(n // td,),
        in_specs=[pl.BlockSpec((2, td), lambda i: (0, i), memory_space=pltpu.SMEM),
                  pl.BlockSpec(memory_space=pl.ANY), pl.BlockSpec(memory_space=pl.ANY)],
        out_specs=pl.BlockSpec(memory_space=pl.ANY),
        out_shape=jax.ShapeDtypeStruct((cap, d), F32),
        scratch_shapes=[pltpu.SemaphoreType.DMA(())],
        input_output_aliases={2: 0},
        compiler_params=pltpu.CompilerParams(dimension_semantics=("arbitrary",), has_side_effects=True),
        name="moe_dispatch",
    )(dest, h, jnp.zeros((cap, d), F32))


def _combine_kernel(dest_ref, y_ref, o_ref, sem):
    td = dest_ref.shape[1]
    d = y_ref.shape[1]
    base = pl.program_id(0) * td

    def body(t, carry):
        for k in range(2):
            pltpu.make_async_copy(y_ref.at[pl.ds(dest_ref[k, t], 1)],
                                  o_ref.at[pl.ds(base + t, 1), pl.ds(k * d, d)], sem).start()
        return carry

    lax.fori_loop(0, td, body, 0)
    _row_copy_wait(y_ref, y_ref, sem, 2 * td)


def _combine(dest, y, n):
    d = y.shape[1]
    td = DMA_TILE
    return pl.pallas_call(
        _combine_kernel,
        grid=(n // td,),
        in_specs=[pl.BlockSpec((2, td), lambda i: (0, i), memory_space=pltpu.SMEM),
                  pl.BlockSpec(memory_space=pl.ANY)],
        out_specs=pl.BlockSpec(memory_space=pl.ANY),
        out_shape=jax.ShapeDtypeStruct((n, 2 * d), F32),
        scratch_shapes=[pltpu.SemaphoreType.DMA(())],
        compiler_params=pltpu.CompilerParams(dimension_semantics=("arbitrary",), has_side_effects=True),
        name="moe_combine",
    )(dest, y)


def _expert_kernel(be_ref, nu_ref, x_ref, w1_ref, w3_ref, w2_ref, y_ref, w1b, w3b, w2b):
    i = pl.program_id(0)
    changed = jnp.logical_or(i == 0, be_ref[i] != be_ref[jnp.maximum(i - 1, 0)])

    @pl.when(changed)
    def _():
        w1b[...] = w1_ref[...].astype(BF16)
        w3b[...] = w3_ref[...].astype(BF16)
        w2b[...] = w2_ref[...].astype(BF16)

    @pl.when(i < nu_ref[0])
    def _():
        x = x_ref[...].astype(BF16)
        a = jnp.dot(x, w1b[...], preferred_element_type=F32)
        b = jnp.dot(x, w3b[...], preferred_element_type=F32)
        act = (a * jax.nn.sigmoid(a) * b).astype(BF16)
        y_ref[...] = jnp.dot(act, w2b[...], preferred_element_type=F32)

    @pl.when(i >= nu_ref[0])
    def _():
        y_ref[...] = jnp.zeros_like(y_ref)


def _experts(block_e, n_used, buf, w1, w3, w2, layer):
    cap, d = buf.shape
    rb = EXPERT_ROWS
    de = w1.shape[-1]
    grid_spec = pltpu.PrefetchScalarGridSpec(
        num_scalar_prefetch=2,
        grid=(cap // rb,),
        in_specs=[pl.BlockSpec((rb, d), lambda i, be, nu: (i, 0)),
                  pl.BlockSpec((None, None, d, de), lambda i, be, nu: (layer, be[i], 0, 0)),
                  pl.BlockSpec((None, None, d, de), lambda i, be, nu: (layer, be[i], 0, 0)),
                  pl.BlockSpec((None, None, de, d), lambda i, be, nu: (layer, be[i], 0, 0))],
        out_specs=pl.BlockSpec((rb, d), lambda i, be, nu: (i, 0)),
        scratch_shapes=[pltpu.VMEM((d, de), BF16), pltpu.VMEM((d, de), BF16), pltpu.VMEM((de, d), BF16)],
    )
    return pl.pallas_call(
        _expert_kernel,
        grid_spec=grid_spec,
        out_shape=jax.ShapeDtypeStruct((cap, d), F32),
        compiler_params=_cparams(("arbitrary",)),
        name="moe_experts",
    )(block_e, n_used, buf, w1, w3, w2)


def _final_kernel(x_ref, y_ref, gt_ref, mod_ref, g_ref, o_ref):
    d = D_MODEL
    gt = gt_ref[...]
    x = x_ref[...] + mod_ref[5:6, :] * (gt[:, 0:1] * y_ref[:, :d] + gt[:, 1:2] * y_ref[:, d:])
    ms = jnp.mean(x * x, axis=-1, keepdims=True)
    o_ref[...] = x * lax.rsqrt(ms + EPS) * g_ref[...]


def _final(x1, y2, gates, mod_l, g, seq):
    n, d = x1.shape
    tm = TOKEN_TILE
    tiles_per_seq = seq // tm
    row = lambda i: (i, 0)
    return pl.pallas_call(
        _final_kernel,
        grid=(n // tm,),
        in_specs=[pl.BlockSpec((tm, d), row), pl.BlockSpec((tm, 2 * d), row), pl.BlockSpec((tm, 2), row),
                  pl.BlockSpec((None, 6, d), lambda i: (i // tiles_per_seq, 0, 0)),
                  pl.BlockSpec((1, d), lambda i: (0, 0))],
        out_specs=pl.BlockSpec((tm, d), row),
        out_shape=jax.ShapeDtypeStruct((n, d), F32),
        compiler_params=_cparams(("arbitrary",)),
        name="final_norm",
    )(x1, y2, gates, mod_l, g)


def _rope_tables(seq):
    freqs = ROPE_THETA ** (-np.arange(0, A_QK_DIM, 2, dtype=np.float32) / A_QK_DIM)
    lane = np.arange(LANES)
    f = freqs[lane % 16][None, :]
    sign = np.where((lane % 32) < 16, -1.0, 1.0)[None, :].astype(np.float32)
    pos = np.arange(seq, dtype=np.float32)[:, None]
    ang_a = pos * f
    axial = np.where((lane % HEAD_DIM) < 32, np.floor(pos / GRID_W), np.mod(pos, GRID_W)).astype(np.float32)
    ang_b = axial * f
    tabs = (np.cos(ang_a), np.sin(ang_a) * sign, np.cos(ang_b), np.sin(ang_b) * sign)
    return tuple(jnp.asarray(t, dtype=F32) for t in tabs)


def _qb_perm():
    cols = []
    for j in range(B_HEADS // B_KV_HEADS):
        cols += list(range(j * HEAD_DIM, (j + 1) * HEAD_DIM))
        cols += list(range((4 + j) * HEAD_DIM, (5 + j) * HEAD_DIM))
    return np.asarray(cols, dtype=np.int32)


def _trunk(x, mod, batch, seq, p):
    n = batch * seq
    d = D_MODEL
    depth = p["w_in"].shape[0]
    tabs = _rope_tables(seq)
    perm = _qb_perm()
    tri = jnp.asarray(np.triu(np.ones((TOKEN_TILE, TOKEN_TILE), np.float32), 1), dtype=BF16)
    rb = EXPERT_ROWS
    cap = (2 * n + N_EXPERTS * (rb - 1) + rb - 1) // rb * rb
    nb = cap // rb
    x = x.reshape(n, d)
    moe = None
    for l in range(depth):
        lambda_init = 0.8 - 0.6 * math.exp(-0.3 * l)
        mod_l = mod[l].reshape(batch, 6, d)
        w_in = p["w_in"][l]
        qcols = 3 * A_WIDTH + perm
        w_in = jnp.concatenate([w_in[:, :3 * A_WIDTH], w_in[:, qcols], w_in[:, 3 * A_WIDTH + B_WIDTH:]],
                               axis=1).astype(BF16)
        tile2 = lambda v: jnp.tile(v.reshape(1, HEAD_DIM), (1, 2))
        x, qa, ka, va, qb, kb, vb, cp = _inproj(x, moe, mod_l, p["norm1"][l].reshape(1, d), w_in,
                                                tile2(p["q_norm"][l]), tile2(p["k_norm"][l]), tabs, seq)
        oa = _diff_attn(qa, ka, va, p["diff_lambda"][l], tile2(p["diff_subln"][l]), lambda_init, batch, seq)
        ob = _gqa_attn(qb, kb, vb, batch, seq)
        oc = _short_conv(cp, p["conv_w"][l], batch, seq)
        w_out = p["w_out"][l]
        wa = w_out[:A_WIDTH].astype(BF16)
        wb = w_out[A_WIDTH + perm].astype(BF16)
        wc = w_out[A_WIDTH + B_WIDTH:].astype(BF16)
        wr = jnp.zeros((d, LANES), F32)
        wr = wr.at[:, :N_GROUPS].set(p["router_group_w"][l]).at[:, 8:8 + N_EXPERTS].set(p["router_expert_w"][l])
        rhi = wr.astype(BF16)
        rlo = (wr - rhi.astype(F32)).astype(BF16)
        rbias = jnp.zeros((1, LANES), F32)
        rbias = rbias.at[0, :N_GROUPS].set(p["router_group_b"][l]).at[0, 8:8 + N_EXPERTS].set(p["router_expert_b"][l])
        x1, h, eid, gate, rank, cnt = _outproj(oa, ob, oc, x, mod_l, p["norm2"][l].reshape(1, d),
                                               wa, wb, wc, rhi, rlo, rbias, tri, seq)
        counts = cnt[:, 0].astype(jnp.int32)
        pcounts = (counts + rb - 1) // rb * rb
        pends = jnp.cumsum(pcounts)
        pstarts = pends - pcounts
        dest = pstarts[eid] + rank
        block_e = jnp.minimum(jnp.searchsorted(pends, jnp.arange(nb, dtype=jnp.int32) * rb, side="right"),
                              N_EXPERTS - 1).astype(jnp.int32)
        n_used = (pends[-1:] // rb).astype(jnp.int32)
        buf = _dispatch(dest, h, cap)
        y = _experts(block_e, n_used, buf, p["expert_w1"], p["expert_w3"], p["expert_w2"], l)
        y2 = _combine(dest, y, n)
        x = x1
        moe = (y2, gate.T, mod_l)
    out = _final(x, moe[0], moe[1], moe[2], p["final_norm"].reshape(1, d), seq)
    return out.reshape(batch, seq, d)


def kernel(x_prompt, x_sample, c_prompt, c_sample, w_ada, b_ada, norm1, norm2, w_in, w_out, diff_lambda,
           diff_subln, q_norm, k_norm, conv_w, router_group_w, router_group_b, router_expert_w,
           router_expert_b, expert_w1, expert_w3, expert_w2, final_norm):
    p = dict(norm1=norm1, norm2=norm2, w_in=w_in, w_out=w_out, diff_lambda=diff_lambda, diff_subln=diff_subln,
             q_norm=q_norm, k_norm=k_norm, conv_w=conv_w, router_group_w=router_group_w,
             router_group_b=router_group_b, router_expert_w=router_expert_w, router_expert_b=router_expert_b,
             expert_w1=expert_w1, expert_w3=expert_w3, expert_w2=expert_w2, final_norm=final_norm)
    bp, sp, _ = x_prompt.shape
    bs, ss, _ = x_sample.shape
    mod = _ada_mod(jnp.concatenate([c_prompt, c_sample], axis=0), w_ada, b_ada)
    y_prompt = _trunk(x_prompt, mod[:, :bp], bp, sp, p)
    y_sample = _trunk(x_sample, mod[:, bp:], bs, ss, p)
    return (y_prompt, y_sample)
```

```python
import functools
import math

import numpy as np
import jax
import jax.numpy as jnp
from jax import lax
from jax.experimental import pallas as pl
from jax.experimental.pallas import tpu as pltpu

F32 = jnp.float32
BF16 = jnp.bfloat16

D_MODEL = 1024
HEAD_DIM = 64
A_HEADS = 4
A_QK_DIM = 32
A_WIDTH = 256
B_HEADS = 8
B_KV_HEADS = 2
B_WIDTH = 512
B_KV_WIDTH = 128
C_WIDTH = 256
IN_COLS = 2304
N_GROUPS = 4
EXPERTS_PER_GROUP = 8
N_EXPERTS = 32
D_EXPERT = 512
GRID_W = 64
ROPE_THETA = 10000.0
EPS = 1e-6
LANES = 128
LOG2E = 1.4426950408889634

VMEM_LIMIT = 56 * 1024 * 1024

TOKEN_TILE = 256
Q_TILE = 256
KV_TILE = 512
CONV_TILE = 512
EXPERT_ROWS = 256
DMA_TILE = 512


def _cparams(sem):
    return pltpu.CompilerParams(dimension_semantics=sem, vmem_limit_bytes=VMEM_LIMIT)


SUBLANES = 8
LANE_BLOCKS = D_MODEL // LANES
assert LANE_BLOCKS == SUBLANES


def _store_token_tiles(ref, val, first=0, per_token=SUBLANES):
    rows = val.shape[0]
    for c in range(LANE_BLOCKS):
        ref[pl.ds(first + c, rows, stride=per_token), :] = val[:, c * LANES:(c + 1) * LANES]


def _load_token_tiles(ref, rows, first=0, per_token=SUBLANES):
    return jnp.concatenate([ref[pl.ds(first + c, rows, stride=per_token), :] for c in range(LANE_BLOCKS)], axis=1)


def _ada_kernel(c_ref, w_ref, b_ref, o_ref):
    c = c_ref[...]
    s = c * jax.nn.sigmoid(c)
    o_ref[...] = jnp.dot(s, w_ref[...], precision=lax.Precision.HIGHEST,
                         preferred_element_type=F32) + b_ref[...]


def _ada_mod(c_all, w_ada, b_ada):
    depth, d, six_d = w_ada.shape
    bt = c_all.shape[0]
    nt = six_d // d
    return pl.pallas_call(
        _ada_kernel,
        grid=(depth, nt),
        in_specs=[pl.BlockSpec((bt, d), lambda l, j: (0, 0)),
                  pl.BlockSpec((None, d, d), lambda l, j: (l, 0, j)),
                  pl.BlockSpec((None, 1, d), lambda l, j: (l, 0, j))],
        out_specs=pl.BlockSpec((None, bt, d), lambda l, j: (l, 0, j)),
        out_shape=jax.ShapeDtypeStruct((depth, bt, six_d), F32),
        compiler_params=_cparams(("arbitrary", "arbitrary")),
        name="ada_mod",
    )(c_all, w_ada, b_ada.reshape(depth, 1, six_d))


def _swap16(x):
    lane = lax.broadcasted_iota(jnp.int32, x.shape, 1)
    return jnp.where((lane % 32) < 16, pltpu.roll(x, LANES - 16, 1), pltpu.roll(x, 16, 1))


def _rope128(x, cos, sin_signed):
    return x * cos + _swap16(x) * sin_signed


def _head_rms(x, gain):
    lane = lax.broadcasted_iota(jnp.int32, x.shape, 1)
    lo = lane < HEAD_DIM
    sq = x * x
    s_lo = jnp.sum(jnp.where(lo, sq, 0.0), axis=-1, keepdims=True)
    s_hi = jnp.sum(jnp.where(lo, 0.0, sq), axis=-1, keepdims=True)
    ms = jnp.where(lo, s_lo, s_hi) * (1.0 / HEAD_DIM)
    return x * lax.rsqrt(ms + EPS) * gain


def _gated_pair(y_ref, gt):
    rows = gt.shape[0]
    y0 = _load_token_tiles(y_ref, rows, 0, 2 * SUBLANES)
    y1 = _load_token_tiles(y_ref, rows, SUBLANES, 2 * SUBLANES)
    return gt[:, 0:1] * y0 + gt[:, 1:2] * y1


def _rms_mod(x, gain, scale, shift):
    ms = jnp.mean(x * x, axis=-1, keepdims=True)
    return (x * lax.rsqrt(ms + EPS) * gain) * (1.0 + scale) + shift


def _inproj_kernel(with_moe, *refs):
    if with_moe:
        (x_ref, y_ref, gt_ref, modp_ref, mod_ref, g_ref, w_ref, qn_ref, kn_ref, ca_ref, sa_ref, cb_ref, sb_ref,
         xo_ref, qa_ref, ka_ref, va_ref, qb_ref, kb_ref, vb_ref, cp_ref) = refs
    else:
        (x_ref, mod_ref, g_ref, w_ref, qn_ref, kn_ref, ca_ref, sa_ref, cb_ref, sb_ref,
         qa_ref, ka_ref, va_ref, qb_ref, kb_ref, vb_ref, cp_ref) = refs
    d = D_MODEL
    x = x_ref[...]
    if with_moe:
        gt = gt_ref[...]
        x = x + modp_ref[5:6, :] * _gated_pair(y_ref, gt)
        xo_ref[...] = x
    h = _rms_mod(x, g_ref[...], mod_ref[1:2, :], mod_ref[0:1, :])
    proj = jnp.dot(h.astype(BF16), w_ref[...], preferred_element_type=F32)

    ca, sa, cb, sb = ca_ref[...], sa_ref[...], cb_ref[...], sb_ref[...]
    qa_scale = (A_QK_DIM ** -0.5) * LOG2E
    qb_scale = (HEAD_DIM ** -0.5) * LOG2E
    for i in range(A_WIDTH // LANES):
        sl = slice(i * LANES, (i + 1) * LANES)
        qa_ref[:, sl] = (_rope128(proj[:, sl], ca, sa) * qa_scale).astype(BF16)
        ka_ref[:, sl] = _rope128(proj[:, A_WIDTH + i * LANES:A_WIDTH + (i + 1) * LANES], ca, sa).astype(BF16)
    va_ref[...] = proj[:, 2 * A_WIDTH:3 * A_WIDTH].astype(BF16)
    o = 3 * A_WIDTH
    for j in range(B_WIDTH // LANES):
        q = _head_rms(proj[:, o + j * LANES:o + (j + 1) * LANES], qn_ref[...])
        qb_ref[:, j * LANES:(j + 1) * LANES] = (_rope128(q, cb, sb) * qb_scale).astype(BF16)
    o += B_WIDTH
    k = _head_rms(proj[:, o:o + B_KV_WIDTH], kn_ref[...])
    kb_ref[...] = _rope128(k, cb, sb).astype(BF16)
    o += B_KV_WIDTH
    vb_ref[...] = proj[:, o:o + B_KV_WIDTH].astype(BF16)
    o += B_KV_WIDTH
    cp_ref[:, :C_WIDTH] = proj[:, o:o + C_WIDTH]
    cp_ref[:, C_WIDTH:] = proj[:, o + C_WIDTH:o + 2 * C_WIDTH] * proj[:, o + 2 * C_WIDTH:o + 3 * C_WIDTH]


def _inproj(x, moe, mod_l, g1, w, qn, kn, tabs, seq):
    n, d = x.shape
    tm = TOKEN_TILE
    tiles_per_seq = seq // tm
    row = lambda i: (i, 0)
    const = lambda i: (0, 0)
    pos = lambda i: (i % tiles_per_seq, 0)
    in_specs = [pl.BlockSpec((tm, d), row)]
    args = [x]
    if moe is not None:
        y2, gates, mod_prev = moe
        in_specs += [pl.BlockSpec((2 * tm * SUBLANES, LANES), row), pl.BlockSpec((tm, 2), row),
                     pl.BlockSpec((None, 6, d), lambda i: (i // tiles_per_seq, 0, 0))]
        args += [y2, gates, mod_prev]
    in_specs += [pl.BlockSpec((None, 6, d), lambda i: (i // tiles_per_seq, 0, 0)),
                 pl.BlockSpec((1, d), const),
                 pl.BlockSpec((d, IN_COLS), const),
                 pl.BlockSpec((1, LANES), const), pl.BlockSpec((1, LANES), const)]
    in_specs += [pl.BlockSpec((tm, LANES), pos)] * 4
    args += [mod_l, g1, w, qn, kn, *tabs]
    widths = [(A_WIDTH, BF16), (A_WIDTH, BF16), (A_WIDTH, BF16), (B_WIDTH, BF16), (B_KV_WIDTH, BF16),
              (B_KV_WIDTH, BF16), (2 * C_WIDTH, F32)]
    out_specs = [pl.BlockSpec((tm, wd), row) for wd, _ in widths]
    out_shape = [jax.ShapeDtypeStruct((n, wd), dt) for wd, dt in widths]
    if moe is not None:
        out_specs = [pl.BlockSpec((tm, d), row)] + out_specs
        out_shape = [jax.ShapeDtypeStruct((n, d), F32)] + out_shape
    outs = pl.pallas_call(
        functools.partial(_inproj_kernel, moe is not None),
        grid=(n // tm,),
        in_specs=in_specs, out_specs=out_specs, out_shape=out_shape,
        compiler_params=_cparams(("arbitrary",)),
        name="inproj",
    )(*args)
    if moe is None:
        outs = [x] + list(outs)
    return outs


def _flash(qs, k_ref, v_ref, col, seq):
    rows = qs.shape[0]
    tk = KV_TILE
    sl = slice(col * LANES, (col + 1) * LANES)

    def body(c, carry):
        m, l, acc = carry
        r0 = pl.multiple_of(c * tk, tk)
        k = k_ref[pl.ds(r0, tk), sl]
        v = v_ref[pl.ds(r0, tk), sl]
        s = lax.dot_general(qs, k, (((1,), (1,)), ((), ())), preferred_element_type=F32)
        m_new = jnp.maximum(m, jnp.max(s, axis=-1, keepdims=True))
        alpha = jnp.exp2(m - m_new)
        p = jnp.exp2(s - m_new)
        l = alpha * l + jnp.sum(p, axis=-1, keepdims=True)
        acc = alpha * acc + jnp.dot(p.astype(BF16), v, preferred_element_type=F32)
        return m_new, l, acc

    init = (jnp.full((rows, 1), -jnp.inf, F32), jnp.zeros((rows, 1), F32), jnp.zeros((rows, LANES), F32))
    _, l, acc = lax.fori_loop(0, seq // tk, body, init)
    return acc, l


def _diff_attn_kernel(lambda_init, seq, q_ref, k_ref, v_ref, lam_ref, g_ref, o_ref):
    tq = q_ref.shape[0]
    lv = lam_ref[...]
    lam = (jnp.exp(jnp.sum(lv[0:1] * lv[1:2], keepdims=True))
           - jnp.exp(jnp.sum(lv[2:3] * lv[3:4], keepdims=True)) + lambda_init)
    lane = lax.broadcasted_iota(jnp.int32, (tq, LANES), 1)
    lo = lane < HEAD_DIM
    for i in range(A_WIDTH // LANES):
        q = q_ref[:, i * LANES:(i + 1) * LANES]
        zero = jnp.zeros_like(q)
        qs = jnp.concatenate([jnp.where(lane // A_QK_DIM == j, q, zero) for j in range(4)], axis=0)
        acc, l = _flash(qs, k_ref, v_ref, i, seq)
        o = acc / l
        o_even = o[0:tq] - lam * o[tq:2 * tq]
        o_odd = o[2 * tq:3 * tq] - lam * o[3 * tq:4 * tq]
        oo = jnp.where(lo, o_even, o_odd)
        o_ref[:, i * LANES:(i + 1) * LANES] = (_head_rms(oo, g_ref[...]) * (1.0 - lambda_init)).astype(BF16)


def _diff_attn(qa, ka, va, lam_vecs, subln_tiled, lambda_init, batch, seq):
    tq = Q_TILE
    q3, k3, v3 = (a.reshape(batch, seq, A_WIDTH) for a in (qa, ka, va))
    out = pl.pallas_call(
        functools.partial(_diff_attn_kernel, lambda_init, seq),
        grid=(batch, seq // tq),
        in_specs=[pl.BlockSpec((None, tq, A_WIDTH), lambda b, i: (b, i, 0)),
                  pl.BlockSpec((None, seq, A_WIDTH), lambda b, i: (b, 0, 0)),
                  pl.BlockSpec((None, seq, A_WIDTH), lambda b, i: (b, 0, 0)),
                  pl.BlockSpec((4, A_QK_DIM), lambda b, i: (0, 0)),
                  pl.BlockSpec((1, LANES), lambda b, i: (0, 0))],
        out_specs=pl.BlockSpec((None, tq, A_WIDTH), lambda b, i: (b, i, 0)),
        out_shape=jax.ShapeDtypeStruct((batch, seq, A_WIDTH), BF16),
        compiler_params=_cparams(("arbitrary", "arbitrary")),
        name="diff_attn",
    )(q3, k3, v3, lam_vecs, subln_tiled)
    return out.reshape(batch * seq, A_WIDTH)


def _gqa_kernel(seq, q_ref, k_ref, v_ref, o_ref):
    tq = q_ref.shape[0]
    lane = lax.broadcasted_iota(jnp.int32, (tq, LANES), 1)
    lo = lane < HEAD_DIM
    for j in range(B_WIDTH // LANES):
        q = q_ref[:, j * LANES:(j + 1) * LANES]
        zero = jnp.zeros_like(q)
        qs = jnp.concatenate([jnp.where(lo, q, zero), jnp.where(lo, zero, q)], axis=0)
        acc, l = _flash(qs, k_ref, v_ref, 0, seq)
        o = acc / l
        o_ref[:, j * LANES:(j + 1) * LANES] = jnp.where(lo, o[0:tq], o[tq:2 * tq]).astype(BF16)


def _gqa_attn(qb, kb, vb, batch, seq):
    tq = Q_TILE
    q3 = qb.reshape(batch, seq, B_WIDTH)
    k3 = kb.reshape(batch, seq, B_KV_WIDTH)
    v3 = vb.reshape(batch, seq, B_KV_WIDTH)
    out = pl.pallas_call(
        functools.partial(_gqa_kernel, seq),
        grid=(batch, seq // tq),
        in_specs=[pl.BlockSpec((None, tq, B_WIDTH), lambda b, i: (b, i, 0)),
                  pl.BlockSpec((None, seq, B_KV_WIDTH), lambda b, i: (b, 0, 0)),
                  pl.BlockSpec((None, seq, B_KV_WIDTH), lambda b, i: (b, 0, 0))],
        out_specs=pl.BlockSpec((None, tq, B_WIDTH), lambda b, i: (b, i, 0)),
        out_shape=jax.ShapeDtypeStruct((batch, seq, B_WIDTH), BF16),
        compiler_params=_cparams(("arbitrary", "arbitrary")),
        name="gqa_attn",
    )(q3, k3, v3)
    return out.reshape(batch * seq, B_WIDTH)


def _conv_kernel(cp_ref, prev_ref, next_ref, w_ref, o_ref):
    i = pl.program_id(1)
    last = pl.num_programs(1) - 1
    tc = cp_ref.shape[0]
    gate = cp_ref[:, :C_WIDTH]
    u = cp_ref[:, C_WIDTH:]
    row = lax.broadcasted_iota(jnp.int32, u.shape, 0)
    before = jnp.where(i == 0, 0.0, prev_ref[7:8, C_WIDTH:])
    after = jnp.where(i == last, 0.0, next_ref[0:1, C_WIDTH:])
    u_prev = jnp.where(row == 0, before, pltpu.roll(u, 1, 0))
    u_next = jnp.where(row == tc - 1, after, pltpu.roll(u, tc - 1, 0))
    w = w_ref[...]
    o_ref[...] = (gate * (w[0:1] * u_prev + w[1:2] * u + w[2:3] * u_next)).astype(BF16)


def _short_conv(cp, conv_w, batch, seq):
    tc = CONV_TILE
    sub = 8
    cp3 = cp.reshape(batch, seq, 2 * C_WIDTH)
    nsub = seq // sub
    out = pl.pallas_call(
        _conv_kernel,
        grid=(batch, seq // tc),
        in_specs=[pl.BlockSpec((None, tc, 2 * C_WIDTH), lambda b, i: (b, i, 0)),
                  pl.BlockSpec((None, sub, 2 * C_WIDTH),
                               lambda b, i: (b, jnp.maximum(i * (tc // sub) - 1, 0), 0)),
                  pl.BlockSpec((None, sub, 2 * C_WIDTH),
                               lambda b, i: (b, jnp.minimum((i + 1) * (tc // sub), nsub - 1), 0)),
                  pl.BlockSpec((3, C_WIDTH), lambda b, i: (0, 0))],
        out_specs=pl.BlockSpec((None, tc, C_WIDTH), lambda b, i: (b, i, 0)),
        out_shape=jax.ShapeDtypeStruct((batch, seq, C_WIDTH), BF16),
        compiler_params=_cparams(("arbitrary", "arbitrary")),
        name="short_conv",
    )(cp3, cp3, cp3, conv_w)
    return out.reshape(batch * seq, C_WIDTH)


def _outproj_kernel(oa_ref, ob_ref, oc_ref, x_ref, mod_ref, g_ref, wa_ref, wb_ref, wc_ref,
                    rhi_ref, rlo_ref, rb_ref, tri_ref,
                    x1_ref, h_ref, eid_ref, gate_ref, rank_ref, cnt_ref, run_ref):
    i = pl.program_id(0)
    tm = x_ref.shape[0]

    @pl.when(i == 0)
    def _():
        run_ref[...] = jnp.zeros_like(run_ref)

    mix = (jnp.dot(oa_ref[...], wa_ref[...], preferred_element_type=F32)
           + jnp.dot(ob_ref[...], wb_ref[...], preferred_element_type=F32)
           + jnp.dot(oc_ref[...], wc_ref[...], preferred_element_type=F32))
    x1 = x_ref[...] + mod_ref[2:3, :] * mix
    x1_ref[...] = x1
    h = _rms_mod(x1, g_ref[...], mod_ref[4:5, :], mod_ref[3:4, :])
    _store_token_tiles(h_ref, h)

    hi = h.astype(BF16)
    lo = (h - hi.astype(F32)).astype(BF16)
    logits = (jnp.dot(hi, rhi_ref[...], preferred_element_type=F32)
              + jnp.dot(hi, rlo_ref[...], preferred_element_type=F32)
              + jnp.dot(lo, rhi_ref[...], preferred_element_type=F32)) + rb_ref[...]
    lt = logits.T
    r8 = lax.broadcasted_iota(jnp.int32, (8, tm), 0)
    neg = -jnp.inf
    gl = jnp.where(r8 < N_GROUPS, lt[0:8], neg)
    gmax = jnp.max(gl, axis=0, keepdims=True)
    gidx = jnp.min(jnp.where(gl == gmax, r8, 8), axis=0, keepdims=True)
    p_sel = 1.0 / jnp.sum(jnp.exp(gl - gmax), axis=0, keepdims=True)
    sel = jnp.zeros((8, tm), F32)
    for g in range(N_GROUPS):
        sel = sel + jnp.where(gidx == g, lt[8 + 8 * g:16 + 8 * g], 0.0)
    v1 = jnp.max(sel, axis=0, keepdims=True)
    i1 = jnp.min(jnp.where(sel == v1, r8, 8), axis=0, keepdims=True)
    sel2 = jnp.where(r8 == i1, neg, sel)
    v2 = jnp.max(sel2, axis=0, keepdims=True)
    i2 = jnp.min(jnp.where(sel2 == v2, r8, 8), axis=0, keepdims=True)
    e = jnp.exp(v2 - v1)
    gate_ref[0:1, :] = p_sel / (1.0 + e)
    gate_ref[1:2, :] = p_sel * e / (1.0 + e)
    eid0 = gidx * EXPERTS_PER_GROUP + i1
    eid1 = gidx * EXPERTS_PER_GROUP + i2
    eid_ref[0:1, :] = eid0
    eid_ref[1:2, :] = eid1

    r32 = lax.broadcasted_iota(jnp.int32, (N_EXPERTS, tm), 0)
    oh0 = r32 == eid0
    oh1 = r32 == eid1
    cnt = jnp.where(oh0, 1.0, 0.0) + jnp.where(oh1, 1.0, 0.0)
    before = jnp.dot(cnt.astype(BF16), tri_ref[...], preferred_element_type=F32) + run_ref[:, 0:1]
    rank_ref[0:1, :] = jnp.sum(jnp.where(oh0, before, 0.0), axis=0, keepdims=True).astype(jnp.int32)
    rank_ref[1:2, :] = jnp.sum(jnp.where(oh1, before, 0.0), axis=0, keepdims=True).astype(jnp.int32)
    run = run_ref[...] + jnp.sum(cnt, axis=1, keepdims=True)
    run_ref[...] = run
    cnt_ref[...] = run


def _outproj(oa, ob, oc, x, mod_l, g2, wa, wb, wc, rhi, rlo, rb, tri, seq):
    n, d = x.shape
    tm = TOKEN_TILE
    tiles_per_seq = seq // tm
    row = lambda i: (i, 0)
    col = lambda i: (0, i)
    const = lambda i: (0, 0)
    return pl.pallas_call(
        _outproj_kernel,
        grid=(n // tm,),
        in_specs=[pl.BlockSpec((tm, A_WIDTH), row), pl.BlockSpec((tm, B_WIDTH), row),
                  pl.BlockSpec((tm, C_WIDTH), row), pl.BlockSpec((tm, d), row),
                  pl.BlockSpec((None, 6, d), lambda i: (i // tiles_per_seq, 0, 0)),
                  pl.BlockSpec((1, d), const),
                  pl.BlockSpec((A_WIDTH, d), const), pl.BlockSpec((B_WIDTH, d), const),
                  pl.BlockSpec((C_WIDTH, d), const),
                  pl.BlockSpec((d, LANES), const), pl.BlockSpec((d, LANES), const),
                  pl.BlockSpec((1, LANES), const), pl.BlockSpec((tm, tm), const)],
        out_specs=[pl.BlockSpec((tm, d), row), pl.BlockSpec((tm * SUBLANES, LANES), row),
                   pl.BlockSpec((2, tm), col), pl.BlockSpec((2, tm), col), pl.BlockSpec((2, tm), col),
                   pl.BlockSpec((N_EXPERTS, LANES), const)],
        out_shape=[jax.ShapeDtypeStruct((n, d), F32), jax.ShapeDtypeStruct((n * SUBLANES, LANES), F32),
                   jax.ShapeDtypeStruct((2, n), jnp.int32), jax.ShapeDtypeStruct((2, n), F32),
                   jax.ShapeDtypeStruct((2, n), jnp.int32),
                   jax.ShapeDtypeStruct((N_EXPERTS, LANES), F32)],
        scratch_shapes=[pltpu.VMEM((N_EXPERTS, LANES), F32)],
        compiler_params=_cparams(("arbitrary",)),
        name="outproj_router",
    )(oa, ob, oc, x, mod_l, g2, wa, wb, wc, rhi, rlo, rb, tri)


def _dispatch_kernel(ps_ref, eid_ref, rank_ref, h_ref, buf_in_ref, buf_ref, sem):
    del buf_in_ref
    td = eid_ref.shape[1]

    def body(t, carry):
        src = h_ref.at[pl.ds(pl.multiple_of(t * SUBLANES, SUBLANES), SUBLANES)]
        for k in range(2):
            dest = ps_ref[eid_ref[k, t]] + rank_ref[k, t]
            dst = buf_ref.at[pl.ds(pl.multiple_of(dest * SUBLANES, SUBLANES), SUBLANES)]
            pltpu.make_async_copy(src, dst, sem).start()
        return carry

    lax.fori_loop(0, td, body, 0)
    for _ in range(2):
        pltpu.make_async_copy(h_ref, buf_ref.at[pl.ds(0, td * SUBLANES)], sem).wait()


def _dispatch(pstarts, eid, rank, h8, cap):
    n = eid.shape[1]
    td = DMA_TILE
    grid_spec = pltpu.PrefetchScalarGridSpec(
        num_scalar_prefetch=1,
        grid=(n // td,),
        in_specs=[pl.BlockSpec((2, td), lambda i, ps: (0, i), memory_space=pltpu.SMEM),
                  pl.BlockSpec((2, td), lambda i, ps: (0, i), memory_space=pltpu.SMEM),
                  pl.BlockSpec((td * SUBLANES, LANES), lambda i, ps: (i, 0)),
                  pl.BlockSpec(memory_space=pl.ANY)],
        out_specs=pl.BlockSpec(memory_space=pl.ANY),
        scratch_shapes=[pltpu.SemaphoreType.DMA(())],
    )
    return pl.pallas_call(
        _dispatch_kernel,
        grid_spec=grid_spec,
        out_shape=jax.ShapeDtypeStruct((cap * SUBLANES, LANES), F32),
        input_output_aliases={4: 0},
        compiler_params=pltpu.CompilerParams(dimension_semantics=("arbitrary",), has_side_effects=True),
        name="moe_dispatch",
    )(pstarts, eid, rank, h8, jnp.zeros((cap * SUBLANES, LANES), F32))


def _combine_kernel(ps_ref, eid_ref, rank_ref, y_ref, o_ref, sem):
    td = eid_ref.shape[1]

    def body(t, carry):
        for k in range(2):
            dest = ps_ref[eid_ref[k, t]] + rank_ref[k, t]
            src = y_ref.at[pl.ds(pl.multiple_of(dest * SUBLANES, SUBLANES), SUBLANES)]
            dst = o_ref.at[pl.ds(pl.multiple_of((2 * t + k) * SUBLANES, SUBLANES), SUBLANES)]
            pltpu.make_async_copy(src, dst, sem).start()
        return carry

    lax.fori_loop(0, td, body, 0)
    pltpu.make_async_copy(y_ref.at[pl.ds(0, 2 * td * SUBLANES)], o_ref, sem).wait()


def _combine(pstarts, eid, rank, y8):
    n = eid.shape[1]
    td = DMA_TILE
    grid_spec = pltpu.PrefetchScalarGridSpec(
        num_scalar_prefetch=1,
        grid=(n // td,),
        in_specs=[pl.BlockSpec((2, td), lambda i, ps: (0, i), memory_space=pltpu.SMEM),
                  pl.BlockSpec((2, td), lambda i, ps: (0, i), memory_space=pltpu.SMEM),
                  pl.BlockSpec(memory_space=pl.ANY)],
        out_specs=pl.BlockSpec((2 * td * SUBLANES, LANES), lambda i, ps: (i, 0)),
        scratch_shapes=[pltpu.SemaphoreType.DMA(())],
    )
    return pl.pallas_call(
        _combine_kernel,
        grid_spec=grid_spec,
        out_shape=jax.ShapeDtypeStruct((2 * n * SUBLANES, LANES), F32),
        compiler_params=_cparams(("arbitrary",)),
        name="moe_combine",
    )(pstarts, eid, rank, y8)


def _expert_kernel(be_ref, nu_ref, x_ref, w1_ref, w3_ref, w2_ref, y_ref, w1b, w3b, w2b):
    i = pl.program_id(0)
    rb = x_ref.shape[0] // SUBLANES
    changed = jnp.logical_or(i == 0, be_ref[i] != be_ref[jnp.maximum(i - 1, 0)])

    @pl.when(changed)
    def _():
        w1b[...] = w1_ref[...].astype(BF16)
        w3b[...] = w3_ref[...].astype(BF16)
        w2b[...] = w2_ref[...].astype(BF16)

    @pl.when(i < nu_ref[0])
    def _():
        x = _load_token_tiles(x_ref, rb).astype(BF16)
        a = jnp.dot(x, w1b[...], preferred_element_type=F32)
        b = jnp.dot(x, w3b[...], preferred_element_type=F32)
        act = (a * jax.nn.sigmoid(a) * b).astype(BF16)
        _store_token_tiles(y_ref, jnp.dot(act, w2b[...], preferred_element_type=F32))

    @pl.when(i >= nu_ref[0])
    def _():
        y_ref[...] = jnp.zeros_like(y_ref)


def _experts(block_e, n_used, buf8, w1, w3, w2, layer):
    cap = buf8.shape[0] // SUBLANES
    d = D_MODEL
    rb = EXPERT_ROWS
    de = w1.shape[-1]
    grid_spec = pltpu.PrefetchScalarGridSpec(
        num_scalar_prefetch=2,
        grid=(cap // rb,),
        in_specs=[pl.BlockSpec((rb * SUBLANES, LANES), lambda i, be, nu: (i, 0)),
                  pl.BlockSpec((None, None, d, de), lambda i, be, nu: (layer, be[i], 0, 0)),
                  pl.BlockSpec((None, None, d, de), lambda i, be, nu: (layer, be[i], 0, 0)),
                  pl.BlockSpec((None, None, de, d), lambda i, be, nu: (layer, be[i], 0, 0))],
        out_specs=pl.BlockSpec((rb * SUBLANES, LANES), lambda i, be, nu: (i, 0)),
        scratch_shapes=[pltpu.VMEM((d, de), BF16), pltpu.VMEM((d, de), BF16), pltpu.VMEM((de, d), BF16)],
    )
    return pl.pallas_call(
        _expert_kernel,
        grid_spec=grid_spec,
        out_shape=jax.ShapeDtypeStruct((cap * SUBLANES, LANES), F32),
        compiler_params=_cparams(("arbitrary",)),
        name="moe_experts",
    )(block_e, n_used, buf8, w1, w3, w2)


def _final_kernel(x_ref, y_ref, gt_ref, mod_ref, g_ref, o_ref):
    d = D_MODEL
    gt = gt_ref[...]
    x = x_ref[...] + mod_ref[5:6, :] * _gated_pair(y_ref, gt)
    ms = jnp.mean(x * x, axis=-1, keepdims=True)
    o_ref[...] = x * lax.rsqrt(ms + EPS) * g_ref[...]


def _final(x1, y2, gates, mod_l, g, seq):
    n, d = x1.shape
    tm = TOKEN_TILE
    tiles_per_seq = seq // tm
    row = lambda i: (i, 0)
    return pl.pallas_call(
        _final_kernel,
        grid=(n // tm,),
        in_specs=[pl.BlockSpec((tm, d), row), pl.BlockSpec((2 * tm * SUBLANES, LANES), row),
                  pl.BlockSpec((tm, 2), row),
                  pl.BlockSpec((None, 6, d), lambda i: (i // tiles_per_seq, 0, 0)),
                  pl.BlockSpec((1, d), lambda i: (0, 0))],
        out_specs=pl.BlockSpec((tm, d), row),
        out_shape=jax.ShapeDtypeStruct((n, d), F32),
        compiler_params=_cparams(("arbitrary",)),
        name="final_norm",
    )(x1, y2, gates, mod_l, g)


def _rope_tables(seq):
    freqs = ROPE_THETA ** (-np.arange(0, A_QK_DIM, 2, dtype=np.float32) / A_QK_DIM)
    lane = np.arange(LANES)
    f = freqs[lane % 16][None, :]
    sign = np.where((lane % 32) < 16, -1.0, 1.0)[None, :].astype(np.float32)
    pos = np.arange(seq, dtype=np.float32)[:, None]
    ang_a = pos * f
    axial = np.where((lane % HEAD_DIM) < 32, np.floor(pos / GRID_W), np.mod(pos, GRID_W)).astype(np.float32)
    ang_b = axial * f
    tabs = (np.cos(ang_a), np.sin(ang_a) * sign, np.cos(ang_b), np.sin(ang_b) * sign)
    return tuple(jnp.asarray(t, dtype=F32) for t in tabs)


def _qb_perm():
    cols = []
    for j in range(B_HEADS // B_KV_HEADS):
        cols += list(range(j * HEAD_DIM, (j + 1) * HEAD_DIM))
        cols += list(range((4 + j) * HEAD_DIM, (5 + j) * HEAD_DIM))
    return np.asarray(cols, dtype=np.int32)


def _trunk(x, mod, batch, seq, p):
    n = batch * seq
    d = D_MODEL
    depth = p["w_in"].shape[0]
    tabs = _rope_tables(seq)
    perm = _qb_perm()
    tri = jnp.asarray(np.triu(np.ones((TOKEN_TILE, TOKEN_TILE), np.float32), 1), dtype=BF16)
    rb = EXPERT_ROWS
    cap = (2 * n + N_EXPERTS * (rb - 1) + rb - 1) // rb * rb
    nb = cap // rb
    x = x.reshape(n, d)
    moe = None
    for l in range(depth):
        lambda_init = 0.8 - 0.6 * math.exp(-0.3 * l)
        mod_l = mod[l].reshape(batch, 6, d)
        w_in = p["w_in"][l]
        qcols = 3 * A_WIDTH + perm
        w_in = jnp.concatenate([w_in[:, :3 * A_WIDTH], w_in[:, qcols], w_in[:, 3 * A_WIDTH + B_WIDTH:]],
                               axis=1).astype(BF16)
        tile2 = lambda v: jnp.tile(v.reshape(1, HEAD_DIM), (1, 2))
        x, qa, ka, va, qb, kb, vb, cp = _inproj(x, moe, mod_l, p["norm1"][l].reshape(1, d), w_in,
                                                tile2(p["q_norm"][l]), tile2(p["k_norm"][l]), tabs, seq)
        oa = _diff_attn(qa, ka, va, p["diff_lambda"][l], tile2(p["diff_subln"][l]), lambda_init, batch, seq)
        ob = _gqa_attn(qb, kb, vb, batch, seq)
        oc = _short_conv(cp, p["conv_w"][l], batch, seq)
        w_out = p["w_out"][l]
        wa = w_out[:A_WIDTH].astype(BF16)
        wb = w_out[A_WIDTH + perm].astype(BF16)
        wc = w_out[A_WIDTH + B_WIDTH:].astype(BF16)
        wr = jnp.zeros((d, LANES), F32)
        wr = wr.at[:, :N_GROUPS].set(p["router_group_w"][l]).at[:, 8:8 + N_EXPERTS].set(p["router_expert_w"][l])
        rhi = wr.astype(BF16)
        rlo = (wr - rhi.astype(F32)).astype(BF16)
        rbias = jnp.zeros((1, LANES), F32)
        rbias = rbias.at[0, :N_GROUPS].set(p["router_group_b"][l]).at[0, 8:8 + N_EXPERTS].set(p["router_expert_b"][l])
        x1, h8, eid, gate, rank, cnt = _outproj(oa, ob, oc, x, mod_l, p["norm2"][l].reshape(1, d),
                                                wa, wb, wc, rhi, rlo, rbias, tri, seq)
        counts = cnt[:, 0].astype(jnp.int32)
        pcounts = (counts + rb - 1) // rb * rb
        pends = jnp.cumsum(pcounts)
        pstarts = pends - pcounts
        block_row = jnp.arange(nb, dtype=jnp.int32)[:, None] * rb
        block_e = jnp.minimum(jnp.sum((pends[None, :] <= block_row).astype(jnp.int32), axis=1), N_EXPERTS - 1)
        n_used = (pends[-1:] // rb).astype(jnp.int32)
        buf8 = _dispatch(pstarts, eid, rank, h8, cap)
        y8 = _experts(block_e, n_used, buf8, p["expert_w1"], p["expert_w3"], p["expert_w2"], l)
        y2 = _combine(pstarts, eid, rank, y8)
        x = x1
        moe = (y2, gate.T, mod_l)
    out = _final(x, moe[0], moe[1], moe[2], p["final_norm"].reshape(1, d), seq)
    return out.reshape(batch, seq, d)


def kernel(x_prompt, x_sample, c_prompt, c_sample, w_ada, b_ada, norm1, norm2, w_in, w_out, diff_lambda,
           diff_subln, q_norm, k_norm, conv_w, router_group_w, router_group_b, router_expert_w,
           router_expert_b, expert_w1, expert_w3, expert_w2, final_norm):
    p = dict(norm1=norm1, norm2=norm2, w_in=w_in, w_out=w_out, diff_lambda=diff_lambda, diff_subln=diff_subln,
             q_norm=q_norm, k_norm=k_norm, conv_w=conv_w, router_group_w=router_group_w,
             router_group_b=router_group_b, router_expert_w=router_expert_w, router_expert_b=router_expert_b,
             expert_w1=expert_w1, expert_w3=expert_w3, expert_w2=expert_w2, final_norm=final_norm)
    bp, sp, _ = x_prompt.shape
    bs, ss, _ = x_sample.shape
    mod = _ada_mod(jnp.concatenate([c_prompt, c_sample], axis=0), w_ada, b_ada)
    y_prompt = _trunk(x_prompt, mod[:, :bp], bp, sp, p)
    y_sample = _trunk(x_sample, mod[:, bp:], bs, ss, p)
    return (y_prompt, y_sample)
```

```python
import functools
import math

import numpy as np
import jax
import jax.numpy as jnp
from jax import lax
from jax.experimental import pallas as pl
from jax.experimental.pallas import tpu as pltpu

F32 = jnp.float32
BF16 = jnp.bfloat16

D_MODEL = 1024
HEAD_DIM = 64
A_HEADS = 4
A_QK_DIM = 32
A_WIDTH = 256
B_HEADS = 8
B_KV_HEADS = 2
B_WIDTH = 512
B_KV_WIDTH = 128
C_WIDTH = 256
IN_COLS = 2304
N_GROUPS = 4
EXPERTS_PER_GROUP = 8
N_EXPERTS = 32
D_EXPERT = 512
GRID_W = 64
ROPE_THETA = 10000.0
EPS = 1e-6
LANES = 128
LOG2E = 1.4426950408889634

VMEM_LIMIT = 56 * 1024 * 1024

TOKEN_TILE = 256
Q_TILE = 256
KV_TILE = 2048
CONV_TILE = 512
EXPERT_ROWS = 256
DMA_TILE = 512
DMA_UNROLL = 8


def _cparams(sem):
    return pltpu.CompilerParams(dimension_semantics=sem, vmem_limit_bytes=VMEM_LIMIT)


SUBLANES = 8
LANE_BLOCKS = D_MODEL // LANES
assert LANE_BLOCKS == SUBLANES


def _store_token_tiles(ref, val, first=0, per_token=SUBLANES):
    rows = val.shape[0]
    for c in range(LANE_BLOCKS):
        ref[pl.ds(first + c, rows, stride=per_token), :] = val[:, c * LANES:(c + 1) * LANES]


def _load_token_tiles(ref, rows, first=0, per_token=SUBLANES):
    return jnp.concatenate([ref[pl.ds(first + c, rows, stride=per_token), :] for c in range(LANE_BLOCKS)], axis=1)


def _ada_kernel(c_ref, w_ref, b_ref, o_ref):
    c = c_ref[...]
    s = c * jax.nn.sigmoid(c)
    o_ref[...] = jnp.dot(s, w_ref[...], precision=lax.Precision.HIGHEST,
                         preferred_element_type=F32) + b_ref[...]


def _ada_mod(c_all, w_ada, b_ada):
    depth, d, six_d = w_ada.shape
    bt = c_all.shape[0]
    nt = six_d // d
    return pl.pallas_call(
        _ada_kernel,
        grid=(depth, nt),
        in_specs=[pl.BlockSpec((bt, d), lambda l, j: (0, 0)),
                  pl.BlockSpec((None, d, d), lambda l, j: (l, 0, j)),
                  pl.BlockSpec((None, 1, d), lambda l, j: (l, 0, j))],
        out_specs=pl.BlockSpec((None, bt, d), lambda l, j: (l, 0, j)),
        out_shape=jax.ShapeDtypeStruct((depth, bt, six_d), F32),
        compiler_params=_cparams(("arbitrary", "arbitrary")),
        name="ada_mod",
    )(c_all, w_ada, b_ada.reshape(depth, 1, six_d))


def _swap16(x):
    lane = lax.broadcasted_iota(jnp.int32, x.shape, 1)
    return jnp.where((lane % 32) < 16, pltpu.roll(x, LANES - 16, 1), pltpu.roll(x, 16, 1))


def _rope128(x, cos, sin_signed):
    return x * cos + _swap16(x) * sin_signed


def _head_rms(x, gain):
    lane = lax.broadcasted_iota(jnp.int32, x.shape, 1)
    lo = lane < HEAD_DIM
    sq = x * x
    s_lo = jnp.sum(jnp.where(lo, sq, 0.0), axis=-1, keepdims=True)
    s_hi = jnp.sum(jnp.where(lo, 0.0, sq), axis=-1, keepdims=True)
    ms = jnp.where(lo, s_lo, s_hi) * (1.0 / HEAD_DIM)
    return x * lax.rsqrt(ms + EPS) * gain


def _gated_pair(y_ref, gt):
    rows = gt.shape[0]
    y0 = _load_token_tiles(y_ref, rows, 0, 2 * SUBLANES)
    y1 = _load_token_tiles(y_ref, rows, SUBLANES, 2 * SUBLANES)
    return gt[:, 0:1] * y0 + gt[:, 1:2] * y1


def _rms_mod(x, gain, scale, shift):
    ms = jnp.mean(x * x, axis=-1, keepdims=True)
    return (x * lax.rsqrt(ms + EPS) * gain) * (1.0 + scale) + shift


def _inproj_kernel(with_moe, *refs):
    if with_moe:
        (x_ref, y_ref, gt_ref, modp_ref, mod_ref, g_ref, w_ref, qn_ref, kn_ref, ca_ref, sa_ref, cb_ref, sb_ref,
         xo_ref, qa_ref, ka_ref, va_ref, qb_ref, kb_ref, vb_ref, cp_ref) = refs
    else:
        (x_ref, mod_ref, g_ref, w_ref, qn_ref, kn_ref, ca_ref, sa_ref, cb_ref, sb_ref,
         qa_ref, ka_ref, va_ref, qb_ref, kb_ref, vb_ref, cp_ref) = refs
    d = D_MODEL
    x = x_ref[...]
    if with_moe:
        gt = gt_ref[...]
        x = x + modp_ref[5:6, :] * _gated_pair(y_ref, gt)
        xo_ref[...] = x
    h = _rms_mod(x, g_ref[...], mod_ref[1:2, :], mod_ref[0:1, :])
    proj = jnp.dot(h.astype(BF16), w_ref[...], preferred_element_type=F32)

    ca, sa, cb, sb = ca_ref[...], sa_ref[...], cb_ref[...], sb_ref[...]
    qa_scale = (A_QK_DIM ** -0.5) * LOG2E
    qb_scale = (HEAD_DIM ** -0.5) * LOG2E
    for i in range(A_WIDTH // LANES):
        sl = slice(i * LANES, (i + 1) * LANES)
        qa_ref[:, sl] = (_rope128(proj[:, sl], ca, sa) * qa_scale).astype(BF16)
        ka_ref[:, sl] = _rope128(proj[:, A_WIDTH + i * LANES:A_WIDTH + (i + 1) * LANES], ca, sa).astype(BF16)
    va_ref[...] = proj[:, 2 * A_WIDTH:3 * A_WIDTH].astype(BF16)
    o = 3 * A_WIDTH
    for j in range(B_WIDTH // LANES):
        q = _head_rms(proj[:, o + j * LANES:o + (j + 1) * LANES], qn_ref[...])
        qb_ref[:, j * LANES:(j + 1) * LANES] = (_rope128(q, cb, sb) * qb_scale).astype(BF16)
    o += B_WIDTH
    k = _head_rms(proj[:, o:o + B_KV_WIDTH], kn_ref[...])
    kb_ref[...] = _rope128(k, cb, sb).astype(BF16)
    o += B_KV_WIDTH
    vb_ref[...] = proj[:, o:o + B_KV_WIDTH].astype(BF16)
    o += B_KV_WIDTH
    cp_ref[:, :C_WIDTH] = proj[:, o:o + C_WIDTH]
    cp_ref[:, C_WIDTH:] = proj[:, o + C_WIDTH:o + 2 * C_WIDTH] * proj[:, o + 2 * C_WIDTH:o + 3 * C_WIDTH]


def _inproj(x, moe, mod_l, g1, w, qn, kn, tabs, seq):
    n, d = x.shape
    tm = TOKEN_TILE
    tiles_per_seq = seq // tm
    row = lambda i: (i, 0)
    const = lambda i: (0, 0)
    pos = lambda i: (i % tiles_per_seq, 0)
    in_specs = [pl.BlockSpec((tm, d), row)]
    args = [x]
    if moe is not None:
        y2, gates, mod_prev = moe
        in_specs += [pl.BlockSpec((2 * tm * SUBLANES, LANES), row), pl.BlockSpec((tm, 2), row),
                     pl.BlockSpec((None, 6, d), lambda i: (i // tiles_per_seq, 0, 0))]
        args += [y2, gates, mod_prev]
    in_specs += [pl.BlockSpec((None, 6, d), lambda i: (i // tiles_per_seq, 0, 0)),
                 pl.BlockSpec((1, d), const),
                 pl.BlockSpec((d, IN_COLS), const),
                 pl.BlockSpec((1, LANES), const), pl.BlockSpec((1, LANES), const)]
    in_specs += [pl.BlockSpec((tm, LANES), pos)] * 4
    args += [mod_l, g1, w, qn, kn, *tabs]
    widths = [(A_WIDTH, BF16), (A_WIDTH, BF16), (A_WIDTH, BF16), (B_WIDTH, BF16), (B_KV_WIDTH, BF16),
              (B_KV_WIDTH, BF16), (2 * C_WIDTH, F32)]
    out_specs = [pl.BlockSpec((tm, wd), row) for wd, _ in widths]
    out_shape = [jax.ShapeDtypeStruct((n, wd), dt) for wd, dt in widths]
    if moe is not None:
        out_specs = [pl.BlockSpec((tm, d), row)] + out_specs
        out_shape = [jax.ShapeDtypeStruct((n, d), F32)] + out_shape
    outs = pl.pallas_call(
        functools.partial(_inproj_kernel, moe is not None),
        grid=(n // tm,),
        in_specs=in_specs, out_specs=out_specs, out_shape=out_shape,
        compiler_params=_cparams(("arbitrary",)),
        name="inproj",
    )(*args)
    if moe is None:
        outs = [x] + list(outs)
    return outs


def _flash(qs, k_ref, v_ref, col, seq):
    rows = qs.shape[0]
    tk = min(KV_TILE, seq)
    sl = slice(col * LANES, (col + 1) * LANES)

    def body(c, carry):
        m, l, acc = carry
        r0 = pl.multiple_of(c * tk, tk)
        k = k_ref[pl.ds(r0, tk), sl]
        v = v_ref[pl.ds(r0, tk), sl]
        s = lax.dot_general(qs, k, (((1,), (1,)), ((), ())), preferred_element_type=F32)
        m_new = jnp.maximum(m, jnp.max(s, axis=-1, keepdims=True))
        alpha = jnp.exp2(m - m_new)
        p = jnp.exp2(s - m_new)
        l = alpha * l + jnp.sum(p, axis=-1, keepdims=True)
        acc = alpha * acc + jnp.dot(p.astype(BF16), v, preferred_element_type=F32)
        return m_new, l, acc

    init = (jnp.full((rows, 1), -jnp.inf, F32), jnp.zeros((rows, 1), F32), jnp.zeros((rows, LANES), F32))
    _, l, acc = lax.fori_loop(0, seq // tk, body, init, unroll=True)
    return acc, l


def _diff_attn_kernel(lambda_init, seq, q_ref, k_ref, v_ref, lam_ref, g_ref, o_ref):
    lv = lam_ref[...]
    lam = (jnp.exp(jnp.sum(lv[0:1] * lv[1:2], keepdims=True))
           - jnp.exp(jnp.sum(lv[2:3] * lv[3:4], keepdims=True)) + lambda_init)
    tq = q_ref.shape[0]
    lane = lax.broadcasted_iota(jnp.int32, (tq, LANES), 1)
    lo = lane < HEAD_DIM
    for i in range(A_WIDTH // LANES):
        sl = slice(i * LANES, (i + 1) * LANES)
        q = q_ref[:, sl]
        zero = jnp.zeros_like(q)
        qs = jnp.concatenate([jnp.where(lane // A_QK_DIM == j, q, zero) for j in range(4)], axis=0)
        acc, l = _flash(qs, k_ref, v_ref, i, seq)
        o = acc / l
        o_even = o[0:tq] - lam * o[tq:2 * tq]
        o_odd = o[2 * tq:3 * tq] - lam * o[3 * tq:4 * tq]
        oo = jnp.where(lo, o_even, o_odd)
        o_ref[:, sl] = (_head_rms(oo, g_ref[...]) * (1.0 - lambda_init)).astype(BF16)


def _diff_attn(qa, ka, va, lam_vecs, subln_tiled, lambda_init, batch, seq):
    tq = Q_TILE
    q3, k3, v3 = (a.reshape(batch, seq, A_WIDTH) for a in (qa, ka, va))
    out = pl.pallas_call(
        functools.partial(_diff_attn_kernel, lambda_init, seq),
        grid=(batch, seq // tq),
        in_specs=[pl.BlockSpec((None, tq, A_WIDTH), lambda b, i: (b, i, 0)),
                  pl.BlockSpec((None, seq, A_WIDTH), lambda b, i: (b, 0, 0)),
                  pl.BlockSpec((None, seq, A_WIDTH), lambda b, i: (b, 0, 0)),
                  pl.BlockSpec((4, A_QK_DIM), lambda b, i: (0, 0)),
                  pl.BlockSpec((1, LANES), lambda b, i: (0, 0))],
        out_specs=pl.BlockSpec((None, tq, A_WIDTH), lambda b, i: (b, i, 0)),
        out_shape=jax.ShapeDtypeStruct((batch, seq, A_WIDTH), BF16),
        compiler_params=_cparams(("arbitrary", "arbitrary")),
        name="diff_attn",
    )(q3, k3, v3, lam_vecs, subln_tiled)
    return out.reshape(batch * seq, A_WIDTH)


def _gqa_kernel(seq, q_ref, k_ref, v_ref, o_ref):
    tq = q_ref.shape[0]
    lane = lax.broadcasted_iota(jnp.int32, (tq, LANES), 1)
    lo = lane < HEAD_DIM
    for j in range(B_WIDTH // LANES):
        q = q_ref[:, j * LANES:(j + 1) * LANES]
        zero = jnp.zeros_like(q)
        qs = jnp.concatenate([jnp.where(lo, q, zero), jnp.where(lo, zero, q)], axis=0)
        acc, l = _flash(qs, k_ref, v_ref, 0, seq)
        o = acc / l
        o_ref[:, j * LANES:(j + 1) * LANES] = jnp.where(lo, o[0:tq], o[tq:2 * tq]).astype(BF16)


def _gqa_attn(qb, kb, vb, batch, seq):
    tq = Q_TILE
    q3 = qb.reshape(batch, seq, B_WIDTH)
    k3 = kb.reshape(batch, seq, B_KV_WIDTH)
    v3 = vb.reshape(batch, seq, B_KV_WIDTH)
    out = pl.pallas_call(
        functools.partial(_gqa_kernel, seq),
        grid=(batch, seq // tq),
        in_specs=[pl.BlockSpec((None, tq, B_WIDTH), lambda b, i: (b, i, 0)),
                  pl.BlockSpec((None, seq, B_KV_WIDTH), lambda b, i: (b, 0, 0)),
                  pl.BlockSpec((None, seq, B_KV_WIDTH), lambda b, i: (b, 0, 0))],
        out_specs=pl.BlockSpec((None, tq, B_WIDTH), lambda b, i: (b, i, 0)),
        out_shape=jax.ShapeDtypeStruct((batch, seq, B_WIDTH), BF16),
        compiler_params=_cparams(("arbitrary", "arbitrary")),
        name="gqa_attn",
    )(q3, k3, v3)
    return out.reshape(batch * seq, B_WIDTH)


def _conv_kernel(cp_ref, prev_ref, next_ref, w_ref, o_ref):
    i = pl.program_id(1)
    last = pl.num_programs(1) - 1
    tc = cp_ref.shape[0]
    gate = cp_ref[:, :C_WIDTH]
    u = cp_ref[:, C_WIDTH:]
    row = lax.broadcasted_iota(jnp.int32, u.shape, 0)
    before = jnp.where(i == 0, 0.0, prev_ref[7:8, C_WIDTH:])
    after = jnp.where(i == last, 0.0, next_ref[0:1, C_WIDTH:])
    u_prev = jnp.where(row == 0, before, pltpu.roll(u, 1, 0))
    u_next = jnp.where(row == tc - 1, after, pltpu.roll(u, tc - 1, 0))
    w = w_ref[...]
    o_ref[...] = (gate * (w[0:1] * u_prev + w[1:2] * u + w[2:3] * u_next)).astype(BF16)


def _short_conv(cp, conv_w, batch, seq):
    tc = CONV_TILE
    sub = 8
    cp3 = cp.reshape(batch, seq, 2 * C_WIDTH)
    nsub = seq // sub
    out = pl.pallas_call(
        _conv_kernel,
        grid=(batch, seq // tc),
        in_specs=[pl.BlockSpec((None, tc, 2 * C_WIDTH), lambda b, i: (b, i, 0)),
                  pl.BlockSpec((None, sub, 2 * C_WIDTH),
                               lambda b, i: (b, jnp.maximum(i * (tc // sub) - 1, 0), 0)),
                  pl.BlockSpec((None, sub, 2 * C_WIDTH),
                               lambda b, i: (b, jnp.minimum((i + 1) * (tc // sub), nsub - 1), 0)),
                  pl.BlockSpec((3, C_WIDTH), lambda b, i: (0, 0))],
        out_specs=pl.BlockSpec((None, tc, C_WIDTH), lambda b, i: (b, i, 0)),
        out_shape=jax.ShapeDtypeStruct((batch, seq, C_WIDTH), BF16),
        compiler_params=_cparams(("arbitrary", "arbitrary")),
        name="short_conv",
    )(cp3, cp3, cp3, conv_w)
    return out.reshape(batch * seq, C_WIDTH)


def _outproj_kernel(oa_ref, ob_ref, oc_ref, x_ref, mod_ref, g_ref, wa_ref, wb_ref, wc_ref,
                    rhi_ref, rlo_ref, rb_ref, tri_ref,
                    x1_ref, h_ref, eid_ref, gate_ref, rank_ref, cnt_ref, run_ref):
    i = pl.program_id(0)
    tm = x_ref.shape[0]

    @pl.when(i == 0)
    def _():
        run_ref[...] = jnp.zeros_like(run_ref)

    mix = (jnp.dot(oa_ref[...], wa_ref[...], preferred_element_type=F32)
           + jnp.dot(ob_ref[...], wb_ref[...], preferred_element_type=F32)
           + jnp.dot(oc_ref[...], wc_ref[...], preferred_element_type=F32))
    x1 = x_ref[...] + mod_ref[2:3, :] * mix
    x1_ref[...] = x1
    h = _rms_mod(x1, g_ref[...], mod_ref[4:5, :], mod_ref[3:4, :])
    _store_token_tiles(h_ref, h)

    hi = h.astype(BF16)
    lo = (h - hi.astype(F32)).astype(BF16)
    logits = (jnp.dot(hi, rhi_ref[...], preferred_element_type=F32)
              + jnp.dot(hi, rlo_ref[...], preferred_element_type=F32)
              + jnp.dot(lo, rhi_ref[...], preferred_element_type=F32)) + rb_ref[...]
    lt = logits.T
    r8 = lax.broadcasted_iota(jnp.int32, (8, tm), 0)
    neg = -jnp.inf
    gl = jnp.where(r8 < N_GROUPS, lt[0:8], neg)
    gmax = jnp.max(gl, axis=0, keepdims=True)
    gidx = jnp.min(jnp.where(gl == gmax, r8, 8), axis=0, keepdims=True)
    p_sel = 1.0 / jnp.sum(jnp.exp(gl - gmax), axis=0, keepdims=True)
    sel = jnp.zeros((8, tm), F32)
    for g in range(N_GROUPS):
        sel = sel + jnp.where(gidx == g, lt[8 + 8 * g:16 + 8 * g], 0.0)
    v1 = jnp.max(sel, axis=0, keepdims=True)
    i1 = jnp.min(jnp.where(sel == v1, r8, 8), axis=0, keepdims=True)
    sel2 = jnp.where(r8 == i1, neg, sel)
    v2 = jnp.max(sel2, axis=0, keepdims=True)
    i2 = jnp.min(jnp.where(sel2 == v2, r8, 8), axis=0, keepdims=True)
    e = jnp.exp(v2 - v1)
    gate_ref[0:1, :] = p_sel / (1.0 + e)
    gate_ref[1:2, :] = p_sel * e / (1.0 + e)
    eid0 = gidx * EXPERTS_PER_GROUP + i1
    eid1 = gidx * EXPERTS_PER_GROUP + i2
    eid_ref[0:1, :] = eid0
    eid_ref[1:2, :] = eid1

    r32 = lax.broadcasted_iota(jnp.int32, (N_EXPERTS, tm), 0)
    oh0 = r32 == eid0
    oh1 = r32 == eid1
    cnt = jnp.where(oh0, 1.0, 0.0) + jnp.where(oh1, 1.0, 0.0)
    before = jnp.dot(cnt.astype(BF16), tri_ref[...], preferred_element_type=F32) + run_ref[:, 0:1]
    rank_ref[0:1, :] = jnp.sum(jnp.where(oh0, before, 0.0), axis=0, keepdims=True).astype(jnp.int32)
    rank_ref[1:2, :] = jnp.sum(jnp.where(oh1, before, 0.0), axis=0, keepdims=True).astype(jnp.int32)
    run = run_ref[...] + jnp.sum(cnt, axis=1, keepdims=True)
    run_ref[...] = run
    cnt_ref[...] = run


def _outproj(oa, ob, oc, x, mod_l, g2, wa, wb, wc, rhi, rlo, rb, tri, seq):
    n, d = x.shape
    tm = TOKEN_TILE
    tiles_per_seq = seq // tm
    row = lambda i: (i, 0)
    col = lambda i: (0, i)
    const = lambda i: (0, 0)
    return pl.pallas_call(
        _outproj_kernel,
        grid=(n // tm,),
        in_specs=[pl.BlockSpec((tm, A_WIDTH), row), pl.BlockSpec((tm, B_WIDTH), row),
                  pl.BlockSpec((tm, C_WIDTH), row), pl.BlockSpec((tm, d), row),
                  pl.BlockSpec((None, 6, d), lambda i: (i // tiles_per_seq, 0, 0)),
                  pl.BlockSpec((1, d), const),
                  pl.BlockSpec((A_WIDTH, d), const), pl.BlockSpec((B_WIDTH, d), const),
                  pl.BlockSpec((C_WIDTH, d), const),
                  pl.BlockSpec((d, LANES), const), pl.BlockSpec((d, LANES), const),
                  pl.BlockSpec((1, LANES), const), pl.BlockSpec((tm, tm), const)],
        out_specs=[pl.BlockSpec((tm, d), row), pl.BlockSpec((tm * SUBLANES, LANES), row),
                   pl.BlockSpec((2, tm), col), pl.BlockSpec((2, tm), col), pl.BlockSpec((2, tm), col),
                   pl.BlockSpec((N_EXPERTS, LANES), const)],
        out_shape=[jax.ShapeDtypeStruct((n, d), F32), jax.ShapeDtypeStruct((n * SUBLANES, LANES), F32),
                   jax.ShapeDtypeStruct((2, n), jnp.int32), jax.ShapeDtypeStruct((2, n), F32),
                   jax.ShapeDtypeStruct((2, n), jnp.int32),
                   jax.ShapeDtypeStruct((N_EXPERTS, LANES), F32)],
        scratch_shapes=[pltpu.VMEM((N_EXPERTS, LANES), F32)],
        compiler_params=_cparams(("arbitrary",)),
        name="outproj_router",
    )(oa, ob, oc, x, mod_l, g2, wa, wb, wc, rhi, rlo, rb, tri)


def _dispatch_kernel(ps_ref, eid_ref, rank_ref, h_ref, buf_in_ref, buf_ref, sem):
    del buf_in_ref
    td = eid_ref.shape[1]

    def body(t, carry):
        src = h_ref.at[pl.ds(pl.multiple_of(t * SUBLANES, SUBLANES), SUBLANES)]
        for k in range(2):
            dest = ps_ref[eid_ref[k, t]] + rank_ref[k, t]
            dst = buf_ref.at[pl.ds(pl.multiple_of(dest * SUBLANES, SUBLANES), SUBLANES)]
            pltpu.make_async_copy(src, dst, sem).start(priority=k)
        return carry

    lax.fori_loop(0, td, body, 0, unroll=DMA_UNROLL)
    for _ in range(2):
        pltpu.make_async_copy(h_ref, buf_ref.at[pl.ds(0, td * SUBLANES)], sem).wait()


def _dispatch(pstarts, eid, rank, h8, cap):
    n = eid.shape[1]
    td = DMA_TILE
    grid_spec = pltpu.PrefetchScalarGridSpec(
        num_scalar_prefetch=1,
        grid=(n // td,),
        in_specs=[pl.BlockSpec((2, td), lambda i, ps: (0, i), memory_space=pltpu.SMEM),
                  pl.BlockSpec((2, td), lambda i, ps: (0, i), memory_space=pltpu.SMEM),
                  pl.BlockSpec((td * SUBLANES, LANES), lambda i, ps: (i, 0)),
                  pl.BlockSpec(memory_space=pl.ANY)],
        out_specs=pl.BlockSpec(memory_space=pl.ANY),
        scratch_shapes=[pltpu.SemaphoreType.DMA(())],
    )
    return pl.pallas_call(
        _dispatch_kernel,
        grid_spec=grid_spec,
        out_shape=jax.ShapeDtypeStruct((cap * SUBLANES, LANES), F32),
        input_output_aliases={4: 0},
        compiler_params=pltpu.CompilerParams(dimension_semantics=("arbitrary",), has_side_effects=True),
        name="moe_dispatch",
    )(pstarts, eid, rank, h8, jnp.zeros((cap * SUBLANES, LANES), F32))


def _combine_kernel(ps_ref, eid_ref, rank_ref, y_ref, o_ref, sem):
    td = eid_ref.shape[1]

    def body(t, carry):
        for k in range(2):
            dest = ps_ref[eid_ref[k, t]] + rank_ref[k, t]
            src = y_ref.at[pl.ds(pl.multiple_of(dest * SUBLANES, SUBLANES), SUBLANES)]
            dst = o_ref.at[pl.ds(pl.multiple_of((2 * t + k) * SUBLANES, SUBLANES), SUBLANES)]
            pltpu.make_async_copy(src, dst, sem).start(priority=k)
        return carry

    lax.fori_loop(0, td, body, 0, unroll=DMA_UNROLL)
    pltpu.make_async_copy(y_ref.at[pl.ds(0, 2 * td * SUBLANES)], o_ref, sem).wait()


def _combine(pstarts, eid, rank, y8):
    n = eid.shape[1]
    td = DMA_TILE
    grid_spec = pltpu.PrefetchScalarGridSpec(
        num_scalar_prefetch=1,
        grid=(n // td,),
        in_specs=[pl.BlockSpec((2, td), lambda i, ps: (0, i), memory_space=pltpu.SMEM),
                  pl.BlockSpec((2, td), lambda i, ps: (0, i), memory_space=pltpu.SMEM),
                  pl.BlockSpec(memory_space=pl.ANY)],
        out_specs=pl.BlockSpec((2 * td * SUBLANES, LANES), lambda i, ps: (i, 0)),
        scratch_shapes=[pltpu.SemaphoreType.DMA(())],
    )
    return pl.pallas_call(
        _combine_kernel,
        grid_spec=grid_spec,
        out_shape=jax.ShapeDtypeStruct((2 * n * SUBLANES, LANES), F32),
        compiler_params=_cparams(("arbitrary",)),
        name="moe_combine",
    )(pstarts, eid, rank, y8)


def _expert_kernel(be_ref, nu_ref, x_ref, w1_ref, w3_ref, w2_ref, y_ref, w1b, w3b, w2b):
    i = pl.program_id(0)
    rb = x_ref.shape[0] // SUBLANES
    changed = jnp.logical_or(i == 0, be_ref[i] != be_ref[jnp.maximum(i - 1, 0)])

    @pl.when(changed)
    def _():
        w1b[...] = w1_ref[...].astype(BF16)
        w3b[...] = w3_ref[...].astype(BF16)
        w2b[...] = w2_ref[...].astype(BF16)

    @pl.when(i < nu_ref[0])
    def _():
        x = _load_token_tiles(x_ref, rb).astype(BF16)
        a = jnp.dot(x, w1b[...], preferred_element_type=F32)
        b = jnp.dot(x, w3b[...], preferred_element_type=F32)
        act = (a * jax.nn.sigmoid(a) * b).astype(BF16)
        _store_token_tiles(y_ref, jnp.dot(act, w2b[...], preferred_element_type=F32))

    @pl.when(i >= nu_ref[0])
    def _():
        y_ref[...] = jnp.zeros_like(y_ref)


def _experts(block_e, n_used, buf8, w1, w3, w2, layer):
    cap = buf8.shape[0] // SUBLANES
    d = D_MODEL
    rb = EXPERT_ROWS
    de = w1.shape[-1]
    grid_spec = pltpu.PrefetchScalarGridSpec(
        num_scalar_prefetch=2,
        grid=(cap // rb,),
        in_specs=[pl.BlockSpec((rb * SUBLANES, LANES), lambda i, be, nu: (i, 0)),
                  pl.BlockSpec((None, None, d, de), lambda i, be, nu: (layer, be[i], 0, 0)),
                  pl.BlockSpec((None, None, d, de), lambda i, be, nu: (layer, be[i], 0, 0)),
                  pl.BlockSpec((None, None, de, d), lambda i, be, nu: (layer, be[i], 0, 0))],
        out_specs=pl.BlockSpec((rb * SUBLANES, LANES), lambda i, be, nu: (i, 0)),
        scratch_shapes=[pltpu.VMEM((d, de), BF16), pltpu.VMEM((d, de), BF16), pltpu.VMEM((de, d), BF16)],
    )
    return pl.pallas_call(
        _expert_kernel,
        grid_spec=grid_spec,
        out_shape=jax.ShapeDtypeStruct((cap * SUBLANES, LANES), F32),
        compiler_params=_cparams(("arbitrary",)),
        name="moe_experts",
    )(block_e, n_used, buf8, w1, w3, w2)


def _final_kernel(x_ref, y_ref, gt_ref, mod_ref, g_ref, o_ref):
    d = D_MODEL
    gt = gt_ref[...]
    x = x_ref[...] + mod_ref[5:6, :] * _gated_pair(y_ref, gt)
    ms = jnp.mean(x * x, axis=-1, keepdims=True)
    o_ref[...] = x * lax.rsqrt(ms + EPS) * g_ref[...]


def _final(x1, y2, gates, mod_l, g, seq):
    n, d = x1.shape
    tm = TOKEN_TILE
    tiles_per_seq = seq // tm
    row = lambda i: (i, 0)
    return pl.pallas_call(
        _final_kernel,
        grid=(n // tm,),
        in_specs=[pl.BlockSpec((tm, d), row), pl.BlockSpec((2 * tm * SUBLANES, LANES), row),
                  pl.BlockSpec((tm, 2), row),
                  pl.BlockSpec((None, 6, d), lambda i: (i // tiles_per_seq, 0, 0)),
                  pl.BlockSpec((1, d), lambda i: (0, 0))],
        out_specs=pl.BlockSpec((tm, d), row),
        out_shape=jax.ShapeDtypeStruct((n, d), F32),
        compiler_params=_cparams(("arbitrary",)),
        name="final_norm",
    )(x1, y2, gates, mod_l, g)


def _rope_tables(seq):
    freqs = ROPE_THETA ** (-np.arange(0, A_QK_DIM, 2, dtype=np.float32) / A_QK_DIM)
    lane = np.arange(LANES)
    f = freqs[lane % 16][None, :]
    sign = np.where((lane % 32) < 16, -1.0, 1.0)[None, :].astype(np.float32)
    pos = np.arange(seq, dtype=np.float32)[:, None]
    ang_a = pos * f
    axial = np.where((lane % HEAD_DIM) < 32, np.floor(pos / GRID_W), np.mod(pos, GRID_W)).astype(np.float32)
    ang_b = axial * f
    tabs = (np.cos(ang_a), np.sin(ang_a) * sign, np.cos(ang_b), np.sin(ang_b) * sign)
    return tuple(jnp.asarray(t, dtype=F32) for t in tabs)


def _qb_perm():
    cols = []
    for j in range(B_HEADS // B_KV_HEADS):
        cols += list(range(j * HEAD_DIM, (j + 1) * HEAD_DIM))
        cols += list(range((4 + j) * HEAD_DIM, (5 + j) * HEAD_DIM))
    return np.asarray(cols, dtype=np.int32)


def _trunk(x, mod, batch, seq, p):
    n = batch * seq
    d = D_MODEL
    depth = p["w_in"].shape[0]
    tabs = _rope_tables(seq)
    perm = _qb_perm()
    tri = jnp.asarray(np.triu(np.ones((TOKEN_TILE, TOKEN_TILE), np.float32), 1), dtype=BF16)
    rb = EXPERT_ROWS
    cap = (2 * n + N_EXPERTS * (rb - 1) + rb - 1) // rb * rb
    nb = cap // rb
    x = x.reshape(n, d)
    moe = None
    for l in range(depth):
        lambda_init = 0.8 - 0.6 * math.exp(-0.3 * l)
        mod_l = mod[l].reshape(batch, 6, d)
        w_in = p["w_in"][l]
        qcols = 3 * A_WIDTH + perm
        w_in = jnp.concatenate([w_in[:, :3 * A_WIDTH], w_in[:, qcols], w_in[:, 3 * A_WIDTH + B_WIDTH:]],
                               axis=1).astype(BF16)
        tile2 = lambda v: jnp.tile(v.reshape(1, HEAD_DIM), (1, 2))
        x, qa, ka, va, qb, kb, vb, cp = _inproj(x, moe, mod_l, p["norm1"][l].reshape(1, d), w_in,
                                                tile2(p["q_norm"][l]), tile2(p["k_norm"][l]), tabs, seq)
        oa = _diff_attn(qa, ka, va, p["diff_lambda"][l], tile2(p["diff_subln"][l]), lambda_init, batch, seq)
        ob = _gqa_attn(qb, kb, vb, batch, seq)
        oc = _short_conv(cp, p["conv_w"][l], batch, seq)
        w_out = p["w_out"][l]
        wa = w_out[:A_WIDTH].astype(BF16)
        wb = w_out[A_WIDTH + perm].astype(BF16)
        wc = w_out[A_WIDTH + B_WIDTH:].astype(BF16)
        wr = jnp.zeros((d, LANES), F32)
        wr = wr.at[:, :N_GROUPS].set(p["router_group_w"][l]).at[:, 8:8 + N_EXPERTS].set(p["router_expert_w"][l])
        rhi = wr.astype(BF16)
        rlo = (wr - rhi.astype(F32)).astype(BF16)
        rbias = jnp.zeros((1, LANES), F32)
        rbias = rbias.at[0, :N_GROUPS].set(p["router_group_b"][l]).at[0, 8:8 + N_EXPERTS].set(p["router_expert_b"][l])
        x1, h8, eid, gate, rank, cnt = _outproj(oa, ob, oc, x, mod_l, p["norm2"][l].reshape(1, d),
                                                wa, wb, wc, rhi, rlo, rbias, tri, seq)
        counts = cnt[:, 0].astype(jnp.int32)
        pcounts = (counts + rb - 1) // rb * rb
        pends = jnp.cumsum(pcounts)
        pstarts = pends - pcounts
        block_row = jnp.arange(nb, dtype=jnp.int32)[:, None] * rb
        block_e = jnp.minimum(jnp.sum((pends[None, :] <= block_row).astype(jnp.int32), axis=1), N_EXPERTS - 1)
        n_used = (pends[-1:] // rb).astype(jnp.int32)
        buf8 = _dispatch(pstarts, eid, rank, h8, cap)
        y8 = _experts(block_e, n_used, buf8, p["expert_w1"], p["expert_w3"], p["expert_w2"], l)
        y2 = _combine(pstarts, eid, rank, y8)
        x = x1
        moe = (y2, gate.T, mod_l)
    out = _final(x, moe[0], moe[1], moe[2], p["final_norm"].reshape(1, d), seq)
    return out.reshape(batch, seq, d)


def kernel(x_prompt, x_sample, c_prompt, c_sample, w_ada, b_ada, norm1, norm2, w_in, w_out, diff_lambda,
           diff_subln, q_norm, k_norm, conv_w, router_group_w, router_group_b, router_expert_w,
           router_expert_b, expert_w1, expert_w3, expert_w2, final_norm):
    p = dict(norm1=norm1, norm2=norm2, w_in=w_in, w_out=w_out, diff_lambda=diff_lambda, diff_subln=diff_subln,
             q_norm=q_norm, k_norm=k_norm, conv_w=conv_w, router_group_w=router_group_w,
             router_group_b=router_group_b, router_expert_w=router_expert_w, router_expert_b=router_expert_b,
             expert_w1=expert_w1, expert_w3=expert_w3, expert_w2=expert_w2, final_norm=final_norm)
    bp, sp, _ = x_prompt.shape
    bs, ss, _ = x_sample.shape
    mod = _ada_mod(jnp.concatenate([c_prompt, c_sample], axis=0), w_ada, b_ada)
    y_prompt = _trunk(x_prompt, mod[:, :bp], bp, sp, p)
    y_sample = _trunk(x_sample, mod[:, bp:], bs, ss, p)
    return (y_prompt, y_sample)
```

```python
import functools
import math

import numpy as np
import jax
import jax.numpy as jnp
from jax import lax
from jax.experimental import pallas as pl
from jax.experimental.pallas import tpu as pltpu

F32 = jnp.float32
BF16 = jnp.bfloat16

D_MODEL = 1024
HEAD_DIM = 64
A_HEADS = 4
A_QK_DIM = 32
A_WIDTH = 256
B_HEADS = 8
B_KV_HEADS = 2
B_WIDTH = 512
B_KV_WIDTH = 128
C_WIDTH = 256
IN_COLS = 2304
N_GROUPS = 4
EXPERTS_PER_GROUP = 8
N_EXPERTS = 32
D_EXPERT = 512
GRID_W = 64
ROPE_THETA = 10000.0
EPS = 1e-6
LANES = 128
LOG2E = 1.4426950408889634

VMEM_LIMIT = 56 * 1024 * 1024

TOKEN_TILE = 256
Q_TILE = 256
KV_TILE = 2048
CONV_TILE = 512
EXPERT_ROWS = 256
DMA_TILE = 1024
DMA_UNROLL = 8


def _cparams(sem):
    return pltpu.CompilerParams(dimension_semantics=sem, vmem_limit_bytes=VMEM_LIMIT)


SUBLANES = 8
LANE_BLOCKS = D_MODEL // LANES
assert LANE_BLOCKS == SUBLANES


def _store_token_tiles(ref, val, first=0, per_token=SUBLANES):
    rows = val.shape[0]
    for c in range(LANE_BLOCKS):
        ref[pl.ds(first + c, rows, stride=per_token), :] = val[:, c * LANES:(c + 1) * LANES]


def _load_token_tiles(ref, rows, first=0, per_token=SUBLANES):
    return jnp.concatenate([ref[pl.ds(first + c, rows, stride=per_token), :] for c in range(LANE_BLOCKS)], axis=1)


def _ada_kernel(c_ref, w_ref, b_ref, o_ref):
    c = c_ref[...]
    s = c * jax.nn.sigmoid(c)
    o_ref[...] = jnp.dot(s, w_ref[...], precision=lax.Precision.HIGHEST,
                         preferred_element_type=F32) + b_ref[...]


def _ada_mod(c_all, w_ada, b_ada):
    depth, d, six_d = w_ada.shape
    bt = c_all.shape[0]
    nt = six_d // d
    return pl.pallas_call(
        _ada_kernel,
        grid=(depth, nt),
        in_specs=[pl.BlockSpec((bt, d), lambda l, j: (0, 0)),
                  pl.BlockSpec((None, d, d), lambda l, j: (l, 0, j)),
                  pl.BlockSpec((None, 1, d), lambda l, j: (l, 0, j))],
        out_specs=pl.BlockSpec((None, bt, d), lambda l, j: (l, 0, j)),
        out_shape=jax.ShapeDtypeStruct((depth, bt, six_d), F32),
        compiler_params=_cparams(("arbitrary", "arbitrary")),
        name="ada_mod",
    )(c_all, w_ada, b_ada.reshape(depth, 1, six_d))


def _swap16(x):
    lane = lax.broadcasted_iota(jnp.int32, x.shape, 1)
    return jnp.where((lane % 32) < 16, pltpu.roll(x, LANES - 16, 1), pltpu.roll(x, 16, 1))


def _rope128(x, cos, sin_signed):
    return x * cos + _swap16(x) * sin_signed


def _head_rms(x, gain):
    lane = lax.broadcasted_iota(jnp.int32, x.shape, 1)
    lo = lane < HEAD_DIM
    sq = x * x
    s_lo = jnp.sum(jnp.where(lo, sq, 0.0), axis=-1, keepdims=True)
    s_hi = jnp.sum(jnp.where(lo, 0.0, sq), axis=-1, keepdims=True)
    ms = jnp.where(lo, s_lo, s_hi) * (1.0 / HEAD_DIM)
    return x * lax.rsqrt(ms + EPS) * gain


def _start_expert_row_gather(dest_ref, y_hbm, ybuf, sem, slot):
    def body(t, carry):
        for k in range(2):
            src = y_hbm.at[pl.ds(pl.multiple_of(dest_ref[k, t], SUBLANES), SUBLANES)]
            dst = ybuf.at[slot, pl.ds(pl.multiple_of((2 * t + k) * SUBLANES, SUBLANES), SUBLANES)]
            pltpu.make_async_copy(src, dst, sem.at[slot]).start(priority=k)
        return carry

    lax.fori_loop(0, dest_ref.shape[1], body, 0, unroll=DMA_UNROLL)


def _gated_pair(dest_ref, dest_next_ref, y_hbm, ybuf, sem, gt):
    i = pl.program_id(0)
    rows = gt.shape[0]
    slot = lax.rem(i, 2)

    @pl.when(i == 0)
    def _():
        _start_expert_row_gather(dest_ref, y_hbm, ybuf, sem, 0)

    @pl.when(i + 1 < pl.num_programs(0))
    def _():
        _start_expert_row_gather(dest_next_ref, y_hbm, ybuf, sem, 1 - slot)

    pltpu.make_async_copy(y_hbm.at[pl.ds(0, 2 * rows * SUBLANES)], ybuf.at[slot], sem.at[slot]).wait()
    y_ref = ybuf.at[slot]
    y0 = _load_token_tiles(y_ref, rows, 0, 2 * SUBLANES)
    y1 = _load_token_tiles(y_ref, rows, SUBLANES, 2 * SUBLANES)
    return gt[:, 0:1] * y0 + gt[:, 1:2] * y1


def _moe_gather_specs(tm, n_tiles):
    in_specs = [pl.BlockSpec((2, tm), lambda i: (0, i), memory_space=pltpu.SMEM),
                pl.BlockSpec((2, tm), lambda i: (0, jnp.minimum(i + 1, n_tiles - 1)), memory_space=pltpu.SMEM),
                pl.BlockSpec(memory_space=pl.ANY)]
    scratch = [pltpu.VMEM((2, 2 * tm * SUBLANES, LANES), F32), pltpu.SemaphoreType.DMA((2,))]
    return in_specs, scratch


def _rms_mod(x, gain, scale, shift):
    ms = jnp.mean(x * x, axis=-1, keepdims=True)
    return (x * lax.rsqrt(ms + EPS) * gain) * (1.0 + scale) + shift


def _inproj_kernel(with_moe, *refs):
    if with_moe:
        (x_ref, dest_ref, destn_ref, y_hbm, gt_ref, modp_ref, mod_ref, g_ref, w_ref, qn_ref, kn_ref,
         ca_ref, sa_ref, cb_ref, sb_ref,
         xo_ref, qa_ref, ka_ref, va_ref, qb_ref, kb_ref, vb_ref, cp_ref, ybuf, ysem) = refs
    else:
        (x_ref, mod_ref, g_ref, w_ref, qn_ref, kn_ref, ca_ref, sa_ref, cb_ref, sb_ref,
         qa_ref, ka_ref, va_ref, qb_ref, kb_ref, vb_ref, cp_ref) = refs
    x = x_ref[...]
    if with_moe:
        x = x + modp_ref[5:6, :] * _gated_pair(dest_ref, destn_ref, y_hbm, ybuf, ysem, gt_ref[...])
        xo_ref[...] = x
    h = _rms_mod(x, g_ref[...], mod_ref[1:2, :], mod_ref[0:1, :])
    proj = jnp.dot(h.astype(BF16), w_ref[...], preferred_element_type=F32)

    ca, sa, cb, sb = ca_ref[...], sa_ref[...], cb_ref[...], sb_ref[...]
    qa_scale = (A_QK_DIM ** -0.5) * LOG2E
    qb_scale = (HEAD_DIM ** -0.5) * LOG2E
    for i in range(A_WIDTH // LANES):
        sl = slice(i * LANES, (i + 1) * LANES)
        qa_ref[:, sl] = (_rope128(proj[:, sl], ca, sa) * qa_scale).astype(BF16)
        ka_ref[:, sl] = _rope128(proj[:, A_WIDTH + i * LANES:A_WIDTH + (i + 1) * LANES], ca, sa).astype(BF16)
    va_ref[...] = proj[:, 2 * A_WIDTH:3 * A_WIDTH].astype(BF16)
    o = 3 * A_WIDTH
    for j in range(B_WIDTH // LANES):
        q = _head_rms(proj[:, o + j * LANES:o + (j + 1) * LANES], qn_ref[...])
        qb_ref[:, j * LANES:(j + 1) * LANES] = (_rope128(q, cb, sb) * qb_scale).astype(BF16)
    o += B_WIDTH
    k = _head_rms(proj[:, o:o + B_KV_WIDTH], kn_ref[...])
    kb_ref[...] = _rope128(k, cb, sb).astype(BF16)
    o += B_KV_WIDTH
    vb_ref[...] = proj[:, o:o + B_KV_WIDTH].astype(BF16)
    o += B_KV_WIDTH
    cp_ref[:, :C_WIDTH] = proj[:, o:o + C_WIDTH]
    cp_ref[:, C_WIDTH:] = proj[:, o + C_WIDTH:o + 2 * C_WIDTH] * proj[:, o + 2 * C_WIDTH:o + 3 * C_WIDTH]


def _inproj(x, moe, mod_l, g1, w, qn, kn, tabs, seq):
    n, d = x.shape
    tm = TOKEN_TILE
    tiles_per_seq = seq // tm
    row = lambda i: (i, 0)
    const = lambda i: (0, 0)
    pos = lambda i: (i % tiles_per_seq, 0)
    in_specs = [pl.BlockSpec((tm, d), row)]
    args = [x]
    scratch = []
    if moe is not None:
        dest8, y8, gates, mod_prev = moe
        gather_specs, scratch = _moe_gather_specs(tm, n // tm)
        in_specs += gather_specs + [pl.BlockSpec((tm, 2), row),
                                    pl.BlockSpec((None, 6, d), lambda i: (i // tiles_per_seq, 0, 0))]
        args += [dest8, dest8, y8, gates, mod_prev]
    in_specs += [pl.BlockSpec((None, 6, d), lambda i: (i // tiles_per_seq, 0, 0)),
                 pl.BlockSpec((1, d), const),
                 pl.BlockSpec((d, IN_COLS), const),
                 pl.BlockSpec((1, LANES), const), pl.BlockSpec((1, LANES), const)]
    in_specs += [pl.BlockSpec((tm, LANES), pos)] * 4
    args += [mod_l, g1, w, qn, kn, *tabs]
    widths = [(A_WIDTH, BF16), (A_WIDTH, BF16), (A_WIDTH, BF16), (B_WIDTH, BF16), (B_KV_WIDTH, BF16),
              (B_KV_WIDTH, BF16), (2 * C_WIDTH, F32)]
    out_specs = [pl.BlockSpec((tm, wd), row) for wd, _ in widths]
    out_shape = [jax.ShapeDtypeStruct((n, wd), dt) for wd, dt in widths]
    if moe is not None:
        out_specs = [pl.BlockSpec((tm, d), row)] + out_specs
        out_shape = [jax.ShapeDtypeStruct((n, d), F32)] + out_shape
    outs = pl.pallas_call(
        functools.partial(_inproj_kernel, moe is not None),
        grid=(n // tm,),
        in_specs=in_specs, out_specs=out_specs, out_shape=out_shape, scratch_shapes=scratch,
        compiler_params=_cparams(("arbitrary",)),
        name="inproj",
    )(*args)
    if moe is None:
        outs = [x] + list(outs)
    return outs


def _flash(qs, k_ref, v_ref, col, seq):
    rows = qs.shape[0]
    tk = min(KV_TILE, seq)
    sl = slice(col * LANES, (col + 1) * LANES)

    def body(c, carry):
        m, l, acc = carry
        r0 = pl.multiple_of(c * tk, tk)
        k = k_ref[pl.ds(r0, tk), sl]
        v = v_ref[pl.ds(r0, tk), sl]
        s = lax.dot_general(qs, k, (((1,), (1,)), ((), ())), preferred_element_type=F32)
        m_new = jnp.maximum(m, jnp.max(s, axis=-1, keepdims=True))
        alpha = jnp.exp2(m - m_new)
        p = jnp.exp2(s - m_new)
        l = alpha * l + jnp.sum(p, axis=-1, keepdims=True)
        acc = alpha * acc + jnp.dot(p.astype(BF16), v, preferred_element_type=F32)
        return m_new, l, acc

    init = (jnp.full((rows, 1), -jnp.inf, F32), jnp.zeros((rows, 1), F32), jnp.zeros((rows, LANES), F32))
    _, l, acc = lax.fori_loop(0, seq // tk, body, init, unroll=True)
    return acc, l


def _diff_attn_kernel(lambda_init, seq, q_ref, k_ref, v_ref, lam_ref, g_ref, o_ref):
    lv = lam_ref[...]
    lam = (jnp.exp(jnp.sum(lv[0:1] * lv[1:2], keepdims=True))
           - jnp.exp(jnp.sum(lv[2:3] * lv[3:4], keepdims=True)) + lambda_init)
    tq = q_ref.shape[0]
    lane = lax.broadcasted_iota(jnp.int32, (tq, LANES), 1)
    lo = lane < HEAD_DIM
    for i in range(A_WIDTH // LANES):
        sl = slice(i * LANES, (i + 1) * LANES)
        q = q_ref[:, sl]
        zero = jnp.zeros_like(q)
        qs = jnp.concatenate([jnp.where(lane // A_QK_DIM == j, q, zero) for j in range(4)], axis=0)
        acc, l = _flash(qs, k_ref, v_ref, i, seq)
        o = acc / l
        o_even = o[0:tq] - lam * o[tq:2 * tq]
        o_odd = o[2 * tq:3 * tq] - lam * o[3 * tq:4 * tq]
        oo = jnp.where(lo, o_even, o_odd)
        o_ref[:, sl] = (_head_rms(oo, g_ref[...]) * (1.0 - lambda_init)).astype(BF16)


def _diff_attn(qa, ka, va, lam_vecs, subln_tiled, lambda_init, batch, seq):
    tq = Q_TILE
    q3, k3, v3 = (a.reshape(batch, seq, A_WIDTH) for a in (qa, ka, va))
    out = pl.pallas_call(
        functools.partial(_diff_attn_kernel, lambda_init, seq),
        grid=(batch, seq // tq),
        in_specs=[pl.BlockSpec((None, tq, A_WIDTH), lambda b, i: (b, i, 0)),
                  pl.BlockSpec((None, seq, A_WIDTH), lambda b, i: (b, 0, 0)),
                  pl.BlockSpec((None, seq, A_WIDTH), lambda b, i: (b, 0, 0)),
                  pl.BlockSpec((4, A_QK_DIM), lambda b, i: (0, 0)),
                  pl.BlockSpec((1, LANES), lambda b, i: (0, 0))],
        out_specs=pl.BlockSpec((None, tq, A_WIDTH), lambda b, i: (b, i, 0)),
        out_shape=jax.ShapeDtypeStruct((batch, seq, A_WIDTH), BF16),
        compiler_params=_cparams(("arbitrary", "arbitrary")),
        name="diff_attn",
    )(q3, k3, v3, lam_vecs, subln_tiled)
    return out.reshape(batch * seq, A_WIDTH)


def _gqa_kernel(seq, q_ref, k_ref, v_ref, o_ref):
    tq = q_ref.shape[0]
    lane = lax.broadcasted_iota(jnp.int32, (tq, LANES), 1)
    lo = lane < HEAD_DIM
    for j in range(B_WIDTH // LANES):
        q = q_ref[:, j * LANES:(j + 1) * LANES]
        zero = jnp.zeros_like(q)
        qs = jnp.concatenate([jnp.where(lo, q, zero), jnp.where(lo, zero, q)], axis=0)
        acc, l = _flash(qs, k_ref, v_ref, 0, seq)
        o = acc / l
        o_ref[:, j * LANES:(j + 1) * LANES] = jnp.where(lo, o[0:tq], o[tq:2 * tq]).astype(BF16)


def _gqa_attn(qb, kb, vb, batch, seq):
    tq = Q_TILE
    q3 = qb.reshape(batch, seq, B_WIDTH)
    k3 = kb.reshape(batch, seq, B_KV_WIDTH)
    v3 = vb.reshape(batch, seq, B_KV_WIDTH)
    out = pl.pallas_call(
        functools.partial(_gqa_kernel, seq),
        grid=(batch, seq // tq),
        in_specs=[pl.BlockSpec((None, tq, B_WIDTH), lambda b, i: (b, i, 0)),
                  pl.BlockSpec((None, seq, B_KV_WIDTH), lambda b, i: (b, 0, 0)),
                  pl.BlockSpec((None, seq, B_KV_WIDTH), lambda b, i: (b, 0, 0))],
        out_specs=pl.BlockSpec((None, tq, B_WIDTH), lambda b, i: (b, i, 0)),
        out_shape=jax.ShapeDtypeStruct((batch, seq, B_WIDTH), BF16),
        compiler_params=_cparams(("arbitrary", "arbitrary")),
        name="gqa_attn",
    )(q3, k3, v3)
    return out.reshape(batch * seq, B_WIDTH)


def _conv_kernel(cp_ref, prev_ref, next_ref, w_ref, o_ref):
    i = pl.program_id(1)
    last = pl.num_programs(1) - 1
    tc = cp_ref.shape[0]
    gate = cp_ref[:, :C_WIDTH]
    u = cp_ref[:, C_WIDTH:]
    row = lax.broadcasted_iota(jnp.int32, u.shape, 0)
    before = jnp.where(i == 0, 0.0, prev_ref[7:8, C_WIDTH:])
    after = jnp.where(i == last, 0.0, next_ref[0:1, C_WIDTH:])
    u_prev = jnp.where(row == 0, before, pltpu.roll(u, 1, 0))
    u_next = jnp.where(row == tc - 1, after, pltpu.roll(u, tc - 1, 0))
    w = w_ref[...]
    o_ref[...] = (gate * (w[0:1] * u_prev + w[1:2] * u + w[2:3] * u_next)).astype(BF16)


def _short_conv(cp, conv_w, batch, seq):
    tc = CONV_TILE
    sub = 8
    cp3 = cp.reshape(batch, seq, 2 * C_WIDTH)
    nsub = seq // sub
    out = pl.pallas_call(
        _conv_kernel,
        grid=(batch, seq // tc),
        in_specs=[pl.BlockSpec((None, tc, 2 * C_WIDTH), lambda b, i: (b, i, 0)),
                  pl.BlockSpec((None, sub, 2 * C_WIDTH),
                               lambda b, i: (b, jnp.maximum(i * (tc // sub) - 1, 0), 0)),
                  pl.BlockSpec((None, sub, 2 * C_WIDTH),
                               lambda b, i: (b, jnp.minimum((i + 1) * (tc // sub), nsub - 1), 0)),
                  pl.BlockSpec((3, C_WIDTH), lambda b, i: (0, 0))],
        out_specs=pl.BlockSpec((None, tc, C_WIDTH), lambda b, i: (b, i, 0)),
        out_shape=jax.ShapeDtypeStruct((batch, seq, C_WIDTH), BF16),
        compiler_params=_cparams(("arbitrary", "arbitrary")),
        name="short_conv",
    )(cp3, cp3, cp3, conv_w)
    return out.reshape(batch * seq, C_WIDTH)


def _outproj_kernel(oa_ref, ob_ref, oc_ref, x_ref, mod_ref, g_ref, wa_ref, wb_ref, wc_ref,
                    rhi_ref, rlo_ref, rb_ref, tri_ref,
                    x1_ref, h_ref, eid_ref, gate_ref, rank_ref, cnt_ref, run_ref):
    i = pl.program_id(0)
    tm = x_ref.shape[0]

    @pl.when(i == 0)
    def _():
        run_ref[...] = jnp.zeros_like(run_ref)

    mix = (jnp.dot(oa_ref[...], wa_ref[...], preferred_element_type=F32)
           + jnp.dot(ob_ref[...], wb_ref[...], preferred_element_type=F32)
           + jnp.dot(oc_ref[...], wc_ref[...], preferred_element_type=F32))
    x1 = x_ref[...] + mod_ref[2:3, :] * mix
    x1_ref[...] = x1
    h = _rms_mod(x1, g_ref[...], mod_ref[4:5, :], mod_ref[3:4, :])
    _store_token_tiles(h_ref, h)

    hi = h.astype(BF16)
    lo = (h - hi.astype(F32)).astype(BF16)
    logits = (jnp.dot(hi, rhi_ref[...], preferred_element_type=F32)
              + jnp.dot(hi, rlo_ref[...], preferred_element_type=F32)
              + jnp.dot(lo, rhi_ref[...], preferred_element_type=F32)) + rb_ref[...]
    lt = logits.T
    r8 = lax.broadcasted_iota(jnp.int32, (8, tm), 0)
    neg = -jnp.inf
    gl = jnp.where(r8 < N_GROUPS, lt[0:8], neg)
    gmax = jnp.max(gl, axis=0, keepdims=True)
    gidx = jnp.min(jnp.where(gl == gmax, r8, 8), axis=0, keepdims=True)
    p_sel = 1.0 / jnp.sum(jnp.exp(gl - gmax), axis=0, keepdims=True)
    sel = jnp.zeros((8, tm), F32)
    for g in range(N_GROUPS):
        sel = sel + jnp.where(gidx == g, lt[8 + 8 * g:16 + 8 * g], 0.0)
    v1 = jnp.max(sel, axis=0, keepdims=True)
    i1 = jnp.min(jnp.where(sel == v1, r8, 8), axis=0, keepdims=True)
    sel2 = jnp.where(r8 == i1, neg, sel)
    v2 = jnp.max(sel2, axis=0, keepdims=True)
    i2 = jnp.min(jnp.where(sel2 == v2, r8, 8), axis=0, keepdims=True)
    e = jnp.exp(v2 - v1)
    gate_ref[0:1, :] = p_sel / (1.0 + e)
    gate_ref[1:2, :] = p_sel * e / (1.0 + e)
    eid0 = gidx * EXPERTS_PER_GROUP + i1
    eid1 = gidx * EXPERTS_PER_GROUP + i2
    eid_ref[0:1, :] = eid0
    eid_ref[1:2, :] = eid1

    r32 = lax.broadcasted_iota(jnp.int32, (N_EXPERTS, tm), 0)
    oh0 = r32 == eid0
    oh1 = r32 == eid1
    cnt = jnp.where(oh0, 1.0, 0.0) + jnp.where(oh1, 1.0, 0.0)
    before = jnp.dot(cnt.astype(BF16), tri_ref[...], preferred_element_type=F32) + run_ref[:, 0:1]
    rank_ref[0:1, :] = jnp.sum(jnp.where(oh0, before, 0.0), axis=0, keepdims=True).astype(jnp.int32)
    rank_ref[1:2, :] = jnp.sum(jnp.where(oh1, before, 0.0), axis=0, keepdims=True).astype(jnp.int32)
    run = run_ref[...] + jnp.sum(cnt, axis=1, keepdims=True)
    run_ref[...] = run
    cnt_ref[...] = run


def _outproj(oa, ob, oc, x, mod_l, g2, wa, wb, wc, rhi, rlo, rb, tri, seq):
    n, d = x.shape
    tm = TOKEN_TILE
    tiles_per_seq = seq // tm
    row = lambda i: (i, 0)
    col = lambda i: (0, i)
    const = lambda i: (0, 0)
    return pl.pallas_call(
        _outproj_kernel,
        grid=(n // tm,),
        in_specs=[pl.BlockSpec((tm, A_WIDTH), row), pl.BlockSpec((tm, B_WIDTH), row),
                  pl.BlockSpec((tm, C_WIDTH), row), pl.BlockSpec((tm, d), row),
                  pl.BlockSpec((None, 6, d), lambda i: (i // tiles_per_seq, 0, 0)),
                  pl.BlockSpec((1, d), const),
                  pl.BlockSpec((A_WIDTH, d), const), pl.BlockSpec((B_WIDTH, d), const),
                  pl.BlockSpec((C_WIDTH, d), const),
                  pl.BlockSpec((d, LANES), const), pl.BlockSpec((d, LANES), const),
                  pl.BlockSpec((1, LANES), const), pl.BlockSpec((tm, tm), const)],
        out_specs=[pl.BlockSpec((tm, d), row), pl.BlockSpec((tm * SUBLANES, LANES), row),
                   pl.BlockSpec((2, tm), col), pl.BlockSpec((2, tm), col), pl.BlockSpec((2, tm), col),
                   pl.BlockSpec((N_EXPERTS, LANES), const)],
        out_shape=[jax.ShapeDtypeStruct((n, d), F32), jax.ShapeDtypeStruct((n * SUBLANES, LANES), F32),
                   jax.ShapeDtypeStruct((2, n), jnp.int32), jax.ShapeDtypeStruct((2, n), F32),
                   jax.ShapeDtypeStruct((2, n), jnp.int32),
                   jax.ShapeDtypeStruct((N_EXPERTS, LANES), F32)],
        scratch_shapes=[pltpu.VMEM((N_EXPERTS, LANES), F32)],
        compiler_params=_cparams(("arbitrary",)),
        name="outproj_router",
    )(oa, ob, oc, x, mod_l, g2, wa, wb, wc, rhi, rlo, rb, tri)


def _dispatch_kernel(dest_ref, h_ref, buf_in_ref, buf_ref, sem):
    del buf_in_ref
    td = dest_ref.shape[1]

    def body(t, carry):
        src = h_ref.at[pl.ds(pl.multiple_of(t * SUBLANES, SUBLANES), SUBLANES)]
        for k in range(2):
            dst = buf_ref.at[pl.ds(pl.multiple_of(dest_ref[k, t], SUBLANES), SUBLANES)]
            pltpu.make_async_copy(src, dst, sem).start(priority=k)
        return carry

    lax.fori_loop(0, td, body, 0, unroll=DMA_UNROLL)
    for _ in range(2):
        pltpu.make_async_copy(h_ref, buf_ref.at[pl.ds(0, td * SUBLANES)], sem).wait()


def _dispatch(dest8, h8, buf_init):
    n = dest8.shape[1]
    td = DMA_TILE
    return pl.pallas_call(
        _dispatch_kernel,
        grid=(n // td,),
        in_specs=[pl.BlockSpec((2, td), lambda i: (0, i), memory_space=pltpu.SMEM),
                  pl.BlockSpec((td * SUBLANES, LANES), lambda i: (i, 0)),
                  pl.BlockSpec(memory_space=pl.ANY)],
        out_specs=pl.BlockSpec(memory_space=pl.ANY),
        out_shape=jax.ShapeDtypeStruct(buf_init.shape, F32),
        scratch_shapes=[pltpu.SemaphoreType.DMA(())],
        input_output_aliases={2: 0},
        compiler_params=pltpu.CompilerParams(dimension_semantics=("arbitrary",), has_side_effects=True),
        name="moe_dispatch",
    )(dest8, h8, buf_init)


def _expert_kernel(be_ref, nu_ref, x_ref, w1_ref, w3_ref, w2_ref, y_ref, w1b, w3b, w2b):
    i = pl.program_id(0)
    rb = x_ref.shape[0] // SUBLANES
    changed = jnp.logical_or(i == 0, be_ref[i] != be_ref[jnp.maximum(i - 1, 0)])

    @pl.when(changed)
    def _():
        w1b[...] = w1_ref[...].astype(BF16)
        w3b[...] = w3_ref[...].astype(BF16)
        w2b[...] = w2_ref[...].astype(BF16)

    @pl.when(i < nu_ref[0])
    def _():
        x = _load_token_tiles(x_ref, rb).astype(BF16)
        a = jnp.dot(x, w1b[...], preferred_element_type=F32)
        b = jnp.dot(x, w3b[...], preferred_element_type=F32)
        act = (a * jax.nn.sigmoid(a) * b).astype(BF16)
        _store_token_tiles(y_ref, jnp.dot(act, w2b[...], preferred_element_type=F32))

    @pl.when(i >= nu_ref[0])
    def _():
        y_ref[...] = jnp.zeros_like(y_ref)


def _experts(block_e, n_used, buf8, w1, w3, w2, layer):
    cap = buf8.shape[0] // SUBLANES
    d = D_MODEL
    rb = EXPERT_ROWS
    de = w1.shape[-1]
    grid_spec = pltpu.PrefetchScalarGridSpec(
        num_scalar_prefetch=2,
        grid=(cap // rb,),
        in_specs=[pl.BlockSpec((rb * SUBLANES, LANES), lambda i, be, nu: (i, 0)),
                  pl.BlockSpec((None, None, d, de), lambda i, be, nu: (layer, be[i], 0, 0)),
                  pl.BlockSpec((None, None, d, de), lambda i, be, nu: (layer, be[i], 0, 0)),
                  pl.BlockSpec((None, None, de, d), lambda i, be, nu: (layer, be[i], 0, 0))],
        out_specs=pl.BlockSpec((rb * SUBLANES, LANES), lambda i, be, nu: (i, 0)),
        scratch_shapes=[pltpu.VMEM((d, de), BF16), pltpu.VMEM((d, de), BF16), pltpu.VMEM((de, d), BF16)],
    )
    return pl.pallas_call(
        _expert_kernel,
        grid_spec=grid_spec,
        out_shape=jax.ShapeDtypeStruct((cap * SUBLANES, LANES), F32),
        compiler_params=_cparams(("arbitrary",)),
        name="moe_experts",
    )(block_e, n_used, buf8, w1, w3, w2)


def _final_kernel(x_ref, dest_ref, destn_ref, y_hbm, gt_ref, mod_ref, g_ref, o_ref, ybuf, ysem):
    x = x_ref[...] + mod_ref[5:6, :] * _gated_pair(dest_ref, destn_ref, y_hbm, ybuf, ysem, gt_ref[...])
    ms = jnp.mean(x * x, axis=-1, keepdims=True)
    o_ref[...] = x * lax.rsqrt(ms + EPS) * g_ref[...]


def _final(x1, dest8, y8, gates, mod_l, g, seq):
    n, d = x1.shape
    tm = TOKEN_TILE
    tiles_per_seq = seq // tm
    row = lambda i: (i, 0)
    gather_specs, scratch = _moe_gather_specs(tm, n // tm)
    return pl.pallas_call(
        _final_kernel,
        grid=(n // tm,),
        in_specs=[pl.BlockSpec((tm, d), row)] + gather_specs + [
            pl.BlockSpec((tm, 2), row),
            pl.BlockSpec((None, 6, d), lambda i: (i // tiles_per_seq, 0, 0)),
            pl.BlockSpec((1, d), lambda i: (0, 0))],
        out_specs=pl.BlockSpec((tm, d), row),
        out_shape=jax.ShapeDtypeStruct((n, d), F32),
        scratch_shapes=scratch,
        compiler_params=_cparams(("arbitrary",)),
        name="final_norm",
    )(x1, dest8, dest8, y8, gates, mod_l, g)


def _rope_tables(seq):
    freqs = ROPE_THETA ** (-np.arange(0, A_QK_DIM, 2, dtype=np.float32) / A_QK_DIM)
    lane = np.arange(LANES)
    f = freqs[lane % 16][None, :]
    sign = np.where((lane % 32) < 16, -1.0, 1.0)[None, :].astype(np.float32)
    pos = np.arange(seq, dtype=np.float32)[:, None]
    ang_a = pos * f
    axial = np.where((lane % HEAD_DIM) < 32, np.floor(pos / GRID_W), np.mod(pos, GRID_W)).astype(np.float32)
    ang_b = axial * f
    tabs = (np.cos(ang_a), np.sin(ang_a) * sign, np.cos(ang_b), np.sin(ang_b) * sign)
    return tuple(jnp.asarray(t, dtype=F32) for t in tabs)


def _qb_perm():
    cols = []
    for j in range(B_HEADS // B_KV_HEADS):
        cols += list(range(j * HEAD_DIM, (j + 1) * HEAD_DIM))
        cols += list(range((4 + j) * HEAD_DIM, (5 + j) * HEAD_DIM))
    return np.asarray(cols, dtype=np.int32)


def _trunk(x, mod, batch, seq, p):
    n = batch * seq
    d = D_MODEL
    depth = p["w_in"].shape[0]
    tabs = _rope_tables(seq)
    perm = _qb_perm()
    tri = jnp.asarray(np.triu(np.ones((TOKEN_TILE, TOKEN_TILE), np.float32), 1), dtype=BF16)
    rb = EXPERT_ROWS
    cap = (2 * n + N_EXPERTS * (rb - 1) + rb - 1) // rb * rb
    nb = cap // rb
    x = x.reshape(n, d)
    moe = None
    buf8 = None
    for l in range(depth):
        lambda_init = 0.8 - 0.6 * math.exp(-0.3 * l)
        mod_l = mod[l].reshape(batch, 6, d)
        w_in = p["w_in"][l]
        qcols = 3 * A_WIDTH + perm
        w_in = jnp.concatenate([w_in[:, :3 * A_WIDTH], w_in[:, qcols], w_in[:, 3 * A_WIDTH + B_WIDTH:]],
                               axis=1).astype(BF16)
        tile2 = lambda v: jnp.tile(v.reshape(1, HEAD_DIM), (1, 2))
        x, qa, ka, va, qb, kb, vb, cp = _inproj(x, moe, mod_l, p["norm1"][l].reshape(1, d), w_in,
                                                tile2(p["q_norm"][l]), tile2(p["k_norm"][l]), tabs, seq)
        oa = _diff_attn(qa, ka, va, p["diff_lambda"][l], tile2(p["diff_subln"][l]), lambda_init, batch, seq)
        ob = _gqa_attn(qb, kb, vb, batch, seq)
        oc = _short_conv(cp, p["conv_w"][l], batch, seq)
        w_out = p["w_out"][l]
        wa = w_out[:A_WIDTH].astype(BF16)
        wb = w_out[A_WIDTH + perm].astype(BF16)
        wc = w_out[A_WIDTH + B_WIDTH:].astype(BF16)
        wr = jnp.zeros((d, LANES), F32)
        wr = wr.at[:, :N_GROUPS].set(p["router_group_w"][l]).at[:, 8:8 + N_EXPERTS].set(p["router_expert_w"][l])
        rhi = wr.astype(BF16)
        rlo = (wr - rhi.astype(F32)).astype(BF16)
        rbias = jnp.zeros((1, LANES), F32)
        rbias = rbias.at[0, :N_GROUPS].set(p["router_group_b"][l]).at[0, 8:8 + N_EXPERTS].set(p["router_expert_b"][l])
        x1, h8, eid, gate, rank, cnt = _outproj(oa, ob, oc, x, mod_l, p["norm2"][l].reshape(1, d),
                                                wa, wb, wc, rhi, rlo, rbias, tri, seq)
        counts = cnt[:, 0].astype(jnp.int32)
        pcounts = (counts + rb - 1) // rb * rb
        pends = jnp.cumsum(pcounts)
        pstarts = pends - pcounts
        block_row = jnp.arange(nb, dtype=jnp.int32)[:, None] * rb
        block_e = jnp.minimum(jnp.sum((pends[None, :] <= block_row).astype(jnp.int32), axis=1), N_EXPERTS - 1)
        n_used = (pends[-1:] // rb).astype(jnp.int32)
        expert_ids = jnp.arange(N_EXPERTS, dtype=jnp.int32)[:, None, None]
        dest8 = (jnp.sum(jnp.where(eid[None] == expert_ids, pstarts[:, None, None], 0), axis=0) + rank) * SUBLANES
        if buf8 is None:
            buf8 = jnp.zeros((cap * SUBLANES, LANES), F32)
        buf8 = _dispatch(dest8, h8, buf8)
        y8 = _experts(block_e, n_used, buf8, p["expert_w1"], p["expert_w3"], p["expert_w2"], l)
        x = x1
        moe = (dest8, y8, gate.T, mod_l)
    out = _final(x, *moe, p["final_norm"].reshape(1, d), seq)
    return out.reshape(batch, seq, d)


def kernel(x_prompt, x_sample, c_prompt, c_sample, w_ada, b_ada, norm1, norm2, w_in, w_out, diff_lambda,
           diff_subln, q_norm, k_norm, conv_w, router_group_w, router_group_b, router_expert_w,
           router_expert_b, expert_w1, expert_w3, expert_w2, final_norm):
    p = dict(norm1=norm1, norm2=norm2, w_in=w_in, w_out=w_out, diff_lambda=diff_lambda, diff_subln=diff_subln,
             q_norm=q_norm, k_norm=k_norm, conv_w=conv_w, router_group_w=router_group_w,
             router_group_b=router_group_b, router_expert_w=router_expert_w, router_expert_b=router_expert_b,
             expert_w1=expert_w1, expert_w3=expert_w3, expert_w2=expert_w2, final_norm=final_norm)
    bp, sp, _ = x_prompt.shape
    bs, ss, _ = x_sample.shape
    mod = _ada_mod(jnp.concatenate([c_prompt, c_sample], axis=0), w_ada, b_ada)
    y_prompt = _trunk(x_prompt, mod[:, :bp], bp, sp, p)
    y_sample = _trunk(x_sample, mod[:, bp:], bs, ss, p)
    return (y_prompt, y_sample)
```

```python
import functools
import math

import numpy as np
import jax
import jax.numpy as jnp
from jax import lax
from jax.experimental import pallas as pl
from jax.experimental.pallas import tpu as pltpu

F32 = jnp.float32
BF16 = jnp.bfloat16

D_MODEL = 1024
HEAD_DIM = 64
A_HEADS = 4
A_QK_DIM = 32
A_WIDTH = 256
B_HEADS = 8
B_KV_HEADS = 2
B_WIDTH = 512
B_KV_WIDTH = 128
C_WIDTH = 256
IN_COLS = 2304
N_GROUPS = 4
EXPERTS_PER_GROUP = 8
N_EXPERTS = 32
D_EXPERT = 512
GRID_W = 64
ROPE_THETA = 10000.0
EPS = 1e-6
LANES = 128
LOG2E = 1.4426950408889634

VMEM_LIMIT = 56 * 1024 * 1024

TOKEN_TILE = 512
Q_TILE = 256
KV_TILE = 2048
CONV_TILE = 512
EXPERT_ROWS = 512
EXPERT_COL_CHUNK = 256
DMA_TILE = 1024
DMA_UNROLL = 8


def _cparams(sem):
    return pltpu.CompilerParams(dimension_semantics=sem, vmem_limit_bytes=VMEM_LIMIT)


SUBLANES = 8
LANE_BLOCKS = D_MODEL // LANES
assert LANE_BLOCKS == SUBLANES


def _store_token_tiles(ref, val, first=0, per_token=SUBLANES):
    rows = val.shape[0]
    for c in range(LANE_BLOCKS):
        ref[pl.ds(first + c, rows, stride=per_token), :] = val[:, c * LANES:(c + 1) * LANES]


def _load_token_tiles(ref, rows, first=0, per_token=SUBLANES):
    return jnp.concatenate([ref[pl.ds(first + c, rows, stride=per_token), :] for c in range(LANE_BLOCKS)], axis=1)


def _ada_kernel(c_ref, w_ref, b_ref, o_ref):
    c = c_ref[...]
    s = c * jax.nn.sigmoid(c)
    o_ref[...] = jnp.dot(s, w_ref[...], precision=lax.Precision.HIGHEST,
                         preferred_element_type=F32) + b_ref[...]


def _ada_mod(c_all, w_ada, b_ada):
    depth, d, six_d = w_ada.shape
    bt = c_all.shape[0]
    nt = six_d // d
    return pl.pallas_call(
        _ada_kernel,
        grid=(depth, nt),
        in_specs=[pl.BlockSpec((bt, d), lambda l, j: (0, 0)),
                  pl.BlockSpec((None, d, d), lambda l, j: (l, 0, j)),
                  pl.BlockSpec((None, 1, d), lambda l, j: (l, 0, j))],
        out_specs=pl.BlockSpec((None, bt, d), lambda l, j: (l, 0, j)),
        out_shape=jax.ShapeDtypeStruct((depth, bt, six_d), F32),
        compiler_params=_cparams(("arbitrary", "arbitrary")),
        name="ada_mod",
    )(c_all, w_ada, b_ada.reshape(depth, 1, six_d))


def _swap16(x):
    lane = lax.broadcasted_iota(jnp.int32, x.shape, 1)
    return jnp.where((lane % 32) < 16, pltpu.roll(x, LANES - 16, 1), pltpu.roll(x, 16, 1))


def _rope128(x, cos, sin_signed):
    return x * cos + _swap16(x) * sin_signed


def _head_rms(x, gain):
    lane = lax.broadcasted_iota(jnp.int32, x.shape, 1)
    lo = lane < HEAD_DIM
    sq = x * x
    s_lo = jnp.sum(jnp.where(lo, sq, 0.0), axis=-1, keepdims=True)
    s_hi = jnp.sum(jnp.where(lo, 0.0, sq), axis=-1, keepdims=True)
    ms = jnp.where(lo, s_lo, s_hi) * (1.0 / HEAD_DIM)
    return x * lax.rsqrt(ms + EPS) * gain


def _start_expert_row_gather(dest_ref, y_hbm, ybuf, sem, slot):
    def body(t, carry):
        for k in range(2):
            src = y_hbm.at[pl.ds(pl.multiple_of(dest_ref[k, t], SUBLANES), SUBLANES)]
            dst = ybuf.at[slot, pl.ds(pl.multiple_of((2 * t + k) * SUBLANES, SUBLANES), SUBLANES)]
            pltpu.make_async_copy(src, dst, sem.at[slot]).start(priority=k)
        return carry

    lax.fori_loop(0, dest_ref.shape[1], body, 0, unroll=DMA_UNROLL)


def _gated_pair(dest_ref, dest_next_ref, y_hbm, ybuf, sem, gt):
    i = pl.program_id(0)
    rows = gt.shape[0]
    slot = lax.rem(i, 2)

    @pl.when(i == 0)
    def _():
        _start_expert_row_gather(dest_ref, y_hbm, ybuf, sem, 0)

    @pl.when(i + 1 < pl.num_programs(0))
    def _():
        _start_expert_row_gather(dest_next_ref, y_hbm, ybuf, sem, 1 - slot)

    pltpu.make_async_copy(y_hbm.at[pl.ds(0, 2 * rows * SUBLANES)], ybuf.at[slot], sem.at[slot]).wait()
    y_ref = ybuf.at[slot]
    y0 = _load_token_tiles(y_ref, rows, 0, 2 * SUBLANES)
    y1 = _load_token_tiles(y_ref, rows, SUBLANES, 2 * SUBLANES)
    return gt[:, 0:1] * y0 + gt[:, 1:2] * y1


def _moe_gather_specs(tm, n_tiles):
    in_specs = [pl.BlockSpec((2, tm), lambda i: (0, i), memory_space=pltpu.SMEM),
                pl.BlockSpec((2, tm), lambda i: (0, jnp.minimum(i + 1, n_tiles - 1)), memory_space=pltpu.SMEM),
                pl.BlockSpec(memory_space=pl.ANY)]
    scratch = [pltpu.VMEM((2, 2 * tm * SUBLANES, LANES), F32), pltpu.SemaphoreType.DMA((2,))]
    return in_specs, scratch


def _rms_mod(x, gain, scale, shift):
    ms = jnp.mean(x * x, axis=-1, keepdims=True)
    return (x * lax.rsqrt(ms + EPS) * gain) * (1.0 + scale) + shift


def _inproj_kernel(with_moe, *refs):
    if with_moe:
        (x_ref, dest_ref, destn_ref, y_hbm, gt_ref, modp_ref, mod_ref, g_ref, w_ref, qn_ref, kn_ref,
         ca_ref, sa_ref, cb_ref, sb_ref,
         xo_ref, qa_ref, ka_ref, va_ref, qb_ref, kb_ref, vb_ref, cp_ref, ybuf, ysem) = refs
    else:
        (x_ref, mod_ref, g_ref, w_ref, qn_ref, kn_ref, ca_ref, sa_ref, cb_ref, sb_ref,
         qa_ref, ka_ref, va_ref, qb_ref, kb_ref, vb_ref, cp_ref) = refs
    x = x_ref[...]
    if with_moe:
        x = x + modp_ref[5:6, :] * _gated_pair(dest_ref, destn_ref, y_hbm, ybuf, ysem, gt_ref[...])
        xo_ref[...] = x
    h = _rms_mod(x, g_ref[...], mod_ref[1:2, :], mod_ref[0:1, :])
    hb = h.astype(BF16)

    def proj(first, width):
        return jnp.dot(hb, w_ref[:, first:first + width], preferred_element_type=F32)

    ca, sa, cb, sb = ca_ref[...], sa_ref[...], cb_ref[...], sb_ref[...]
    qa_scale = (A_QK_DIM ** -0.5) * LOG2E
    qb_scale = (HEAD_DIM ** -0.5) * LOG2E
    pa = proj(0, 3 * A_WIDTH)
    for i in range(A_WIDTH // LANES):
        sl = slice(i * LANES, (i + 1) * LANES)
        qa_ref[:, sl] = (_rope128(pa[:, sl], ca, sa) * qa_scale).astype(BF16)
        ka_ref[:, sl] = _rope128(pa[:, A_WIDTH + i * LANES:A_WIDTH + (i + 1) * LANES], ca, sa).astype(BF16)
    va_ref[...] = pa[:, 2 * A_WIDTH:3 * A_WIDTH].astype(BF16)
    o = 3 * A_WIDTH
    pb = proj(o, B_WIDTH + 2 * B_KV_WIDTH)
    for j in range(B_WIDTH // LANES):
        q = _head_rms(pb[:, j * LANES:(j + 1) * LANES], qn_ref[...])
        qb_ref[:, j * LANES:(j + 1) * LANES] = (_rope128(q, cb, sb) * qb_scale).astype(BF16)
    k = _head_rms(pb[:, B_WIDTH:B_WIDTH + B_KV_WIDTH], kn_ref[...])
    kb_ref[...] = _rope128(k, cb, sb).astype(BF16)
    vb_ref[...] = pb[:, B_WIDTH + B_KV_WIDTH:].astype(BF16)
    o += B_WIDTH + 2 * B_KV_WIDTH
    pc = proj(o, 3 * C_WIDTH)
    cp_ref[:, :C_WIDTH] = pc[:, :C_WIDTH]
    cp_ref[:, C_WIDTH:] = pc[:, C_WIDTH:2 * C_WIDTH] * pc[:, 2 * C_WIDTH:]


def _inproj(x, moe, mod_l, g1, w, qn, kn, tabs, seq):
    n, d = x.shape
    tm = TOKEN_TILE
    tiles_per_seq = seq // tm
    row = lambda i: (i, 0)
    const = lambda i: (0, 0)
    pos = lambda i: (i % tiles_per_seq, 0)
    in_specs = [pl.BlockSpec((tm, d), row)]
    args = [x]
    scratch = []
    if moe is not None:
        dest8, y8, gates, mod_prev = moe
        gather_specs, scratch = _moe_gather_specs(tm, n // tm)
        in_specs += gather_specs + [pl.BlockSpec((tm, 2), row),
                                    pl.BlockSpec((None, 6, d), lambda i: (i // tiles_per_seq, 0, 0))]
        args += [dest8, dest8, y8, gates, mod_prev]
    in_specs += [pl.BlockSpec((None, 6, d), lambda i: (i // tiles_per_seq, 0, 0)),
                 pl.BlockSpec((1, d), const),
                 pl.BlockSpec((d, IN_COLS), const),
                 pl.BlockSpec((1, LANES), const), pl.BlockSpec((1, LANES), const)]
    in_specs += [pl.BlockSpec((tm, LANES), pos)] * 4
    args += [mod_l, g1, w, qn, kn, *tabs]
    widths = [(A_WIDTH, BF16), (A_WIDTH, BF16), (A_WIDTH, BF16), (B_WIDTH, BF16), (B_KV_WIDTH, BF16),
              (B_KV_WIDTH, BF16), (2 * C_WIDTH, F32)]
    out_specs = [pl.BlockSpec((tm, wd), row) for wd, _ in widths]
    out_shape = [jax.ShapeDtypeStruct((n, wd), dt) for wd, dt in widths]
    if moe is not None:
        out_specs = [pl.BlockSpec((tm, d), row)] + out_specs
        out_shape = [jax.ShapeDtypeStruct((n, d), F32)] + out_shape
    outs = pl.pallas_call(
        functools.partial(_inproj_kernel, moe is not None),
        grid=(n // tm,),
        in_specs=in_specs, out_specs=out_specs, out_shape=out_shape, scratch_shapes=scratch,
        compiler_params=_cparams(("arbitrary",)),
        name="inproj",
    )(*args)
    if moe is None:
        outs = [x] + list(outs)
    return outs


def _flash(qs, k_ref, v_ref, col, seq):
    rows = qs.shape[0]
    tk = min(KV_TILE, seq)
    sl = slice(col * LANES, (col + 1) * LANES)

    def body(c, carry):
        m, l, acc = carry
        r0 = pl.multiple_of(c * tk, tk)
        k = k_ref[pl.ds(r0, tk), sl]
        v = v_ref[pl.ds(r0, tk), sl]
        s = lax.dot_general(qs, k, (((1,), (1,)), ((), ())), preferred_element_type=F32)
        m_new = jnp.maximum(m, jnp.max(s, axis=-1, keepdims=True))
        alpha = jnp.exp2(m - m_new)
        p = jnp.exp2(s - m_new)
        l = alpha * l + jnp.sum(p, axis=-1, keepdims=True)
        acc = alpha * acc + jnp.dot(p.astype(BF16), v, preferred_element_type=F32)
        return m_new, l, acc

    init = (jnp.full((rows, 1), -jnp.inf, F32), jnp.zeros((rows, 1), F32), jnp.zeros((rows, LANES), F32))
    _, l, acc = lax.fori_loop(0, seq // tk, body, init, unroll=True)
    return acc, l


def _diff_attn_kernel(lambda_init, seq, q_ref, k_ref, v_ref, lam_ref, g_ref, o_ref):
    lv = lam_ref[...]
    lam = (jnp.exp(jnp.sum(lv[0:1] * lv[1:2], keepdims=True))
           - jnp.exp(jnp.sum(lv[2:3] * lv[3:4], keepdims=True)) + lambda_init)
    tq = q_ref.shape[0]
    lane = lax.broadcasted_iota(jnp.int32, (tq, LANES), 1)
    lo = lane < HEAD_DIM
    for i in range(A_WIDTH // LANES):
        sl = slice(i * LANES, (i + 1) * LANES)
        q = q_ref[:, sl]
        zero = jnp.zeros_like(q)
        qs = jnp.concatenate([jnp.where(lane // A_QK_DIM == j, q, zero) for j in range(4)], axis=0)
        acc, l = _flash(qs, k_ref, v_ref, i, seq)
        o = acc / l
        o_even = o[0:tq] - lam * o[tq:2 * tq]
        o_odd = o[2 * tq:3 * tq] - lam * o[3 * tq:4 * tq]
        oo = jnp.where(lo, o_even, o_odd)
        o_ref[:, sl] = (_head_rms(oo, g_ref[...]) * (1.0 - lambda_init)).astype(BF16)


def _diff_attn(qa, ka, va, lam_vecs, subln_tiled, lambda_init, batch, seq):
    tq = Q_TILE
    q3, k3, v3 = (a.reshape(batch, seq, A_WIDTH) for a in (qa, ka, va))
    out = pl.pallas_call(
        functools.partial(_diff_attn_kernel, lambda_init, seq),
        grid=(batch, seq // tq),
        in_specs=[pl.BlockSpec((None, tq, A_WIDTH), lambda b, i: (b, i, 0)),
                  pl.BlockSpec((None, seq, A_WIDTH), lambda b, i: (b, 0, 0)),
                  pl.BlockSpec((None, seq, A_WIDTH), lambda b, i: (b, 0, 0)),
                  pl.BlockSpec((4, A_QK_DIM), lambda b, i: (0, 0)),
                  pl.BlockSpec((1, LANES), lambda b, i: (0, 0))],
        out_specs=pl.BlockSpec((None, tq, A_WIDTH), lambda b, i: (b, i, 0)),
        out_shape=jax.ShapeDtypeStruct((batch, seq, A_WIDTH), BF16),
        compiler_params=_cparams(("arbitrary", "arbitrary")),
        name="diff_attn",
    )(q3, k3, v3, lam_vecs, subln_tiled)
    return out.reshape(batch * seq, A_WIDTH)


def _gqa_kernel(seq, q_ref, k_ref, v_ref, o_ref):
    tq = q_ref.shape[0]
    lane = lax.broadcasted_iota(jnp.int32, (tq, LANES), 1)
    lo = lane < HEAD_DIM
    for j in range(B_WIDTH // LANES):
        q = q_ref[:, j * LANES:(j + 1) * LANES]
        zero = jnp.zeros_like(q)
        qs = jnp.concatenate([jnp.where(lo, q, zero), jnp.where(lo, zero, q)], axis=0)
        acc, l = _flash(qs, k_ref, v_ref, 0, seq)
        o = acc / l
        o_ref[:, j * LANES:(j + 1) * LANES] = jnp.where(lo, o[0:tq], o[tq:2 * tq]).astype(BF16)


def _gqa_attn(qb, kb, vb, batch, seq):
    tq = 2 * Q_TILE
    q3 = qb.reshape(batch, seq, B_WIDTH)
    k3 = kb.reshape(batch, seq, B_KV_WIDTH)
    v3 = vb.reshape(batch, seq, B_KV_WIDTH)
    out = pl.pallas_call(
        functools.partial(_gqa_kernel, seq),
        grid=(batch, seq // tq),
        in_specs=[pl.BlockSpec((None, tq, B_WIDTH), lambda b, i: (b, i, 0)),
                  pl.BlockSpec((None, seq, B_KV_WIDTH), lambda b, i: (b, 0, 0)),
                  pl.BlockSpec((None, seq, B_KV_WIDTH), lambda b, i: (b, 0, 0))],
        out_specs=pl.BlockSpec((None, tq, B_WIDTH), lambda b, i: (b, i, 0)),
        out_shape=jax.ShapeDtypeStruct((batch, seq, B_WIDTH), BF16),
        compiler_params=_cparams(("arbitrary", "arbitrary")),
        name="gqa_attn",
    )(q3, k3, v3)
    return out.reshape(batch * seq, B_WIDTH)


def _conv_kernel(cp_ref, prev_ref, next_ref, w_ref, o_ref):
    i = pl.program_id(1)
    last = pl.num_programs(1) - 1
    tc = cp_ref.shape[0]
    gate = cp_ref[:, :C_WIDTH]
    u = cp_ref[:, C_WIDTH:]
    row = lax.broadcasted_iota(jnp.int32, u.shape, 0)
    before = jnp.where(i == 0, 0.0, prev_ref[7:8, C_WIDTH:])
    after = jnp.where(i == last, 0.0, next_ref[0:1, C_WIDTH:])
    u_prev = jnp.where(row == 0, before, pltpu.roll(u, 1, 0))
    u_next = jnp.where(row == tc - 1, after, pltpu.roll(u, tc - 1, 0))
    w = w_ref[...]
    o_ref[...] = (gate * (w[0:1] * u_prev + w[1:2] * u + w[2:3] * u_next)).astype(BF16)


def _short_conv(cp, conv_w, batch, seq):
    tc = CONV_TILE
    sub = 8
    cp3 = cp.reshape(batch, seq, 2 * C_WIDTH)
    nsub = seq // sub
    out = pl.pallas_call(
        _conv_kernel,
        grid=(batch, seq // tc),
        in_specs=[pl.BlockSpec((None, tc, 2 * C_WIDTH), lambda b, i: (b, i, 0)),
                  pl.BlockSpec((None, sub, 2 * C_WIDTH),
                               lambda b, i: (b, jnp.maximum(i * (tc // sub) - 1, 0), 0)),
                  pl.BlockSpec((None, sub, 2 * C_WIDTH),
                               lambda b, i: (b, jnp.minimum((i + 1) * (tc // sub), nsub - 1), 0)),
                  pl.BlockSpec((3, C_WIDTH), lambda b, i: (0, 0))],
        out_specs=pl.BlockSpec((None, tc, C_WIDTH), lambda b, i: (b, i, 0)),
        out_shape=jax.ShapeDtypeStruct((batch, seq, C_WIDTH), BF16),
        compiler_params=_cparams(("arbitrary", "arbitrary")),
        name="short_conv",
    )(cp3, cp3, cp3, conv_w)
    return out.reshape(batch * seq, C_WIDTH)


def _outproj_kernel(oa_ref, ob_ref, oc_ref, x_ref, mod_ref, g_ref, wa_ref, wb_ref, wc_ref,
                    rhi_ref, rlo_ref, rb_ref, tri_ref,
                    x1_ref, h_ref, eid_ref, gate_ref, rank_ref, cnt_ref, run_ref):
    i = pl.program_id(0)
    tm = x_ref.shape[0]

    @pl.when(i == 0)
    def _():
        run_ref[...] = jnp.zeros_like(run_ref)

    mix = (jnp.dot(oa_ref[...], wa_ref[...], preferred_element_type=F32)
           + jnp.dot(ob_ref[...], wb_ref[...], preferred_element_type=F32)
           + jnp.dot(oc_ref[...], wc_ref[...], preferred_element_type=F32))
    x1 = x_ref[...] + mod_ref[2:3, :] * mix
    x1_ref[...] = x1
    h = _rms_mod(x1, g_ref[...], mod_ref[4:5, :], mod_ref[3:4, :])
    _store_token_tiles(h_ref, h)

    hi = h.astype(BF16)
    lo = (h - hi.astype(F32)).astype(BF16)
    logits = (jnp.dot(hi, rhi_ref[...], preferred_element_type=F32)
              + jnp.dot(hi, rlo_ref[...], preferred_element_type=F32)
              + jnp.dot(lo, rhi_ref[...], preferred_element_type=F32)) + rb_ref[...]
    lt = logits.T
    r8 = lax.broadcasted_iota(jnp.int32, (8, tm), 0)
    neg = -jnp.inf
    gl = jnp.where(r8 < N_GROUPS, lt[0:8], neg)
    gmax = jnp.max(gl, axis=0, keepdims=True)
    gidx = jnp.min(jnp.where(gl == gmax, r8, 8), axis=0, keepdims=True)
    p_sel = 1.0 / jnp.sum(jnp.exp(gl - gmax), axis=0, keepdims=True)
    sel = jnp.zeros((8, tm), F32)
    for g in range(N_GROUPS):
        sel = sel + jnp.where(gidx == g, lt[8 + 8 * g:16 + 8 * g], 0.0)
    v1 = jnp.max(sel, axis=0, keepdims=True)
    i1 = jnp.min(jnp.where(sel == v1, r8, 8), axis=0, keepdims=True)
    sel2 = jnp.where(r8 == i1, neg, sel)
    v2 = jnp.max(sel2, axis=0, keepdims=True)
    i2 = jnp.min(jnp.where(sel2 == v2, r8, 8), axis=0, keepdims=True)
    e = jnp.exp(v2 - v1)
    gate_ref[0:1, :] = p_sel / (1.0 + e)
    gate_ref[1:2, :] = p_sel * e / (1.0 + e)
    eid0 = gidx * EXPERTS_PER_GROUP + i1
    eid1 = gidx * EXPERTS_PER_GROUP + i2
    eid_ref[0:1, :] = eid0
    eid_ref[1:2, :] = eid1

    r32 = lax.broadcasted_iota(jnp.int32, (N_EXPERTS, tm), 0)
    oh0 = r32 == eid0
    oh1 = r32 == eid1
    cnt = jnp.where(oh0, 1.0, 0.0) + jnp.where(oh1, 1.0, 0.0)
    before = jnp.dot(cnt.astype(BF16), tri_ref[...], preferred_element_type=F32) + run_ref[:, 0:1]
    rank_ref[0:1, :] = jnp.sum(jnp.where(oh0, before, 0.0), axis=0, keepdims=True).astype(jnp.int32)
    rank_ref[1:2, :] = jnp.sum(jnp.where(oh1, before, 0.0), axis=0, keepdims=True).astype(jnp.int32)
    run = run_ref[...] + jnp.sum(cnt, axis=1, keepdims=True)
    run_ref[...] = run
    cnt_ref[...] = run


def _outproj(oa, ob, oc, x, mod_l, g2, wa, wb, wc, rhi, rlo, rb, tri, seq):
    n, d = x.shape
    tm = TOKEN_TILE
    tiles_per_seq = seq // tm
    row = lambda i: (i, 0)
    col = lambda i: (0, i)
    const = lambda i: (0, 0)
    return pl.pallas_call(
        _outproj_kernel,
        grid=(n // tm,),
        in_specs=[pl.BlockSpec((tm, A_WIDTH), row), pl.BlockSpec((tm, B_WIDTH), row),
                  pl.BlockSpec((tm, C_WIDTH), row), pl.BlockSpec((tm, d), row),
                  pl.BlockSpec((None, 6, d), lambda i: (i // tiles_per_seq, 0, 0)),
                  pl.BlockSpec((1, d), const),
                  pl.BlockSpec((A_WIDTH, d), const), pl.BlockSpec((B_WIDTH, d), const),
                  pl.BlockSpec((C_WIDTH, d), const),
                  pl.BlockSpec((d, LANES), const), pl.BlockSpec((d, LANES), const),
                  pl.BlockSpec((1, LANES), const), pl.BlockSpec((tm, tm), const)],
        out_specs=[pl.BlockSpec((tm, d), row), pl.BlockSpec((tm * SUBLANES, LANES), row),
                   pl.BlockSpec((2, tm), col), pl.BlockSpec((2, tm), col), pl.BlockSpec((2, tm), col),
                   pl.BlockSpec((N_EXPERTS, LANES), const)],
        out_shape=[jax.ShapeDtypeStruct((n, d), F32), jax.ShapeDtypeStruct((n * SUBLANES, LANES), F32),
                   jax.ShapeDtypeStruct((2, n), jnp.int32), jax.ShapeDtypeStruct((2, n), F32),
                   jax.ShapeDtypeStruct((2, n), jnp.int32),
                   jax.ShapeDtypeStruct((N_EXPERTS, LANES), F32)],
        scratch_shapes=[pltpu.VMEM((N_EXPERTS, LANES), F32)],
        compiler_params=_cparams(("arbitrary",)),
        name="outproj_router",
    )(oa, ob, oc, x, mod_l, g2, wa, wb, wc, rhi, rlo, rb, tri)


def _dispatch_kernel(dest_ref, h_ref, buf_in_ref, buf_ref, sem):
    del buf_in_ref
    td = dest_ref.shape[1]

    def body(t, carry):
        src = h_ref.at[pl.ds(pl.multiple_of(t * SUBLANES, SUBLANES), SUBLANES)]
        for k in range(2):
            dst = buf_ref.at[pl.ds(pl.multiple_of(dest_ref[k, t], SUBLANES), SUBLANES)]
            pltpu.make_async_copy(src, dst, sem).start(priority=k)
        return carry

    lax.fori_loop(0, td, body, 0, unroll=DMA_UNROLL)
    for _ in range(2):
        pltpu.make_async_copy(h_ref, buf_ref.at[pl.ds(0, td * SUBLANES)], sem).wait()


def _dispatch(dest8, h8, buf_init):
    n = dest8.shape[1]
    td = DMA_TILE
    return pl.pallas_call(
        _dispatch_kernel,
        grid=(n // td,),
        in_specs=[pl.BlockSpec((2, td), lambda i: (0, i), memory_space=pltpu.SMEM),
                  pl.BlockSpec((td * SUBLANES, LANES), lambda i: (i, 0)),
                  pl.BlockSpec(memory_space=pl.ANY)],
        out_specs=pl.BlockSpec(memory_space=pl.ANY),
        out_shape=jax.ShapeDtypeStruct(buf_init.shape, F32),
        scratch_shapes=[pltpu.SemaphoreType.DMA(())],
        input_output_aliases={2: 0},
        compiler_params=pltpu.CompilerParams(dimension_semantics=("arbitrary",), has_side_effects=True),
        name="moe_dispatch",
    )(dest8, h8, buf_init)


def _expert_kernel(be_ref, nu_ref, x_ref, w1_ref, w3_ref, w2_ref, y_ref, w1b, w3b, w2b):
    i = pl.program_id(0)
    rb = x_ref.shape[0] // SUBLANES
    changed = jnp.logical_or(i == 0, be_ref[i] != be_ref[jnp.maximum(i - 1, 0)])

    @pl.when(changed)
    def _():
        w1b[...] = w1_ref[...].astype(BF16)
        w3b[...] = w3_ref[...].astype(BF16)
        w2b[...] = w2_ref[...].astype(BF16)

    @pl.when(i < nu_ref[0])
    def _():
        x = _load_token_tiles(x_ref, rb).astype(BF16)
        de = w1b.shape[1]
        y = None
        for c in range(0, de, EXPERT_COL_CHUNK):
            cols = slice(c, c + EXPERT_COL_CHUNK)
            a = jnp.dot(x, w1b[:, cols], preferred_element_type=F32)
            b = jnp.dot(x, w3b[:, cols], preferred_element_type=F32)
            act = (a * jax.nn.sigmoid(a) * b).astype(BF16)
            part = jnp.dot(act, w2b[cols, :], preferred_element_type=F32)
            y = part if y is None else y + part
        _store_token_tiles(y_ref, y)

    @pl.when(i >= nu_ref[0])
    def _():
        y_ref[...] = jnp.zeros_like(y_ref)


def _experts(block_e, n_used, buf8, w1, w3, w2, layer):
    cap = buf8.shape[0] // SUBLANES
    d = D_MODEL
    rb = EXPERT_ROWS
    de = w1.shape[-1]
    grid_spec = pltpu.PrefetchScalarGridSpec(
        num_scalar_prefetch=2,
        grid=(cap // rb,),
        in_specs=[pl.BlockSpec((rb * SUBLANES, LANES), lambda i, be, nu: (i, 0)),
                  pl.BlockSpec((None, None, d, de), lambda i, be, nu: (layer, be[i], 0, 0)),
                  pl.BlockSpec((None, None, d, de), lambda i, be, nu: (layer, be[i], 0, 0)),
                  pl.BlockSpec((None, None, de, d), lambda i, be, nu: (layer, be[i], 0, 0))],
        out_specs=pl.BlockSpec((rb * SUBLANES, LANES), lambda i, be, nu: (i, 0)),
        scratch_shapes=[pltpu.VMEM((d, de), BF16), pltpu.VMEM((d, de), BF16), pltpu.VMEM((de, d), BF16)],
    )
    return pl.pallas_call(
        _expert_kernel,
        grid_spec=grid_spec,
        out_shape=jax.ShapeDtypeStruct((cap * SUBLANES, LANES), F32),
        compiler_params=_cparams(("arbitrary",)),
        name="moe_experts",
    )(block_e, n_used, buf8, w1, w3, w2)


def _final_kernel(x_ref, dest_ref, destn_ref, y_hbm, gt_ref, mod_ref, g_ref, o_ref, ybuf, ysem):
    x = x_ref[...] + mod_ref[5:6, :] * _gated_pair(dest_ref, destn_ref, y_hbm, ybuf, ysem, gt_ref[...])
    ms = jnp.mean(x * x, axis=-1, keepdims=True)
    o_ref[...] = x * lax.rsqrt(ms + EPS) * g_ref[...]


def _final(x1, dest8, y8, gates, mod_l, g, seq):
    n, d = x1.shape
    tm = TOKEN_TILE
    tiles_per_seq = seq // tm
    row = lambda i: (i, 0)
    gather_specs, scratch = _moe_gather_specs(tm, n // tm)
    return pl.pallas_call(
        _final_kernel,
        grid=(n // tm,),
        in_specs=[pl.BlockSpec((tm, d), row)] + gather_specs + [
            pl.BlockSpec((tm, 2), row),
            pl.BlockSpec((None, 6, d), lambda i: (i // tiles_per_seq, 0, 0)),
            pl.BlockSpec((1, d), lambda i: (0, 0))],
        out_specs=pl.BlockSpec((tm, d), row),
        out_shape=jax.ShapeDtypeStruct((n, d), F32),
        scratch_shapes=scratch,
        compiler_params=_cparams(("arbitrary",)),
        name="final_norm",
    )(x1, dest8, dest8, y8, gates, mod_l, g)


def _rope_tables(seq):
    freqs = ROPE_THETA ** (-np.arange(0, A_QK_DIM, 2, dtype=np.float32) / A_QK_DIM)
    lane = np.arange(LANES)
    f = freqs[lane % 16][None, :]
    sign = np.where((lane % 32) < 16, -1.0, 1.0)[None, :].astype(np.float32)
    pos = np.arange(seq, dtype=np.float32)[:, None]
    ang_a = pos * f
    axial = np.where((lane % HEAD_DIM) < 32, np.floor(pos / GRID_W), np.mod(pos, GRID_W)).astype(np.float32)
    ang_b = axial * f
    tabs = (np.cos(ang_a), np.sin(ang_a) * sign, np.cos(ang_b), np.sin(ang_b) * sign)
    return tuple(jnp.asarray(t, dtype=F32) for t in tabs)


def _qb_perm():
    cols = []
    for j in range(B_HEADS // B_KV_HEADS):
        cols += list(range(j * HEAD_DIM, (j + 1) * HEAD_DIM))
        cols += list(range((4 + j) * HEAD_DIM, (5 + j) * HEAD_DIM))
    return np.asarray(cols, dtype=np.int32)


def _trunk(x, mod, batch, seq, p):
    n = batch * seq
    d = D_MODEL
    depth = p["w_in"].shape[0]
    tabs = _rope_tables(seq)
    perm = _qb_perm()
    tri = jnp.asarray(np.triu(np.ones((TOKEN_TILE, TOKEN_TILE), np.float32), 1), dtype=BF16)
    rb = EXPERT_ROWS
    cap = (2 * n + N_EXPERTS * (rb - 1) + rb - 1) // rb * rb
    nb = cap // rb
    x = x.reshape(n, d)
    moe = None
    buf8 = None
    for l in range(depth):
        lambda_init = 0.8 - 0.6 * math.exp(-0.3 * l)
        mod_l = mod[l].reshape(batch, 6, d)
        w_in = p["w_in"][l]
        qcols = 3 * A_WIDTH + perm
        w_in = jnp.concatenate([w_in[:, :3 * A_WIDTH], w_in[:, qcols], w_in[:, 3 * A_WIDTH + B_WIDTH:]],
                               axis=1).astype(BF16)
        tile2 = lambda v: jnp.tile(v.reshape(1, HEAD_DIM), (1, 2))
        x, qa, ka, va, qb, kb, vb, cp = _inproj(x, moe, mod_l, p["norm1"][l].reshape(1, d), w_in,
                                                tile2(p["q_norm"][l]), tile2(p["k_norm"][l]), tabs, seq)
        oa = _diff_attn(qa, ka, va, p["diff_lambda"][l], tile2(p["diff_subln"][l]), lambda_init, batch, seq)
        ob = _gqa_attn(qb, kb, vb, batch, seq)
        oc = _short_conv(cp, p["conv_w"][l], batch, seq)
        w_out = p["w_out"][l]
        wa = w_out[:A_WIDTH].astype(BF16)
        wb = w_out[A_WIDTH + perm].astype(BF16)
        wc = w_out[A_WIDTH + B_WIDTH:].astype(BF16)
        wr = jnp.zeros((d, LANES), F32)
        wr = wr.at[:, :N_GROUPS].set(p["router_group_w"][l]).at[:, 8:8 + N_EXPERTS].set(p["router_expert_w"][l])
        rhi = wr.astype(BF16)
        rlo = (wr - rhi.astype(F32)).astype(BF16)
        rbias = jnp.zeros((1, LANES), F32)
        rbias = rbias.at[0, :N_GROUPS].set(p["router_group_b"][l]).at[0, 8:8 + N_EXPERTS].set(p["router_expert_b"][l])
        x1, h8, eid, gate, rank, cnt = _outproj(oa, ob, oc, x, mod_l, p["norm2"][l].reshape(1, d),
                                                wa, wb, wc, rhi, rlo, rbias, tri, seq)
        counts = cnt[:, 0].astype(jnp.int32)
        pcounts = (counts + rb - 1) // rb * rb
        pends = jnp.cumsum(pcounts)
        pstarts = pends - pcounts
        block_row = jnp.arange(nb, dtype=jnp.int32)[:, None] * rb
        block_e = jnp.minimum(jnp.sum((pends[None, :] <= block_row).astype(jnp.int32), axis=1), N_EXPERTS - 1)
        n_used = (pends[-1:] // rb).astype(jnp.int32)
        expert_ids = jnp.arange(N_EXPERTS, dtype=jnp.int32)[:, None, None]
        dest8 = (jnp.sum(jnp.where(eid[None] == expert_ids, pstarts[:, None, None], 0), axis=0) + rank) * SUBLANES
        if buf8 is None:
            buf8 = jnp.zeros((cap * SUBLANES, LANES), F32)
        buf8 = _dispatch(dest8, h8, buf8)
        y8 = _experts(block_e, n_used, buf8, p["expert_w1"], p["expert_w3"], p["expert_w2"], l)
        x = x1
        moe = (dest8, y8, gate.T, mod_l)
    out = _final(x, *moe, p["final_norm"].reshape(1, d), seq)
    return out.reshape(batch, seq, d)


def kernel(x_prompt, x_sample, c_prompt, c_sample, w_ada, b_ada, norm1, norm2, w_in, w_out, diff_lambda,
           diff_subln, q_norm, k_norm, conv_w, router_group_w, router_group_b, router_expert_w,
           router_expert_b, expert_w1, expert_w3, expert_w2, final_norm):
    p = dict(norm1=norm1, norm2=norm2, w_in=w_in, w_out=w_out, diff_lambda=diff_lambda, diff_subln=diff_subln,
             q_norm=q_norm, k_norm=k_norm, conv_w=conv_w, router_group_w=router_group_w,
             router_group_b=router_group_b, router_expert_w=router_expert_w, router_expert_b=router_expert_b,
             expert_w1=expert_w1, expert_w3=expert_w3, expert_w2=expert_w2, final_norm=final_norm)
    bp, sp, _ = x_prompt.shape
    bs, ss, _ = x_sample.shape
    mod = _ada_mod(jnp.concatenate([c_prompt, c_sample], axis=0), w_ada, b_ada)
    y_prompt = _trunk(x_prompt, mod[:, :bp], bp, sp, p)
    y_sample = _trunk(x_sample, mod[:, bp:], bs, ss, p)
    return (y_prompt, y_sample)
```

```python
import functools
import math

import numpy as np
import jax
import jax.numpy as jnp
from jax import lax
from jax.experimental import pallas as pl
from jax.experimental.pallas import tpu as pltpu

F32 = jnp.float32
BF16 = jnp.bfloat16

D_MODEL = 1024
HEAD_DIM = 64
A_HEADS = 4
A_QK_DIM = 32
A_WIDTH = 256
B_HEADS = 8
B_KV_HEADS = 2
B_WIDTH = 512
B_KV_WIDTH = 128
C_WIDTH = 256
IN_COLS = 2304
N_GROUPS = 4
EXPERTS_PER_GROUP = 8
N_EXPERTS = 32
D_EXPERT = 512
GRID_W = 64
ROPE_THETA = 10000.0
EPS = 1e-6
LANES = 128
LOG2E = 1.4426950408889634

VMEM_LIMIT = 56 * 1024 * 1024

TOKEN_TILE = 512
PROJ_SUBTILE = 256
ROUTER_SUBTILE = 512
Q_TILE = 256
SCORE_ELEMS = 2 * 1024 * 1024
CONV_TILE = 512
EXPERT_ROWS = 512
EXPERT_COL_CHUNK = 256
DMA_TILE = 1024
DMA_UNROLL = 8


def _cparams(sem):
    return pltpu.CompilerParams(dimension_semantics=sem, vmem_limit_bytes=VMEM_LIMIT)


SUBLANES = 8
LANE_BLOCKS = D_MODEL // LANES
assert LANE_BLOCKS == SUBLANES


def _store_token_tiles(ref, val, first=0, per_token=SUBLANES):
    rows = val.shape[0]
    for c in range(LANE_BLOCKS):
        ref[pl.ds(first + c, rows, stride=per_token), :] = val[:, c * LANES:(c + 1) * LANES]


def _load_token_tiles(ref, rows, first=0, per_token=SUBLANES):
    return jnp.concatenate([ref[pl.ds(first + c, rows, stride=per_token), :] for c in range(LANE_BLOCKS)], axis=1)


def _ada_kernel(c_ref, w_ref, b_ref, o_ref):
    c = c_ref[...]
    s = c * jax.nn.sigmoid(c)
    o_ref[...] = jnp.dot(s, w_ref[...], precision=lax.Precision.HIGHEST,
                         preferred_element_type=F32) + b_ref[...]


def _ada_mod(c_all, w_ada, b_ada):
    depth, d, six_d = w_ada.shape
    bt = c_all.shape[0]
    nt = six_d // d
    return pl.pallas_call(
        _ada_kernel,
        grid=(depth, nt),
        in_specs=[pl.BlockSpec((bt, d), lambda l, j: (0, 0)),
                  pl.BlockSpec((None, d, d), lambda l, j: (l, 0, j)),
                  pl.BlockSpec((None, 1, d), lambda l, j: (l, 0, j))],
        out_specs=pl.BlockSpec((None, bt, d), lambda l, j: (l, 0, j)),
        out_shape=jax.ShapeDtypeStruct((depth, bt, six_d), F32),
        compiler_params=_cparams(("arbitrary", "arbitrary")),
        name="ada_mod",
    )(c_all, w_ada, b_ada.reshape(depth, 1, six_d))


def _rope128(x, cos, sin_signed):
    return x * cos + pltpu.roll(x, LANES // 2, 1) * sin_signed


def _qk_first_head(shape):
    return (lax.broadcasted_iota(jnp.int32, shape, 1) % (LANES // 2)) < HEAD_DIM // 2


def _head_rms(x, gain, lo):
    sq = x * x
    s_lo = jnp.sum(jnp.where(lo, sq, 0.0), axis=-1, keepdims=True)
    s_hi = jnp.sum(jnp.where(lo, 0.0, sq), axis=-1, keepdims=True)
    ms = jnp.where(lo, s_lo, s_hi) * (1.0 / HEAD_DIM)
    return x * lax.rsqrt(ms + EPS) * gain


def _start_expert_row_gather(dest_ref, y_hbm, ybuf, sem, slot):
    def body(t, carry):
        for k in range(2):
            src = y_hbm.at[pl.ds(pl.multiple_of(dest_ref[k, t], SUBLANES), SUBLANES)]
            dst = ybuf.at[slot, pl.ds(pl.multiple_of((2 * t + k) * SUBLANES, SUBLANES), SUBLANES)]
            pltpu.make_async_copy(src, dst, sem.at[slot]).start(priority=k)
        return carry

    lax.fori_loop(0, dest_ref.shape[1], body, 0, unroll=DMA_UNROLL)


def _gated_pair(dest_ref, dest_next_ref, y_hbm, ybuf, sem, gt):
    i = pl.program_id(0)
    rows = gt.shape[0]
    slot = lax.rem(i, 2)

    @pl.when(i == 0)
    def _():
        _start_expert_row_gather(dest_ref, y_hbm, ybuf, sem, 0)

    @pl.when(i + 1 < pl.num_programs(0))
    def _():
        _start_expert_row_gather(dest_next_ref, y_hbm, ybuf, sem, 1 - slot)

    pltpu.make_async_copy(y_hbm.at[pl.ds(0, 2 * rows * SUBLANES)], ybuf.at[slot], sem.at[slot]).wait()
    y_ref = ybuf.at[slot]
    y0 = _load_token_tiles(y_ref, rows, 0, 2 * SUBLANES)
    y1 = _load_token_tiles(y_ref, rows, SUBLANES, 2 * SUBLANES)
    return gt[:, 0:1] * y0 + gt[:, 1:2] * y1


def _moe_gather_specs(tm, n_tiles):
    in_specs = [pl.BlockSpec((2, tm), lambda i: (0, i), memory_space=pltpu.SMEM),
                pl.BlockSpec((2, tm), lambda i: (0, jnp.minimum(i + 1, n_tiles - 1)), memory_space=pltpu.SMEM),
                pl.BlockSpec(memory_space=pl.ANY)]
    scratch = [pltpu.VMEM((2, 2 * tm * SUBLANES, LANES), F32), pltpu.SemaphoreType.DMA((2,))]
    return in_specs, scratch


def _rms_mod(x, gain, scale, shift):
    ms = jnp.mean(x * x, axis=-1, keepdims=True)
    return (x * lax.rsqrt(ms + EPS) * gain) * (1.0 + scale) + shift


def _inproj_kernel(with_moe, *refs):
    if with_moe:
        (x_ref, dest_ref, destn_ref, y_hbm, gt_ref, modp_ref, mod_ref, g_ref, w_ref, qn_ref, kn_ref,
         ca_ref, sa_ref, cb_ref, sb_ref,
         xo_ref, qa_ref, ka_ref, va_ref, qb_ref, kb_ref, vb_ref, cp_ref, ybuf, ysem) = refs
    else:
        (x_ref, mod_ref, g_ref, w_ref, qn_ref, kn_ref, ca_ref, sa_ref, cb_ref, sb_ref,
         qa_ref, ka_ref, va_ref, qb_ref, kb_ref, vb_ref, cp_ref) = refs
    tm = x_ref.shape[0]
    if with_moe:
        moe_out = _gated_pair(dest_ref, destn_ref, y_hbm, ybuf, ysem, gt_ref[...])
    qa_scale = (A_QK_DIM ** -0.5) * LOG2E
    qb_scale = (HEAD_DIM ** -0.5) * LOG2E
    sub = min(tm, PROJ_SUBTILE)
    first = _qk_first_head((sub, LANES))
    for r in range(tm // sub):
        rows = slice(r * sub, (r + 1) * sub)
        x = x_ref[rows, :]
        if with_moe:
            x = x + modp_ref[5:6, :] * moe_out[rows, :]
            xo_ref[rows, :] = x
        hb = _rms_mod(x, g_ref[...], mod_ref[1:2, :], mod_ref[0:1, :]).astype(BF16)
        proj = jnp.dot(hb, w_ref[...], preferred_element_type=F32)
        ca, sa, cb, sb = ca_ref[rows, :], sa_ref[rows, :], cb_ref[rows, :], sb_ref[rows, :]
        for i in range(A_WIDTH // LANES):
            sl = slice(i * LANES, (i + 1) * LANES)
            qa_ref[rows, sl] = (_rope128(proj[:, sl], ca, sa) * qa_scale).astype(BF16)
            ka_ref[rows, sl] = _rope128(proj[:, A_WIDTH + i * LANES:A_WIDTH + (i + 1) * LANES], ca, sa).astype(BF16)
        va_ref[rows, :] = proj[:, 2 * A_WIDTH:3 * A_WIDTH].astype(BF16)
        o = 3 * A_WIDTH
        for j in range(B_WIDTH // LANES):
            q = _head_rms(proj[:, o + j * LANES:o + (j + 1) * LANES], qn_ref[...], first)
            qb_ref[rows, j * LANES:(j + 1) * LANES] = (_rope128(q, cb, sb) * qb_scale).astype(BF16)
        o += B_WIDTH
        k = _head_rms(proj[:, o:o + B_KV_WIDTH], kn_ref[...], first)
        kb_ref[rows, :] = _rope128(k, cb, sb).astype(BF16)
        o += B_KV_WIDTH
        vb_ref[rows, :] = proj[:, o:o + B_KV_WIDTH].astype(BF16)
        o += B_KV_WIDTH
        cp_ref[rows, :C_WIDTH] = proj[:, o:o + C_WIDTH]
        cp_ref[rows, C_WIDTH:] = proj[:, o + C_WIDTH:o + 2 * C_WIDTH] * proj[:, o + 2 * C_WIDTH:o + 3 * C_WIDTH]


def _inproj(x, moe, mod_l, g1, w, qn, kn, tabs, seq):
    n, d = x.shape
    tm = TOKEN_TILE
    tiles_per_seq = seq // tm
    row = lambda i: (i, 0)
    const = lambda i: (0, 0)
    pos = lambda i: (i % tiles_per_seq, 0)
    in_specs = [pl.BlockSpec((tm, d), row)]
    args = [x]
    scratch = []
    if moe is not None:
        dest8, y8, gates, mod_prev = moe
        gather_specs, scratch = _moe_gather_specs(tm, n // tm)
        in_specs += gather_specs + [pl.BlockSpec((tm, 2), row),
                                    pl.BlockSpec((None, 6, d), lambda i: (i // tiles_per_seq, 0, 0))]
        args += [dest8, dest8, y8, gates, mod_prev]
    in_specs += [pl.BlockSpec((None, 6, d), lambda i: (i // tiles_per_seq, 0, 0)),
                 pl.BlockSpec((1, d), const),
                 pl.BlockSpec((d, IN_COLS), const),
                 pl.BlockSpec((1, LANES), const), pl.BlockSpec((1, LANES), const)]
    in_specs += [pl.BlockSpec((tm, LANES), pos)] * 4
    args += [mod_l, g1, w, qn, kn, *tabs]
    widths = [(A_WIDTH, BF16), (A_WIDTH, BF16), (A_WIDTH, BF16), (B_WIDTH, BF16), (B_KV_WIDTH, BF16),
              (B_KV_WIDTH, BF16), (2 * C_WIDTH, F32)]
    out_specs = [pl.BlockSpec((tm, wd), row) for wd, _ in widths]
    out_shape = [jax.ShapeDtypeStruct((n, wd), dt) for wd, dt in widths]
    if moe is not None:
        out_specs = [pl.BlockSpec((tm, d), row)] + out_specs
        out_shape = [jax.ShapeDtypeStruct((n, d), F32)] + out_shape
    outs = pl.pallas_call(
        functools.partial(_inproj_kernel, moe is not None),
        grid=(n // tm,),
        in_specs=in_specs, out_specs=out_specs, out_shape=out_shape, scratch_shapes=scratch,
        compiler_params=_cparams(("arbitrary",)),
        name="inproj",
    )(*args)
    if moe is None:
        outs = [x] + list(outs)
    return outs


def _flash(qs, k_ref, v_ref, col, seq):
    rows = qs.shape[0]
    tk = min(SCORE_ELEMS // rows, seq)
    sl = slice(col * LANES, (col + 1) * LANES)

    def body(c, carry):
        m, l, acc = carry
        r0 = pl.multiple_of(c * tk, tk)
        k = k_ref[pl.ds(r0, tk), sl]
        v = v_ref[pl.ds(r0, tk), sl]
        s = lax.dot_general(qs, k, (((1,), (1,)), ((), ())), preferred_element_type=F32)
        m_new = jnp.maximum(m, jnp.max(s, axis=-1, keepdims=True))
        alpha = jnp.exp2(m - m_new)
        p = jnp.exp2(s - m_new)
        l = alpha * l + jnp.sum(p, axis=-1, keepdims=True)
        acc = alpha * acc + jnp.dot(p.astype(BF16), v, preferred_element_type=F32)
        return m_new, l, acc

    init = (jnp.full((rows, 1), -jnp.inf, F32), jnp.zeros((rows, 1), F32), jnp.zeros((rows, LANES), F32))
    _, l, acc = lax.fori_loop(0, seq // tk, body, init, unroll=True)
    return acc, l


def _diff_attn_kernel(lambda_init, seq, q_ref, k_ref, v_ref, lam_ref, g_ref, o_ref):
    lv = lam_ref[...]
    lam = (jnp.exp(jnp.sum(lv[0:1] * lv[1:2], keepdims=True))
           - jnp.exp(jnp.sum(lv[2:3] * lv[3:4], keepdims=True)) + lambda_init)
    tq = q_ref.shape[0]
    lane = lax.broadcasted_iota(jnp.int32, (tq, LANES), 1)
    lo = lane < HEAD_DIM
    for i in range(A_WIDTH // LANES):
        sl = slice(i * LANES, (i + 1) * LANES)
        q = q_ref[:, sl]
        zero = jnp.zeros_like(q)
        slot = (lane % (LANES // 2)) // (A_QK_DIM // 2)
        qs = jnp.concatenate([jnp.where(slot == j, q, zero) for j in range(4)], axis=0)
        acc, l = _flash(qs, k_ref, v_ref, i, seq)
        o = acc / l
        o_even = o[0:tq] - lam * o[tq:2 * tq]
        o_odd = o[2 * tq:3 * tq] - lam * o[3 * tq:4 * tq]
        oo = jnp.where(lo, o_even, o_odd)
        o_ref[:, sl] = (_head_rms(oo, g_ref[...], lo) * (1.0 - lambda_init)).astype(BF16)


def _diff_attn(qa, ka, va, lam_vecs, subln_tiled, lambda_init, batch, seq):
    tq = Q_TILE
    q3, k3, v3 = (a.reshape(batch, seq, A_WIDTH) for a in (qa, ka, va))
    out = pl.pallas_call(
        functools.partial(_diff_attn_kernel, lambda_init, seq),
        grid=(batch, seq // tq),
        in_specs=[pl.BlockSpec((None, tq, A_WIDTH), lambda b, i: (b, i, 0)),
                  pl.BlockSpec((None, seq, A_WIDTH), lambda b, i: (b, 0, 0)),
                  pl.BlockSpec((None, seq, A_WIDTH), lambda b, i: (b, 0, 0)),
                  pl.BlockSpec((4, A_QK_DIM), lambda b, i: (0, 0)),
                  pl.BlockSpec((1, LANES), lambda b, i: (0, 0))],
        out_specs=pl.BlockSpec((None, tq, A_WIDTH), lambda b, i: (b, i, 0)),
        out_shape=jax.ShapeDtypeStruct((batch, seq, A_WIDTH), BF16),
        compiler_params=_cparams(("arbitrary", "arbitrary")),
        name="diff_attn",
    )(q3, k3, v3, lam_vecs, subln_tiled)
    return out.reshape(batch * seq, A_WIDTH)


def _gqa_kernel(seq, q_ref, k_ref, v_ref, o_ref):
    tq = q_ref.shape[0]
    lane = lax.broadcasted_iota(jnp.int32, (tq, LANES), 1)
    lo = lane < HEAD_DIM
    for j in range(B_WIDTH // LANES):
        q = q_ref[:, j * LANES:(j + 1) * LANES]
        zero = jnp.zeros_like(q)
        first = _qk_first_head(q.shape)
        qs = jnp.concatenate([jnp.where(first, q, zero), jnp.where(first, zero, q)], axis=0)
        acc, l = _flash(qs, k_ref, v_ref, 0, seq)
        o = acc / l
        o_ref[:, j * LANES:(j + 1) * LANES] = jnp.where(lo, o[0:tq], o[tq:2 * tq]).astype(BF16)


def _gqa_attn(qb, kb, vb, batch, seq):
    tq = 2 * Q_TILE
    q3 = qb.reshape(batch, seq, B_WIDTH)
    k3 = kb.reshape(batch, seq, B_KV_WIDTH)
    v3 = vb.reshape(batch, seq, B_KV_WIDTH)
    out = pl.pallas_call(
        functools.partial(_gqa_kernel, seq),
        grid=(batch, seq // tq),
        in_specs=[pl.BlockSpec((None, tq, B_WIDTH), lambda b, i: (b, i, 0)),
                  pl.BlockSpec((None, seq, B_KV_WIDTH), lambda b, i: (b, 0, 0)),
                  pl.BlockSpec((None, seq, B_KV_WIDTH), lambda b, i: (b, 0, 0))],
        out_specs=pl.BlockSpec((None, tq, B_WIDTH), lambda b, i: (b, i, 0)),
        out_shape=jax.ShapeDtypeStruct((batch, seq, B_WIDTH), BF16),
        compiler_params=_cparams(("arbitrary", "arbitrary")),
        name="gqa_attn",
    )(q3, k3, v3)
    return out.reshape(batch * seq, B_WIDTH)


def _conv_kernel(cp_ref, prev_ref, next_ref, w_ref, o_ref):
    i = pl.program_id(1)
    last = pl.num_programs(1) - 1
    tc = cp_ref.shape[0]
    gate = cp_ref[:, :C_WIDTH]
    u = cp_ref[:, C_WIDTH:]
    row = lax.broadcasted_iota(jnp.int32, u.shape, 0)
    before = jnp.where(i == 0, 0.0, prev_ref[7:8, C_WIDTH:])
    after = jnp.where(i == last, 0.0, next_ref[0:1, C_WIDTH:])
    u_prev = jnp.where(row == 0, before, pltpu.roll(u, 1, 0))
    u_next = jnp.where(row == tc - 1, after, pltpu.roll(u, tc - 1, 0))
    w = w_ref[...]
    o_ref[...] = (gate * (w[0:1] * u_prev + w[1:2] * u + w[2:3] * u_next)).astype(BF16)


def _short_conv(cp, conv_w, batch, seq):
    tc = CONV_TILE
    sub = 8
    cp3 = cp.reshape(batch, seq, 2 * C_WIDTH)
    nsub = seq // sub
    out = pl.pallas_call(
        _conv_kernel,
        grid=(batch, seq // tc),
        in_specs=[pl.BlockSpec((None, tc, 2 * C_WIDTH), lambda b, i: (b, i, 0)),
                  pl.BlockSpec((None, sub, 2 * C_WIDTH),
                               lambda b, i: (b, jnp.maximum(i * (tc // sub) - 1, 0), 0)),
                  pl.BlockSpec((None, sub, 2 * C_WIDTH),
                               lambda b, i: (b, jnp.minimum((i + 1) * (tc // sub), nsub - 1), 0)),
                  pl.BlockSpec((3, C_WIDTH), lambda b, i: (0, 0))],
        out_specs=pl.BlockSpec((None, tc, C_WIDTH), lambda b, i: (b, i, 0)),
        out_shape=jax.ShapeDtypeStruct((batch, seq, C_WIDTH), BF16),
        compiler_params=_cparams(("arbitrary", "arbitrary")),
        name="short_conv",
    )(cp3, cp3, cp3, conv_w)
    return out.reshape(batch * seq, C_WIDTH)


def _outproj_kernel(oa_ref, ob_ref, oc_ref, x_ref, mod_ref, g_ref, wa_ref, wb_ref, wc_ref,
                    rhi_ref, rlo_ref, rb_ref, tri_ref,
                    x1_ref, h_ref, eid_ref, gate_ref, rank_ref, cnt_ref, run_ref):
    i = pl.program_id(0)
    tm = x_ref.shape[0]

    @pl.when(i == 0)
    def _():
        run_ref[...] = jnp.zeros_like(run_ref)

    sub = tri_ref.shape[0]
    run = run_ref[...]
    for r in range(tm // sub):
        rows = slice(r * sub, (r + 1) * sub)
        mix = (jnp.dot(oa_ref[rows, :], wa_ref[...], preferred_element_type=F32)
               + jnp.dot(ob_ref[rows, :], wb_ref[...], preferred_element_type=F32)
               + jnp.dot(oc_ref[rows, :], wc_ref[...], preferred_element_type=F32))
        x1 = x_ref[rows, :] + mod_ref[2:3, :] * mix
        x1_ref[rows, :] = x1
        h = _rms_mod(x1, g_ref[...], mod_ref[4:5, :], mod_ref[3:4, :])
        _store_token_tiles(h_ref, h, first=r * sub * SUBLANES)

        hi = h.astype(BF16)
        lo = (h - hi.astype(F32)).astype(BF16)
        logits = (jnp.dot(hi, rhi_ref[...], preferred_element_type=F32)
                  + jnp.dot(hi, rlo_ref[...], preferred_element_type=F32)
                  + jnp.dot(lo, rhi_ref[...], preferred_element_type=F32)) + rb_ref[...]
        lt = logits.T
        r8 = lax.broadcasted_iota(jnp.int32, (8, sub), 0)
        neg = -jnp.inf
        gl = jnp.where(r8 < N_GROUPS, lt[0:8], neg)
        gmax = jnp.max(gl, axis=0, keepdims=True)
        gidx = jnp.min(jnp.where(gl == gmax, r8, 8), axis=0, keepdims=True)
        p_sel = 1.0 / jnp.sum(jnp.exp(gl - gmax), axis=0, keepdims=True)
        sel = jnp.zeros((8, sub), F32)
        for g in range(N_GROUPS):
            sel = sel + jnp.where(gidx == g, lt[8 + 8 * g:16 + 8 * g], 0.0)
        v1 = jnp.max(sel, axis=0, keepdims=True)
        i1 = jnp.min(jnp.where(sel == v1, r8, 8), axis=0, keepdims=True)
        sel2 = jnp.where(r8 == i1, neg, sel)
        v2 = jnp.max(sel2, axis=0, keepdims=True)
        i2 = jnp.min(jnp.where(sel2 == v2, r8, 8), axis=0, keepdims=True)
        e = jnp.exp(v2 - v1)
        gate_ref[0:1, rows] = p_sel / (1.0 + e)
        gate_ref[1:2, rows] = p_sel * e / (1.0 + e)
        eid0 = gidx * EXPERTS_PER_GROUP + i1
        eid1 = gidx * EXPERTS_PER_GROUP + i2
        eid_ref[0:1, rows] = eid0
        eid_ref[1:2, rows] = eid1

        r32 = lax.broadcasted_iota(jnp.int32, (N_EXPERTS, sub), 0)
        oh0 = r32 == eid0
        oh1 = r32 == eid1
        cnt = jnp.where(oh0, 1.0, 0.0) + jnp.where(oh1, 1.0, 0.0)
        before = jnp.dot(cnt.astype(BF16), tri_ref[...], preferred_element_type=F32) + run[:, 0:1]
        rank_ref[0:1, rows] = jnp.sum(jnp.where(oh0, before, 0.0), axis=0, keepdims=True).astype(jnp.int32)
        rank_ref[1:2, rows] = jnp.sum(jnp.where(oh1, before, 0.0), axis=0, keepdims=True).astype(jnp.int32)
        run = run + jnp.sum(cnt, axis=1, keepdims=True)
    run_ref[...] = run
    cnt_ref[...] = run


def _outproj(oa, ob, oc, x, mod_l, g2, wa, wb, wc, rhi, rlo, rb, tri, seq):
    n, d = x.shape
    tm = TOKEN_TILE
    tiles_per_seq = seq // tm
    row = lambda i: (i, 0)
    col = lambda i: (0, i)
    const = lambda i: (0, 0)
    return pl.pallas_call(
        _outproj_kernel,
        grid=(n // tm,),
        in_specs=[pl.BlockSpec((tm, A_WIDTH), row), pl.BlockSpec((tm, B_WIDTH), row),
                  pl.BlockSpec((tm, C_WIDTH), row), pl.BlockSpec((tm, d), row),
                  pl.BlockSpec((None, 6, d), lambda i: (i // tiles_per_seq, 0, 0)),
                  pl.BlockSpec((1, d), const),
                  pl.BlockSpec((A_WIDTH, d), const), pl.BlockSpec((B_WIDTH, d), const),
                  pl.BlockSpec((C_WIDTH, d), const),
                  pl.BlockSpec((d, LANES), const), pl.BlockSpec((d, LANES), const),
                  pl.BlockSpec((1, LANES), const), pl.BlockSpec(tri.shape, const)],
        out_specs=[pl.BlockSpec((tm, d), row), pl.BlockSpec((tm * SUBLANES, LANES), row),
                   pl.BlockSpec((2, tm), col), pl.BlockSpec((2, tm), col), pl.BlockSpec((2, tm), col),
                   pl.BlockSpec((N_EXPERTS, LANES), const)],
        out_shape=[jax.ShapeDtypeStruct((n, d), F32), jax.ShapeDtypeStruct((n * SUBLANES, LANES), F32),
                   jax.ShapeDtypeStruct((2, n), jnp.int32), jax.ShapeDtypeStruct((2, n), F32),
                   jax.ShapeDtypeStruct((2, n), jnp.int32),
                   jax.ShapeDtypeStruct((N_EXPERTS, LANES), F32)],
        scratch_shapes=[pltpu.VMEM((N_EXPERTS, LANES), F32)],
        compiler_params=_cparams(("arbitrary",)),
        name="outproj_router",
    )(oa, ob, oc, x, mod_l, g2, wa, wb, wc, rhi, rlo, rb, tri)


def _dispatch_kernel(dest_ref, h_ref, buf_in_ref, buf_ref, sem):
    del buf_in_ref
    td = dest_ref.shape[1]

    def body(t, carry):
        src = h_ref.at[pl.ds(pl.multiple_of(t * SUBLANES, SUBLANES), SUBLANES)]
        for k in range(2):
            dst = buf_ref.at[pl.ds(pl.multiple_of(dest_ref[k, t], SUBLANES), SUBLANES)]
            pltpu.make_async_copy(src, dst, sem).start(priority=k)
        return carry

    lax.fori_loop(0, td, body, 0, unroll=DMA_UNROLL)
    for _ in range(2):
        pltpu.make_async_copy(h_ref, buf_ref.at[pl.ds(0, td * SUBLANES)], sem).wait()


def _dispatch(dest8, h8, buf_init):
    n = dest8.shape[1]
    td = DMA_TILE
    return pl.pallas_call(
        _dispatch_kernel,
        grid=(n // td,),
        in_specs=[pl.BlockSpec((2, td), lambda i: (0, i), memory_space=pltpu.SMEM),
                  pl.BlockSpec((td * SUBLANES, LANES), lambda i: (i, 0)),
                  pl.BlockSpec(memory_space=pl.ANY)],
        out_specs=pl.BlockSpec(memory_space=pl.ANY),
        out_shape=jax.ShapeDtypeStruct(buf_init.shape, F32),
        scratch_shapes=[pltpu.SemaphoreType.DMA(())],
        input_output_aliases={2: 0},
        compiler_params=pltpu.CompilerParams(dimension_semantics=("arbitrary",), has_side_effects=True),
        name="moe_dispatch",
    )(dest8, h8, buf_init)


def _expert_kernel(be_ref, nu_ref, x_ref, w1_ref, w3_ref, w2_ref, y_ref, w1b, w3b, w2b):
    i = pl.program_id(0)
    rb = x_ref.shape[0] // SUBLANES
    changed = jnp.logical_or(i == 0, be_ref[i] != be_ref[jnp.maximum(i - 1, 0)])

    @pl.when(changed)
    def _():
        w1b[...] = w1_ref[...].astype(BF16)
        w3b[...] = w3_ref[...].astype(BF16)
        w2b[...] = w2_ref[...].astype(BF16)

    @pl.when(i < nu_ref[0])
    def _():
        x = _load_token_tiles(x_ref, rb).astype(BF16)
        de = w1b.shape[1]
        y = None
        for c in range(0, de, EXPERT_COL_CHUNK):
            cols = slice(c, c + EXPERT_COL_CHUNK)
            a = jnp.dot(x, w1b[:, cols], preferred_element_type=F32)
            b = jnp.dot(x, w3b[:, cols], preferred_element_type=F32)
            act = (a * jax.nn.sigmoid(a) * b).astype(BF16)
            part = jnp.dot(act, w2b[cols, :], preferred_element_type=F32)
            y = part if y is None else y + part
        _store_token_tiles(y_ref, y)

    @pl.when(i >= nu_ref[0])
    def _():
        y_ref[...] = jnp.zeros_like(y_ref)


def _experts(block_e, n_used, buf8, w1, w3, w2, layer):
    cap = buf8.shape[0] // SUBLANES
    d = D_MODEL
    rb = EXPERT_ROWS
    de = w1.shape[-1]
    grid_spec = pltpu.PrefetchScalarGridSpec(
        num_scalar_prefetch=2,
        grid=(cap // rb,),
        in_specs=[pl.BlockSpec((rb * SUBLANES, LANES), lambda i, be, nu: (i, 0)),
                  pl.BlockSpec((None, None, d, de), lambda i, be, nu: (layer, be[i], 0, 0)),
                  pl.BlockSpec((None, None, d, de), lambda i, be, nu: (layer, be[i], 0, 0)),
                  pl.BlockSpec((None, None, de, d), lambda i, be, nu: (layer, be[i], 0, 0))],
        out_specs=pl.BlockSpec((rb * SUBLANES, LANES), lambda i, be, nu: (i, 0)),
        scratch_shapes=[pltpu.VMEM((d, de), BF16), pltpu.VMEM((d, de), BF16), pltpu.VMEM((de, d), BF16)],
    )
    return pl.pallas_call(
        _expert_kernel,
        grid_spec=grid_spec,
        out_shape=jax.ShapeDtypeStruct((cap * SUBLANES, LANES), F32),
        compiler_params=_cparams(("arbitrary",)),
        name="moe_experts",
    )(block_e, n_used, buf8, w1, w3, w2)


def _final_kernel(x_ref, dest_ref, destn_ref, y_hbm, gt_ref, mod_ref, g_ref, o_ref, ybuf, ysem):
    x = x_ref[...] + mod_ref[5:6, :] * _gated_pair(dest_ref, destn_ref, y_hbm, ybuf, ysem, gt_ref[...])
    ms = jnp.mean(x * x, axis=-1, keepdims=True)
    o_ref[...] = x * lax.rsqrt(ms + EPS) * g_ref[...]


def _final(x1, dest8, y8, gates, mod_l, g, seq):
    n, d = x1.shape
    tm = TOKEN_TILE
    tiles_per_seq = seq // tm
    row = lambda i: (i, 0)
    gather_specs, scratch = _moe_gather_specs(tm, n // tm)
    return pl.pallas_call(
        _final_kernel,
        grid=(n // tm,),
        in_specs=[pl.BlockSpec((tm, d), row)] + gather_specs + [
            pl.BlockSpec((tm, 2), row),
            pl.BlockSpec((None, 6, d), lambda i: (i // tiles_per_seq, 0, 0)),
            pl.BlockSpec((1, d), lambda i: (0, 0))],
        out_specs=pl.BlockSpec((tm, d), row),
        out_shape=jax.ShapeDtypeStruct((n, d), F32),
        scratch_shapes=scratch,
        compiler_params=_cparams(("arbitrary",)),
        name="final_norm",
    )(x1, dest8, dest8, y8, gates, mod_l, g)


def _rope_tables(seq):
    freqs = ROPE_THETA ** (-np.arange(0, A_QK_DIM, 2, dtype=np.float32) / A_QK_DIM)
    lane = np.arange(LANES)
    f = freqs[lane % 16][None, :]
    sign = np.where(lane < LANES // 2, -1.0, 1.0)[None, :].astype(np.float32)
    pos = np.arange(seq, dtype=np.float32)[:, None]
    ang_a = pos * f
    axial = np.where((lane % 32) < 16, np.floor(pos / GRID_W), np.mod(pos, GRID_W)).astype(np.float32)
    ang_b = axial * f
    tabs = (np.cos(ang_a), np.sin(ang_a) * sign, np.cos(ang_b), np.sin(ang_b) * sign)
    return tuple(jnp.asarray(t, dtype=F32) for t in tabs)


def _qb_perm():
    cols = []
    for j in range(B_HEADS // B_KV_HEADS):
        cols += list(range(j * HEAD_DIM, (j + 1) * HEAD_DIM))
        cols += list(range((4 + j) * HEAD_DIM, (5 + j) * HEAD_DIM))
    return np.asarray(cols, dtype=np.int32)


def _split_half_perms():
    lane = np.arange(LANES)
    half, f = lane // 64, lane % 16
    slot = (lane % 64) // 16
    perm_a = slot * 32 + half * 16 + f
    head, part = (lane % 64) // 32, (lane % 32) // 16
    perm_b = head * 64 + part * 32 + half * 16 + f
    return perm_a.astype(np.int32), perm_b.astype(np.int32)


def _trunk(x, mod, batch, seq, p):
    n = batch * seq
    d = D_MODEL
    depth = p["w_in"].shape[0]
    tabs = _rope_tables(seq)
    perm = _qb_perm()
    perm_a, perm_b = _split_half_perms()
    sub = min(TOKEN_TILE, ROUTER_SUBTILE)
    tri = jnp.asarray(np.triu(np.ones((sub, sub), np.float32), 1), dtype=BF16)
    rb = EXPERT_ROWS
    cap = (2 * n + N_EXPERTS * (rb - 1) + rb - 1) // rb * rb
    nb = cap // rb
    x = x.reshape(n, d)
    moe = None
    buf8 = None
    for l in range(depth):
        lambda_init = 0.8 - 0.6 * math.exp(-0.3 * l)
        mod_l = mod[l].reshape(batch, 6, d)
        cols = np.arange(IN_COLS, dtype=np.int32)
        for blk in range(2 * A_WIDTH // LANES):
            cols[blk * LANES:(blk + 1) * LANES] = blk * LANES + perm_a
        qb0 = 3 * A_WIDTH
        for blk in range(B_WIDTH // LANES):
            cols[qb0 + blk * LANES:qb0 + (blk + 1) * LANES] = qb0 + perm[blk * LANES + perm_b]
        cols[qb0 + B_WIDTH:qb0 + B_WIDTH + B_KV_WIDTH] = qb0 + B_WIDTH + perm_b
        w_in = p["w_in"][l][:, cols].astype(BF16)
        tile2 = lambda v: jnp.tile(v.reshape(1, HEAD_DIM), (1, 2))
        x, qa, ka, va, qb, kb, vb, cp = _inproj(x, moe, mod_l, p["norm1"][l].reshape(1, d), w_in,
                                                tile2(p["q_norm"][l])[:, perm_b], tile2(p["k_norm"][l])[:, perm_b],
                                                tabs, seq)
        oa = _diff_attn(qa, ka, va, p["diff_lambda"][l], tile2(p["diff_subln"][l]), lambda_init, batch, seq)
        ob = _gqa_attn(qb, kb, vb, batch, seq)
        oc = _short_conv(cp, p["conv_w"][l], batch, seq)
        w_out = p["w_out"][l]
        wa = w_out[:A_WIDTH].astype(BF16)
        wb = w_out[A_WIDTH + perm].astype(BF16)
        wc = w_out[A_WIDTH + B_WIDTH:].astype(BF16)
        wr = jnp.zeros((d, LANES), F32)
        wr = wr.at[:, :N_GROUPS].set(p["router_group_w"][l]).at[:, 8:8 + N_EXPERTS].set(p["router_expert_w"][l])
        rhi = wr.astype(BF16)
        rlo = (wr - rhi.astype(F32)).astype(BF16)
        rbias = jnp.zeros((1, LANES), F32)
        rbias = rbias.at[0, :N_GROUPS].set(p["router_group_b"][l]).at[0, 8:8 + N_EXPERTS].set(p["router_expert_b"][l])
        x1, h8, eid, gate, rank, cnt = _outproj(oa, ob, oc, x, mod_l, p["norm2"][l].reshape(1, d),
                                                wa, wb, wc, rhi, rlo, rbias, tri, seq)
        counts = cnt[:, 0].astype(jnp.int32)
        pcounts = (counts + rb - 1) // rb * rb
        pends = jnp.cumsum(pcounts)
        pstarts = pends - pcounts
        block_row = jnp.arange(nb, dtype=jnp.int32)[:, None] * rb
        block_e = jnp.minimum(jnp.sum((pends[None, :] <= block_row).astype(jnp.int32), axis=1), N_EXPERTS - 1)
        n_used = (pends[-1:] // rb).astype(jnp.int32)
        expert_ids = jnp.arange(N_EXPERTS, dtype=jnp.int32)[:, None, None]
        dest8 = (jnp.sum(jnp.where(eid[None] == expert_ids, pstarts[:, None, None], 0), axis=0) + rank) * SUBLANES
        if buf8 is None:
            buf8 = jnp.zeros((cap * SUBLANES, LANES), F32)
        buf8 = _dispatch(dest8, h8, buf8)
        y8 = _experts(block_e, n_used, buf8, p["expert_w1"], p["expert_w3"], p["expert_w2"], l)
        x = x1
        moe = (dest8, y8, gate.T, mod_l)
    out = _final(x, *moe, p["final_norm"].reshape(1, d), seq)
    return out.reshape(batch, seq, d)


def kernel(x_prompt, x_sample, c_prompt, c_sample, w_ada, b_ada, norm1, norm2, w_in, w_out, diff_lambda,
           diff_subln, q_norm, k_norm, conv_w, router_group_w, router_group_b, router_expert_w,
           router_expert_b, expert_w1, expert_w3, expert_w2, final_norm):
    p = dict(norm1=norm1, norm2=norm2, w_in=w_in, w_out=w_out, diff_lambda=diff_lambda, diff_subln=diff_subln,
             q_norm=q_norm, k_norm=k_norm, conv_w=conv_w, router_group_w=router_group_w,
             router_group_b=router_group_b, router_expert_w=router_expert_w, router_expert_b=router_expert_b,
             expert_w1=expert_w1, expert_w3=expert_w3, expert_w2=expert_w2, final_norm=final_norm)
    bp, sp, _ = x_prompt.shape
    bs, ss, _ = x_sample.shape
    mod = _ada_mod(jnp.concatenate([c_prompt, c_sample], axis=0), w_ada, b_ada)
    y_prompt = _trunk(x_prompt, mod[:, :bp], bp, sp, p)
    y_sample = _trunk(x_sample, mod[:, bp:], bs, ss, p)
    return (y_prompt, y_sample)
```

```python
import functools
import math

import numpy as np
import jax
import jax.numpy as jnp
from jax import lax
from jax.experimental import pallas as pl
from jax.experimental.pallas import tpu as pltpu

F32 = jnp.float32
BF16 = jnp.bfloat16

D_MODEL = 1024
HEAD_DIM = 64
A_HEADS = 4
A_QK_DIM = 32
A_WIDTH = 256
B_HEADS = 8
B_KV_HEADS = 2
B_WIDTH = 512
B_KV_WIDTH = 128
C_WIDTH = 256
IN_COLS = 2304
N_GROUPS = 4
EXPERTS_PER_GROUP = 8
N_EXPERTS = 32
D_EXPERT = 512
GRID_W = 64
ROPE_THETA = 10000.0
EPS = 1e-6
LANES = 128
LOG2E = 1.4426950408889634

VMEM_LIMIT = 56 * 1024 * 1024

TOKEN_TILE = 512
PROJ_SUBTILE = 256
ROUTER_SUBTILE = 512
Q_TILE = 256
SCORE_ELEMS = 2 * 1024 * 1024
EXPERT_ROWS = 512
EXPERT_COL_CHUNK = 256
DMA_TILE = 1024
DMA_UNROLL = 8


def _cparams(sem):
    return pltpu.CompilerParams(dimension_semantics=sem, vmem_limit_bytes=VMEM_LIMIT)


SUBLANES = 8
LANE_BLOCKS = D_MODEL // LANES
assert LANE_BLOCKS == SUBLANES


def _store_token_tiles(ref, val, first=0, per_token=SUBLANES):
    rows = val.shape[0]
    for c in range(LANE_BLOCKS):
        ref[pl.ds(first + c, rows, stride=per_token), :] = val[:, c * LANES:(c + 1) * LANES]


def _load_token_tiles(ref, rows, first=0, per_token=SUBLANES):
    return jnp.concatenate([ref[pl.ds(first + c, rows, stride=per_token), :] for c in range(LANE_BLOCKS)], axis=1)


def _ada_kernel(c_ref, w_ref, b_ref, o_ref):
    c = c_ref[...]
    s = c * jax.nn.sigmoid(c)
    o_ref[...] = jnp.dot(s, w_ref[...], precision=lax.Precision.HIGHEST,
                         preferred_element_type=F32) + b_ref[...]


def _ada_mod(c_all, w_ada, b_ada):
    depth, d, six_d = w_ada.shape
    bt = c_all.shape[0]
    nt = six_d // d
    return pl.pallas_call(
        _ada_kernel,
        grid=(depth, nt),
        in_specs=[pl.BlockSpec((bt, d), lambda l, j: (0, 0)),
                  pl.BlockSpec((None, d, d), lambda l, j: (l, 0, j)),
                  pl.BlockSpec((None, 1, d), lambda l, j: (l, 0, j))],
        out_specs=pl.BlockSpec((None, bt, d), lambda l, j: (l, 0, j)),
        out_shape=jax.ShapeDtypeStruct((depth, bt, six_d), F32),
        compiler_params=_cparams(("arbitrary", "arbitrary")),
        name="ada_mod",
    )(c_all, w_ada, b_ada.reshape(depth, 1, six_d))


def _rope128(x, cos, sin_signed):
    return x * cos + pltpu.roll(x, LANES // 2, 1) * sin_signed


def _qk_first_head(shape):
    return (lax.broadcasted_iota(jnp.int32, shape, 1) % (LANES // 2)) < HEAD_DIM // 2


def _head_rms(x, gain, lo):
    sq = x * x
    s_lo = jnp.sum(jnp.where(lo, sq, 0.0), axis=-1, keepdims=True)
    s_hi = jnp.sum(jnp.where(lo, 0.0, sq), axis=-1, keepdims=True)
    ms = jnp.where(lo, s_lo, s_hi) * (1.0 / HEAD_DIM)
    return x * lax.rsqrt(ms + EPS) * gain


def _start_expert_row_gather(dest_ref, y_hbm, ybuf, sem, slot):
    def body(t, carry):
        for k in range(2):
            src = y_hbm.at[pl.ds(pl.multiple_of(dest_ref[2 * t + k], SUBLANES), SUBLANES)]
            dst = ybuf.at[slot, pl.ds(pl.multiple_of((2 * t + k) * SUBLANES, SUBLANES), SUBLANES)]
            pltpu.make_async_copy(src, dst, sem.at[slot]).start(priority=k)
        return carry

    lax.fori_loop(0, dest_ref.shape[0] // 2, body, 0, unroll=DMA_UNROLL)


def _gated_pair(dest_ref, dest_next_ref, y_hbm, ybuf, sem, gt):
    i = pl.program_id(0)
    rows = gt.shape[0]
    slot = lax.rem(i, 2)

    @pl.when(i == 0)
    def _():
        _start_expert_row_gather(dest_ref, y_hbm, ybuf, sem, 0)

    @pl.when(i + 1 < pl.num_programs(0))
    def _():
        _start_expert_row_gather(dest_next_ref, y_hbm, ybuf, sem, 1 - slot)

    pltpu.make_async_copy(y_hbm.at[pl.ds(0, 2 * rows * SUBLANES)], ybuf.at[slot], sem.at[slot]).wait()
    y_ref = ybuf.at[slot]
    y0 = _load_token_tiles(y_ref, rows, 0, 2 * SUBLANES)
    y1 = _load_token_tiles(y_ref, rows, SUBLANES, 2 * SUBLANES)
    return gt[:, 0:1] * y0 + gt[:, 1:2] * y1


def _moe_gather_specs(tm, n_tiles):
    in_specs = [pl.BlockSpec((2 * tm,), lambda i: (i,), memory_space=pltpu.SMEM),
                pl.BlockSpec((2 * tm,), lambda i: (jnp.minimum(i + 1, n_tiles - 1),), memory_space=pltpu.SMEM),
                pl.BlockSpec(memory_space=pl.ANY)]
    scratch = [pltpu.VMEM((2, 2 * tm * SUBLANES, LANES), F32), pltpu.SemaphoreType.DMA((2,))]
    return in_specs, scratch


def _rms_mod(x, gain, scale, shift):
    ms = jnp.mean(x * x, axis=-1, keepdims=True)
    return (x * lax.rsqrt(ms + EPS) * gain) * (1.0 + scale) + shift


def _inproj_kernel(with_moe, *refs):
    if with_moe:
        (x_ref, dest_ref, destn_ref, y_hbm, gt_ref, modp_ref, mod_ref, g_ref, w_ref, qn_ref, kn_ref,
         ca_ref, sa_ref, cb_ref, sb_ref,
         xo_ref, qa_ref, ka_ref, va_ref, qb_ref, kb_ref, vb_ref, cp_ref, ybuf, ysem) = refs
    else:
        (x_ref, mod_ref, g_ref, w_ref, qn_ref, kn_ref, ca_ref, sa_ref, cb_ref, sb_ref,
         qa_ref, ka_ref, va_ref, qb_ref, kb_ref, vb_ref, cp_ref) = refs
    tm = x_ref.shape[0]
    if with_moe:
        moe_out = _gated_pair(dest_ref, destn_ref, y_hbm, ybuf, ysem, gt_ref[...])
    qa_scale = (A_QK_DIM ** -0.5) * LOG2E
    qb_scale = (HEAD_DIM ** -0.5) * LOG2E
    sub = min(tm, PROJ_SUBTILE)
    first = _qk_first_head((sub, LANES))
    for r in range(tm // sub):
        rows = slice(r * sub, (r + 1) * sub)
        x = x_ref[rows, :]
        if with_moe:
            x = x + modp_ref[5:6, :] * moe_out[rows, :]
            xo_ref[rows, :] = x
        hb = _rms_mod(x, g_ref[...], mod_ref[1:2, :], mod_ref[0:1, :]).astype(BF16)
        proj = jnp.dot(hb, w_ref[...], preferred_element_type=F32)
        ca, sa, cb, sb = ca_ref[rows, :], sa_ref[rows, :], cb_ref[rows, :], sb_ref[rows, :]
        for i in range(A_WIDTH // LANES):
            sl = slice(i * LANES, (i + 1) * LANES)
            qa_ref[rows, sl] = (_rope128(proj[:, sl], ca, sa) * qa_scale).astype(BF16)
            ka_ref[rows, sl] = _rope128(proj[:, A_WIDTH + i * LANES:A_WIDTH + (i + 1) * LANES], ca, sa).astype(BF16)
        va_ref[rows, :] = proj[:, 2 * A_WIDTH:3 * A_WIDTH].astype(BF16)
        o = 3 * A_WIDTH
        for j in range(B_WIDTH // LANES):
            q = _head_rms(proj[:, o + j * LANES:o + (j + 1) * LANES], qn_ref[...], first)
            qb_ref[rows, j * LANES:(j + 1) * LANES] = (_rope128(q, cb, sb) * qb_scale).astype(BF16)
        o += B_WIDTH
        k = _head_rms(proj[:, o:o + B_KV_WIDTH], kn_ref[...], first)
        kb_ref[rows, :] = _rope128(k, cb, sb).astype(BF16)
        o += B_KV_WIDTH
        vb_ref[rows, :] = proj[:, o:o + B_KV_WIDTH].astype(BF16)
        o += B_KV_WIDTH
        cp_ref[rows, :C_WIDTH] = proj[:, o:o + C_WIDTH]
        cp_ref[rows, C_WIDTH:] = proj[:, o + C_WIDTH:o + 2 * C_WIDTH] * proj[:, o + 2 * C_WIDTH:o + 3 * C_WIDTH]


def _inproj(x, moe, mod_l, g1, w, qn, kn, tabs, seq):
    n, d = x.shape
    tm = TOKEN_TILE
    tiles_per_seq = seq // tm
    row = lambda i: (i, 0)
    const = lambda i: (0, 0)
    pos = lambda i: (i % tiles_per_seq, 0)
    in_specs = [pl.BlockSpec((tm, d), row)]
    args = [x]
    scratch = []
    if moe is not None:
        dest8, y8, gates, mod_prev = moe
        gather_specs, scratch = _moe_gather_specs(tm, n // tm)
        in_specs += gather_specs + [pl.BlockSpec((tm, 2), row),
                                    pl.BlockSpec((None, 6, d), lambda i: (i // tiles_per_seq, 0, 0))]
        args += [dest8, dest8, y8, gates, mod_prev]
    in_specs += [pl.BlockSpec((None, 6, d), lambda i: (i // tiles_per_seq, 0, 0)),
                 pl.BlockSpec((1, d), const),
                 pl.BlockSpec((d, IN_COLS), const),
                 pl.BlockSpec((1, LANES), const), pl.BlockSpec((1, LANES), const)]
    in_specs += [pl.BlockSpec((tm, LANES), pos)] * 4
    args += [mod_l, g1, w, qn, kn, *tabs]
    widths = [(A_WIDTH, BF16), (A_WIDTH, BF16), (A_WIDTH, BF16), (B_WIDTH, BF16), (B_KV_WIDTH, BF16),
              (B_KV_WIDTH, BF16), (2 * C_WIDTH, F32)]
    out_specs = [pl.BlockSpec((tm, wd), row) for wd, _ in widths]
    out_shape = [jax.ShapeDtypeStruct((n, wd), dt) for wd, dt in widths]
    if moe is not None:
        out_specs = [pl.BlockSpec((tm, d), row)] + out_specs
        out_shape = [jax.ShapeDtypeStruct((n, d), F32)] + out_shape
    outs = pl.pallas_call(
        functools.partial(_inproj_kernel, moe is not None),
        grid=(n // tm,),
        in_specs=in_specs, out_specs=out_specs, out_shape=out_shape, scratch_shapes=scratch,
        compiler_params=_cparams(("arbitrary",)),
        name="inproj",
    )(*args)
    if moe is None:
        outs = [x] + list(outs)
    return outs


def _flash(qs, k_ref, v_ref, col, seq):
    rows = qs.shape[0]
    tk = min(SCORE_ELEMS // rows, seq)
    sl = slice(col * LANES, (col + 1) * LANES)

    def body(c, carry):
        m, l, acc = carry
        r0 = pl.multiple_of(c * tk, tk)
        k = k_ref[pl.ds(r0, tk), sl]
        v = v_ref[pl.ds(r0, tk), sl]
        s = lax.dot_general(qs, k, (((1,), (1,)), ((), ())), preferred_element_type=F32)
        m_new = jnp.maximum(m, jnp.max(s, axis=-1, keepdims=True))
        alpha = jnp.exp2(m - m_new)
        p = jnp.exp2(s - m_new)
        l = alpha * l + jnp.sum(p, axis=-1, keepdims=True)
        acc = alpha * acc + jnp.dot(p.astype(BF16), v, preferred_element_type=F32)
        return m_new, l, acc

    init = (jnp.full((rows, 1), -jnp.inf, F32), jnp.zeros((rows, 1), F32), jnp.zeros((rows, LANES), F32))
    _, l, acc = lax.fori_loop(0, seq // tk, body, init, unroll=True)
    return acc, l


def _diff_attn_kernel(lambda_init, seq, q_ref, k_ref, v_ref, lam_ref, g_ref, o_ref):
    lv = lam_ref[...]
    lam = (jnp.exp(jnp.sum(lv[0:1] * lv[1:2], keepdims=True))
           - jnp.exp(jnp.sum(lv[2:3] * lv[3:4], keepdims=True)) + lambda_init)
    tq = q_ref.shape[0]
    lane = lax.broadcasted_iota(jnp.int32, (tq, LANES), 1)
    lo = lane < HEAD_DIM
    for i in range(A_WIDTH // LANES):
        sl = slice(i * LANES, (i + 1) * LANES)
        q = q_ref[:, sl]
        zero = jnp.zeros_like(q)
        slot = (lane % (LANES // 2)) // (A_QK_DIM // 2)
        qs = jnp.concatenate([jnp.where(slot == j, q, zero) for j in range(4)], axis=0)
        acc, l = _flash(qs, k_ref, v_ref, i, seq)
        o = acc / l
        o_even = o[0:tq] - lam * o[tq:2 * tq]
        o_odd = o[2 * tq:3 * tq] - lam * o[3 * tq:4 * tq]
        oo = jnp.where(lo, o_even, o_odd)
        o_ref[:, sl] = (_head_rms(oo, g_ref[...], lo) * (1.0 - lambda_init)).astype(BF16)


def _diff_attn(qa, ka, va, lam_vecs, subln_tiled, lambda_init, batch, seq):
    tq = Q_TILE
    q3, k3, v3 = (a.reshape(batch, seq, A_WIDTH) for a in (qa, ka, va))
    kv_spec = pl.BlockSpec((None, seq, A_WIDTH), lambda b, i: (b, 0, 0))
    out = pl.pallas_call(
        functools.partial(_diff_attn_kernel, lambda_init, seq),
        grid=(batch, seq // tq),
        in_specs=[pl.BlockSpec((None, tq, A_WIDTH), lambda b, i: (b, i, 0)),
                  kv_spec, kv_spec,
                  pl.BlockSpec((4, A_QK_DIM), lambda b, i: (0, 0)),
                  pl.BlockSpec((1, LANES), lambda b, i: (0, 0))],
        out_specs=pl.BlockSpec((None, tq, A_WIDTH), lambda b, i: (b, i, 0)),
        out_shape=jax.ShapeDtypeStruct((batch, seq, A_WIDTH), BF16),
        compiler_params=_cparams(("arbitrary", "arbitrary")),
        name="diff_attn",
    )(q3, k3, v3, lam_vecs, subln_tiled)
    return out.reshape(batch * seq, A_WIDTH)


def _gqa_kernel(seq, q_ref, k_ref, v_ref, o_ref):
    tq = q_ref.shape[0]
    lane = lax.broadcasted_iota(jnp.int32, (tq, LANES), 1)
    lo = lane < HEAD_DIM
    for j in range(B_WIDTH // LANES):
        q = q_ref[:, j * LANES:(j + 1) * LANES]
        zero = jnp.zeros_like(q)
        first = _qk_first_head(q.shape)
        qs = jnp.concatenate([jnp.where(first, q, zero), jnp.where(first, zero, q)], axis=0)
        acc, l = _flash(qs, k_ref, v_ref, 0, seq)
        o = acc / l
        o_ref[:, j * LANES:(j + 1) * LANES] = jnp.where(lo, o[0:tq], o[tq:2 * tq]).astype(BF16)


def _gqa_attn(qb, kb, vb, batch, seq):
    tq = 2 * Q_TILE
    q3 = qb.reshape(batch, seq, B_WIDTH)
    k3 = kb.reshape(batch, seq, B_KV_WIDTH)
    v3 = vb.reshape(batch, seq, B_KV_WIDTH)
    out = pl.pallas_call(
        functools.partial(_gqa_kernel, seq),
        grid=(batch, seq // tq),
        in_specs=[pl.BlockSpec((None, tq, B_WIDTH), lambda b, i: (b, i, 0)),
                  pl.BlockSpec((None, seq, B_KV_WIDTH), lambda b, i: (b, 0, 0)),
                  pl.BlockSpec((None, seq, B_KV_WIDTH), lambda b, i: (b, 0, 0))],
        out_specs=pl.BlockSpec((None, tq, B_WIDTH), lambda b, i: (b, i, 0)),
        out_shape=jax.ShapeDtypeStruct((batch, seq, B_WIDTH), BF16),
        compiler_params=_cparams(("arbitrary", "arbitrary")),
        name="gqa_attn",
    )(q3, k3, v3)
    return out.reshape(batch * seq, B_WIDTH)


def _short_conv(cp_ref, prev_ref, next_ref, w_ref, tile_in_seq, tiles_per_seq):
    tc = cp_ref.shape[0]
    gate = cp_ref[:, :C_WIDTH]
    u = cp_ref[:, C_WIDTH:]
    row = lax.broadcasted_iota(jnp.int32, u.shape, 0)
    before = jnp.where(tile_in_seq == 0, 0.0, prev_ref[SUBLANES - 1:SUBLANES, C_WIDTH:])
    after = jnp.where(tile_in_seq == tiles_per_seq - 1, 0.0, next_ref[0:1, C_WIDTH:])
    u_prev = jnp.where(row == 0, before, pltpu.roll(u, 1, 0))
    u_next = jnp.where(row == tc - 1, after, pltpu.roll(u, tc - 1, 0))
    w = w_ref[...]
    return (gate * (w[0:1] * u_prev + w[1:2] * u + w[2:3] * u_next)).astype(BF16)


def _outproj_kernel(tiles_per_seq, oa_ref, ob_ref, cp_ref, cprev_ref, cnext_ref, cw_ref, x_ref, mod_ref, g_ref,
                    wa_ref, wb_ref, wc_ref, rhi_ref, rlo_ref, rb_ref, tri_ref,
                    x1_ref, h_ref, eid_ref, gate_ref, rank_ref, cnt_ref, run_ref):
    i = pl.program_id(0)
    oc = _short_conv(cp_ref, cprev_ref, cnext_ref, cw_ref, lax.rem(i, tiles_per_seq), tiles_per_seq)
    tm = x_ref.shape[0]

    @pl.when(i == 0)
    def _():
        run_ref[...] = jnp.zeros_like(run_ref)

    sub = tri_ref.shape[0]
    run = run_ref[...]
    for r in range(tm // sub):
        rows = slice(r * sub, (r + 1) * sub)
        mix = (jnp.dot(oa_ref[rows, :], wa_ref[...], preferred_element_type=F32)
               + jnp.dot(ob_ref[rows, :], wb_ref[...], preferred_element_type=F32)
               + jnp.dot(oc[rows, :], wc_ref[...], preferred_element_type=F32))
        x1 = x_ref[rows, :] + mod_ref[2:3, :] * mix
        x1_ref[rows, :] = x1
        h = _rms_mod(x1, g_ref[...], mod_ref[4:5, :], mod_ref[3:4, :])
        _store_token_tiles(h_ref, h, first=r * sub * SUBLANES)

        hi = h.astype(BF16)
        lo = (h - hi.astype(F32)).astype(BF16)
        logits = (jnp.dot(hi, rhi_ref[...], preferred_element_type=F32)
                  + jnp.dot(hi, rlo_ref[...], preferred_element_type=F32)
                  + jnp.dot(lo, rhi_ref[...], preferred_element_type=F32)) + rb_ref[...]
        lt = logits.T
        r8 = lax.broadcasted_iota(jnp.int32, (8, sub), 0)
        neg = -jnp.inf
        gl = jnp.where(r8 < N_GROUPS, lt[0:8], neg)
        gmax = jnp.max(gl, axis=0, keepdims=True)
        gidx = jnp.min(jnp.where(gl == gmax, r8, 8), axis=0, keepdims=True)
        p_sel = 1.0 / jnp.sum(jnp.exp(gl - gmax), axis=0, keepdims=True)
        sel = jnp.zeros((8, sub), F32)
        for g in range(N_GROUPS):
            sel = sel + jnp.where(gidx == g, lt[8 + 8 * g:16 + 8 * g], 0.0)
        v1 = jnp.max(sel, axis=0, keepdims=True)
        i1 = jnp.min(jnp.where(sel == v1, r8, 8), axis=0, keepdims=True)
        sel2 = jnp.where(r8 == i1, neg, sel)
        v2 = jnp.max(sel2, axis=0, keepdims=True)
        i2 = jnp.min(jnp.where(sel2 == v2, r8, 8), axis=0, keepdims=True)
        e = jnp.exp(v2 - v1)
        gate_ref[0:1, rows] = p_sel / (1.0 + e)
        gate_ref[1:2, rows] = p_sel * e / (1.0 + e)
        eid0 = gidx * EXPERTS_PER_GROUP + i1
        eid1 = gidx * EXPERTS_PER_GROUP + i2
        eid_ref[0:1, rows] = eid0
        eid_ref[1:2, rows] = eid1

        r32 = lax.broadcasted_iota(jnp.int32, (N_EXPERTS, sub), 0)
        oh0 = r32 == eid0
        oh1 = r32 == eid1
        cnt = jnp.where(oh0, 1.0, 0.0) + jnp.where(oh1, 1.0, 0.0)
        before = jnp.dot(cnt.astype(BF16), tri_ref[...], preferred_element_type=F32) + run[:, 0:1]
        rank_ref[0:1, rows] = jnp.sum(jnp.where(oh0, before, 0.0), axis=0, keepdims=True).astype(jnp.int32)
        rank_ref[1:2, rows] = jnp.sum(jnp.where(oh1, before, 0.0), axis=0, keepdims=True).astype(jnp.int32)
        run = run + jnp.sum(cnt, axis=1, keepdims=True)
    run_ref[...] = run
    cnt_ref[...] = run


def _outproj(oa, ob, cp, conv_w, x, mod_l, g2, wa, wb, wc, rhi, rlo, rb, tri, seq):
    n, d = x.shape
    tm = TOKEN_TILE
    tiles_per_seq = seq // tm
    halo = tm // SUBLANES
    row = lambda i: (i, 0)
    col = lambda i: (0, i)
    const = lambda i: (0, 0)
    return pl.pallas_call(
        functools.partial(_outproj_kernel, tiles_per_seq),
        grid=(n // tm,),
        in_specs=[pl.BlockSpec((tm, A_WIDTH), row), pl.BlockSpec((tm, B_WIDTH), row),
                  pl.BlockSpec((tm, 2 * C_WIDTH), row),
                  pl.BlockSpec((SUBLANES, 2 * C_WIDTH), lambda i: (jnp.maximum(i * halo - 1, 0), 0)),
                  pl.BlockSpec((SUBLANES, 2 * C_WIDTH),
                               lambda i: (jnp.minimum((i + 1) * halo, n // SUBLANES - 1), 0)),
                  pl.BlockSpec((3, C_WIDTH), const), pl.BlockSpec((tm, d), row),
                  pl.BlockSpec((None, 6, d), lambda i: (i // tiles_per_seq, 0, 0)),
                  pl.BlockSpec((1, d), const),
                  pl.BlockSpec((A_WIDTH, d), const), pl.BlockSpec((B_WIDTH, d), const),
                  pl.BlockSpec((C_WIDTH, d), const),
                  pl.BlockSpec((d, LANES), const), pl.BlockSpec((d, LANES), const),
                  pl.BlockSpec((1, LANES), const), pl.BlockSpec(tri.shape, const)],
        out_specs=[pl.BlockSpec((tm, d), row), pl.BlockSpec((tm * SUBLANES, LANES), row),
                   pl.BlockSpec((2, tm), col), pl.BlockSpec((2, tm), col), pl.BlockSpec((2, tm), col),
                   pl.BlockSpec((N_EXPERTS, LANES), const)],
        out_shape=[jax.ShapeDtypeStruct((n, d), F32), jax.ShapeDtypeStruct((n * SUBLANES, LANES), F32),
                   jax.ShapeDtypeStruct((2, n), jnp.int32), jax.ShapeDtypeStruct((2, n), F32),
                   jax.ShapeDtypeStruct((2, n), jnp.int32),
                   jax.ShapeDtypeStruct((N_EXPERTS, LANES), F32)],
        scratch_shapes=[pltpu.VMEM((N_EXPERTS, LANES), F32)],
        compiler_params=_cparams(("arbitrary",)),
        name="outproj_router",
    )(oa, ob, cp, cp, cp, conv_w, x, mod_l, g2, wa, wb, wc, rhi, rlo, rb, tri)


def _dispatch_kernel(dest_ref, h_ref, buf_in_ref, buf_ref, sem):
    del buf_in_ref
    td = dest_ref.shape[0] // 2

    def body(t, carry):
        src = h_ref.at[pl.ds(pl.multiple_of(t * SUBLANES, SUBLANES), SUBLANES)]
        for k in range(2):
            dst = buf_ref.at[pl.ds(pl.multiple_of(dest_ref[2 * t + k], SUBLANES), SUBLANES)]
            pltpu.make_async_copy(src, dst, sem).start(priority=k)
        return carry

    lax.fori_loop(0, td, body, 0, unroll=DMA_UNROLL)
    for _ in range(2):
        pltpu.make_async_copy(h_ref, buf_ref.at[pl.ds(0, td * SUBLANES)], sem).wait()


def _dispatch(dest8, h8, buf_init):
    n = dest8.shape[0] // 2
    td = DMA_TILE
    return pl.pallas_call(
        _dispatch_kernel,
        grid=(n // td,),
        in_specs=[pl.BlockSpec((2 * td,), lambda i: (i,), memory_space=pltpu.SMEM),
                  pl.BlockSpec((td * SUBLANES, LANES), lambda i: (i, 0)),
                  pl.BlockSpec(memory_space=pl.ANY)],
        out_specs=pl.BlockSpec(memory_space=pl.ANY),
        out_shape=jax.ShapeDtypeStruct(buf_init.shape, F32),
        scratch_shapes=[pltpu.SemaphoreType.DMA(())],
        input_output_aliases={2: 0},
        compiler_params=pltpu.CompilerParams(dimension_semantics=("arbitrary",), has_side_effects=True),
        name="moe_dispatch",
    )(dest8, h8, buf_init)


def _expert_kernel(be_ref, nu_ref, x_ref, w1_ref, w3_ref, w2_ref, y_ref, w1b, w3b, w2b):
    i = pl.program_id(0)
    rb = x_ref.shape[0] // SUBLANES
    changed = jnp.logical_or(i == 0, be_ref[i] != be_ref[jnp.maximum(i - 1, 0)])

    @pl.when(changed)
    def _():
        w1b[...] = w1_ref[...].astype(BF16)
        w3b[...] = w3_ref[...].astype(BF16)
        w2b[...] = w2_ref[...].astype(BF16)

    @pl.when(i < nu_ref[0])
    def _():
        x = _load_token_tiles(x_ref, rb).astype(BF16)
        de = w1b.shape[1]
        y = None
        for c in range(0, de, EXPERT_COL_CHUNK):
            cols = slice(c, c + EXPERT_COL_CHUNK)
            a = jnp.dot(x, w1b[:, cols], preferred_element_type=F32)
            b = jnp.dot(x, w3b[:, cols], preferred_element_type=F32)
            act = (a * jax.nn.sigmoid(a) * b).astype(BF16)
            part = jnp.dot(act, w2b[cols, :], preferred_element_type=F32)
            y = part if y is None else y + part
        _store_token_tiles(y_ref, y)

    @pl.when(i >= nu_ref[0])
    def _():
        y_ref[...] = jnp.zeros_like(y_ref)


def _experts(block_e, n_used, buf8, w1, w3, w2, layer):
    cap = buf8.shape[0] // SUBLANES
    d = D_MODEL
    rb = EXPERT_ROWS
    de = w1.shape[-1]
    grid_spec = pltpu.PrefetchScalarGridSpec(
        num_scalar_prefetch=2,
        grid=(cap // rb,),
        in_specs=[pl.BlockSpec((rb * SUBLANES, LANES), lambda i, be, nu: (i, 0)),
                  pl.BlockSpec((None, None, d, de), lambda i, be, nu: (layer, be[i], 0, 0)),
                  pl.BlockSpec((None, None, d, de), lambda i, be, nu: (layer, be[i], 0, 0)),
                  pl.BlockSpec((None, None, de, d), lambda i, be, nu: (layer, be[i], 0, 0))],
        out_specs=pl.BlockSpec((rb * SUBLANES, LANES), lambda i, be, nu: (i, 0)),
        scratch_shapes=[pltpu.VMEM((d, de), BF16), pltpu.VMEM((d, de), BF16), pltpu.VMEM((de, d), BF16)],
    )
    return pl.pallas_call(
        _expert_kernel,
        grid_spec=grid_spec,
        out_shape=jax.ShapeDtypeStruct((cap * SUBLANES, LANES), F32),
        compiler_params=_cparams(("arbitrary",)),
        name="moe_experts",
    )(block_e, n_used, buf8, w1, w3, w2)


def _final_kernel(x_ref, dest_ref, destn_ref, y_hbm, gt_ref, mod_ref, g_ref, o_ref, ybuf, ysem):
    x = x_ref[...] + mod_ref[5:6, :] * _gated_pair(dest_ref, destn_ref, y_hbm, ybuf, ysem, gt_ref[...])
    ms = jnp.mean(x * x, axis=-1, keepdims=True)
    o_ref[...] = x * lax.rsqrt(ms + EPS) * g_ref[...]


def _final(x1, dest8, y8, gates, mod_l, g, seq):
    n, d = x1.shape
    tm = TOKEN_TILE
    tiles_per_seq = seq // tm
    row = lambda i: (i, 0)
    gather_specs, scratch = _moe_gather_specs(tm, n // tm)
    return pl.pallas_call(
        _final_kernel,
        grid=(n // tm,),
        in_specs=[pl.BlockSpec((tm, d), row)] + gather_specs + [
            pl.BlockSpec((tm, 2), row),
            pl.BlockSpec((None, 6, d), lambda i: (i // tiles_per_seq, 0, 0)),
            pl.BlockSpec((1, d), lambda i: (0, 0))],
        out_specs=pl.BlockSpec((tm, d), row),
        out_shape=jax.ShapeDtypeStruct((n, d), F32),
        scratch_shapes=scratch,
        compiler_params=_cparams(("arbitrary",)),
        name="final_norm",
    )(x1, dest8, dest8, y8, gates, mod_l, g)


def _rope_tables(seq):
    freqs = ROPE_THETA ** (-np.arange(0, A_QK_DIM, 2, dtype=np.float32) / A_QK_DIM)
    lane = np.arange(LANES)
    f = freqs[lane % 16][None, :]
    sign = np.where(lane < LANES // 2, -1.0, 1.0)[None, :].astype(np.float32)
    pos = np.arange(seq, dtype=np.float32)[:, None]
    ang_a = pos * f
    axial = np.where((lane % 32) < 16, np.floor(pos / GRID_W), np.mod(pos, GRID_W)).astype(np.float32)
    ang_b = axial * f
    tabs = (np.cos(ang_a), np.sin(ang_a) * sign, np.cos(ang_b), np.sin(ang_b) * sign)
    return tuple(jnp.asarray(t, dtype=F32) for t in tabs)


def _qb_perm():
    cols = []
    for j in range(B_HEADS // B_KV_HEADS):
        cols += list(range(j * HEAD_DIM, (j + 1) * HEAD_DIM))
        cols += list(range((4 + j) * HEAD_DIM, (5 + j) * HEAD_DIM))
    return np.asarray(cols, dtype=np.int32)


def _split_half_perms():
    lane = np.arange(LANES)
    half, f = lane // 64, lane % 16
    slot = (lane % 64) // 16
    perm_a = slot * 32 + half * 16 + f
    head, part = (lane % 64) // 32, (lane % 32) // 16
    perm_b = head * 64 + part * 32 + half * 16 + f
    return perm_a.astype(np.int32), perm_b.astype(np.int32)


def _trunk(x, mod, batch, seq, p):
    n = batch * seq
    d = D_MODEL
    depth = p["w_in"].shape[0]
    tabs = _rope_tables(seq)
    perm = _qb_perm()
    perm_a, perm_b = _split_half_perms()
    sub = min(TOKEN_TILE, ROUTER_SUBTILE)
    tri = jnp.asarray(np.triu(np.ones((sub, sub), np.float32), 1), dtype=BF16)
    rb = EXPERT_ROWS
    cap = (2 * n + N_EXPERTS * (rb - 1) + rb - 1) // rb * rb
    nb = cap // rb
    x = x.reshape(n, d)
    moe = None
    buf8 = None
    for l in range(depth):
        lambda_init = 0.8 - 0.6 * math.exp(-0.3 * l)
        mod_l = mod[l].reshape(batch, 6, d)
        cols = np.arange(IN_COLS, dtype=np.int32)
        for blk in range(2 * A_WIDTH // LANES):
            cols[blk * LANES:(blk + 1) * LANES] = blk * LANES + perm_a
        qb0 = 3 * A_WIDTH
        for blk in range(B_WIDTH // LANES):
            cols[qb0 + blk * LANES:qb0 + (blk + 1) * LANES] = qb0 + perm[blk * LANES + perm_b]
        cols[qb0 + B_WIDTH:qb0 + B_WIDTH + B_KV_WIDTH] = qb0 + B_WIDTH + perm_b
        w_in = p["w_in"][l][:, cols].astype(BF16)
        tile2 = lambda v: jnp.tile(v.reshape(1, HEAD_DIM), (1, 2))
        x, qa, ka, va, qb, kb, vb, cp = _inproj(x, moe, mod_l, p["norm1"][l].reshape(1, d), w_in,
                                                tile2(p["q_norm"][l])[:, perm_b], tile2(p["k_norm"][l])[:, perm_b],
                                                tabs, seq)
        oa = _diff_attn(qa, ka, va, p["diff_lambda"][l], tile2(p["diff_subln"][l]), lambda_init, batch, seq)
        ob = _gqa_attn(qb, kb, vb, batch, seq)
        w_out = p["w_out"][l]
        wa = w_out[:A_WIDTH].astype(BF16)
        wb = w_out[A_WIDTH + perm].astype(BF16)
        wc = w_out[A_WIDTH + B_WIDTH:].astype(BF16)
        wr = jnp.zeros((d, LANES), F32)
        wr = wr.at[:, :N_GROUPS].set(p["router_group_w"][l]).at[:, 8:8 + N_EXPERTS].set(p["router_expert_w"][l])
        rhi = wr.astype(BF16)
        rlo = (wr - rhi.astype(F32)).astype(BF16)
        rbias = jnp.zeros((1, LANES), F32)
        rbias = rbias.at[0, :N_GROUPS].set(p["router_group_b"][l]).at[0, 8:8 + N_EXPERTS].set(p["router_expert_b"][l])
        x1, h8, eid, gate, rank, cnt = _outproj(oa, ob, cp, p["conv_w"][l], x, mod_l, p["norm2"][l].reshape(1, d),
                                                wa, wb, wc, rhi, rlo, rbias, tri, seq)
        counts = cnt[:, 0].astype(jnp.int32)
        pcounts = (counts + rb - 1) // rb * rb
        pends = jnp.cumsum(pcounts)
        pstarts = pends - pcounts
        block_row = jnp.arange(nb, dtype=jnp.int32)[:, None] * rb
        block_e = jnp.minimum(jnp.sum((pends[None, :] <= block_row).astype(jnp.int32), axis=1), N_EXPERTS - 1)
        n_used = (pends[-1:] // rb).astype(jnp.int32)
        expert_ids = jnp.arange(N_EXPERTS, dtype=jnp.int32)[:, None, None]
        dest8 = (jnp.sum(jnp.where(eid[None] == expert_ids, pstarts[:, None, None], 0), axis=0) + rank) * SUBLANES
        dest8 = dest8.T.reshape(2 * n)
        if buf8 is None:
            buf8 = jnp.zeros((cap * SUBLANES, LANES), F32)
        buf8 = _dispatch(dest8, h8, buf8)
        y8 = _experts(block_e, n_used, buf8, p["expert_w1"], p["expert_w3"], p["expert_w2"], l)
        x = x1
        moe = (dest8, y8, gate.T, mod_l)
    out = _final(x, *moe, p["final_norm"].reshape(1, d), seq)
    return out.reshape(batch, seq, d)


def kernel(x_prompt, x_sample, c_prompt, c_sample, w_ada, b_ada, norm1, norm2, w_in, w_out, diff_lambda,
           diff_subln, q_norm, k_norm, conv_w, router_group_w, router_group_b, router_expert_w,
           router_expert_b, expert_w1, expert_w3, expert_w2, final_norm):
    p = dict(norm1=norm1, norm2=norm2, w_in=w_in, w_out=w_out, diff_lambda=diff_lambda, diff_subln=diff_subln,
             q_norm=q_norm, k_norm=k_norm, conv_w=conv_w, router_group_w=router_group_w,
             router_group_b=router_group_b, router_expert_w=router_expert_w, router_expert_b=router_expert_b,
             expert_w1=expert_w1, expert_w3=expert_w3, expert_w2=expert_w2, final_norm=final_norm)
    bp, sp, _ = x_prompt.shape
    bs, ss, _ = x_sample.shape
    mod = _ada_mod(jnp.concatenate([c_prompt, c_sample], axis=0), w_ada, b_ada)
    y_prompt = _trunk(x_prompt, mod[:, :bp], bp, sp, p)
    y_sample = _trunk(x_sample, mod[:, bp:], bs, ss, p)
    return (y_prompt, y_sample)
```

```python
import functools
import math

import numpy as np
import jax
import jax.numpy as jnp
from jax import lax
from jax.experimental import pallas as pl
from jax.experimental.pallas import tpu as pltpu

F32 = jnp.float32
BF16 = jnp.bfloat16

D_MODEL = 1024
HEAD_DIM = 64
A_HEADS = 4
A_QK_DIM = 32
A_WIDTH = 256
B_HEADS = 8
B_KV_HEADS = 2
B_WIDTH = 512
B_KV_WIDTH = 128
C_WIDTH = 256
IN_COLS = 2304
N_GROUPS = 4
EXPERTS_PER_GROUP = 8
N_EXPERTS = 32
D_EXPERT = 512
GRID_W = 64
ROPE_THETA = 10000.0
EPS = 1e-6
LANES = 128
LOG2E = 1.4426950408889634

VMEM_LIMIT = 56 * 1024 * 1024

TOKEN_TILE = 512
PROJ_SUBTILE = 256
ROUTER_SUBTILE = 512
Q_TILE = 256
SCORE_ELEMS = 2 * 1024 * 1024
EXPERT_ROWS = 512
EXPERT_COL_CHUNK = 256
DMA_TILE = 1024
DMA_UNROLL = 8


def _cparams(sem):
    return pltpu.CompilerParams(dimension_semantics=sem, vmem_limit_bytes=VMEM_LIMIT)


SUBLANES = 8
LANE_BLOCKS = D_MODEL // LANES
assert LANE_BLOCKS == SUBLANES


def _store_token_tiles(ref, val, first=0, per_token=SUBLANES):
    rows = val.shape[0]
    for c in range(LANE_BLOCKS):
        ref[pl.ds(first + c, rows, stride=per_token), :] = val[:, c * LANES:(c + 1) * LANES]


def _load_token_tiles(ref, rows, first=0, per_token=SUBLANES):
    return jnp.concatenate([ref[pl.ds(first + c, rows, stride=per_token), :] for c in range(LANE_BLOCKS)], axis=1)


def _ada_kernel(c_ref, w_ref, b_ref, o_ref):
    c = c_ref[...]
    s = c * jax.nn.sigmoid(c)
    o_ref[...] = jnp.dot(s, w_ref[...], precision=lax.Precision.HIGHEST,
                         preferred_element_type=F32) + b_ref[...]


def _ada_mod(c_all, w_ada, b_ada):
    depth, d, six_d = w_ada.shape
    bt = c_all.shape[0]
    nt = six_d // d
    return pl.pallas_call(
        _ada_kernel,
        grid=(depth, nt),
        in_specs=[pl.BlockSpec((bt, d), lambda l, j: (0, 0)),
                  pl.BlockSpec((None, d, d), lambda l, j: (l, 0, j)),
                  pl.BlockSpec((None, 1, d), lambda l, j: (l, 0, j))],
        out_specs=pl.BlockSpec((None, bt, d), lambda l, j: (l, 0, j)),
        out_shape=jax.ShapeDtypeStruct((depth, bt, six_d), F32),
        compiler_params=_cparams(("arbitrary", "arbitrary")),
        name="ada_mod",
    )(c_all, w_ada, b_ada.reshape(depth, 1, six_d))


def _rope128(x, cos, sin_signed):
    return x * cos + pltpu.roll(x, LANES // 2, 1) * sin_signed


def _qk_first_head(shape):
    return (lax.broadcasted_iota(jnp.int32, shape, 1) % (LANES // 2)) < HEAD_DIM // 2


def _head_rms(x, gain, lo):
    sq = x * x
    s_lo = jnp.sum(jnp.where(lo, sq, 0.0), axis=-1, keepdims=True)
    s_hi = jnp.sum(jnp.where(lo, 0.0, sq), axis=-1, keepdims=True)
    ms = jnp.where(lo, s_lo, s_hi) * (1.0 / HEAD_DIM)
    return x * lax.rsqrt(ms + EPS) * gain


def _start_expert_row_gather(dest_ref, y_hbm, ybuf, sem, slot):
    def body(t, carry):
        for k in range(2):
            src = y_hbm.at[pl.ds(pl.multiple_of(dest_ref[2 * t + k], SUBLANES), SUBLANES)]
            dst = ybuf.at[slot, pl.ds(pl.multiple_of((2 * t + k) * SUBLANES, SUBLANES), SUBLANES)]
            pltpu.make_async_copy(src, dst, sem.at[slot]).start(priority=k)
        return carry

    lax.fori_loop(0, dest_ref.shape[0] // 2, body, 0, unroll=DMA_UNROLL)


def _gated_pair(dest_ref, dest_next_ref, y_hbm, ybuf, sem, gt):
    i = pl.program_id(0)
    rows = gt.shape[0]
    slot = lax.rem(i, 2)

    @pl.when(i == 0)
    def _():
        _start_expert_row_gather(dest_ref, y_hbm, ybuf, sem, 0)

    @pl.when(i + 1 < pl.num_programs(0))
    def _():
        _start_expert_row_gather(dest_next_ref, y_hbm, ybuf, sem, 1 - slot)

    pltpu.make_async_copy(y_hbm.at[pl.ds(0, 2 * rows * SUBLANES)], ybuf.at[slot], sem.at[slot]).wait()
    y_ref = ybuf.at[slot]
    y0 = _load_token_tiles(y_ref, rows, 0, 2 * SUBLANES)
    y1 = _load_token_tiles(y_ref, rows, SUBLANES, 2 * SUBLANES)
    return gt[:, 0:1] * y0 + gt[:, 1:2] * y1


def _moe_gather_specs(tm, n_tiles):
    in_specs = [pl.BlockSpec((2 * tm,), lambda i: (i,), memory_space=pltpu.SMEM),
                pl.BlockSpec((2 * tm,), lambda i: (jnp.minimum(i + 1, n_tiles - 1),), memory_space=pltpu.SMEM),
                pl.BlockSpec(memory_space=pl.ANY)]
    scratch = [pltpu.VMEM((2, 2 * tm * SUBLANES, LANES), F32), pltpu.SemaphoreType.DMA((2,))]
    return in_specs, scratch


def _rms_mod(x, gain, scale, shift):
    ms = jnp.mean(x * x, axis=-1, keepdims=True)
    return (x * lax.rsqrt(ms + EPS) * gain) * (1.0 + scale) + shift


def _inproj_kernel(with_moe, *refs):
    if with_moe:
        (x_ref, dest_ref, destn_ref, y_hbm, gt_ref, modp_ref, mod_ref, g_ref, w_ref, qn_ref, kn_ref,
         ca_ref, sa_ref, cb_ref, sb_ref,
         xo_ref, qa_ref, ka_ref, va_ref, qb_ref, kb_ref, vb_ref, cp_ref, ybuf, ysem) = refs
    else:
        (x_ref, mod_ref, g_ref, w_ref, qn_ref, kn_ref, ca_ref, sa_ref, cb_ref, sb_ref,
         qa_ref, ka_ref, va_ref, qb_ref, kb_ref, vb_ref, cp_ref) = refs
    tm = x_ref.shape[0]
    if with_moe:
        moe_out = _gated_pair(dest_ref, destn_ref, y_hbm, ybuf, ysem, gt_ref[...])
    qa_scale = (A_QK_DIM ** -0.5) * LOG2E
    qb_scale = (HEAD_DIM ** -0.5) * LOG2E
    sub = min(tm, PROJ_SUBTILE)
    first = _qk_first_head((sub, LANES))
    for r in range(tm // sub):
        rows = slice(r * sub, (r + 1) * sub)
        x = x_ref[rows, :]
        if with_moe:
            x = x + modp_ref[5:6, :] * moe_out[rows, :]
            xo_ref[rows, :] = x
        hb = _rms_mod(x, g_ref[...], mod_ref[1:2, :], mod_ref[0:1, :]).astype(BF16)
        proj = jnp.dot(hb, w_ref[...], preferred_element_type=F32)
        ca, sa, cb, sb = ca_ref[rows, :], sa_ref[rows, :], cb_ref[rows, :], sb_ref[rows, :]
        for i in range(A_WIDTH // LANES):
            sl = slice(i * LANES, (i + 1) * LANES)
            qa_ref[rows, sl] = (_rope128(proj[:, sl], ca, sa) * qa_scale).astype(BF16)
            ka_ref[rows, sl] = _rope128(proj[:, A_WIDTH + i * LANES:A_WIDTH + (i + 1) * LANES], ca, sa).astype(BF16)
        va_ref[rows, :] = proj[:, 2 * A_WIDTH:3 * A_WIDTH].astype(BF16)
        o = 3 * A_WIDTH
        for j in range(B_WIDTH // LANES):
            q = _head_rms(proj[:, o + j * LANES:o + (j + 1) * LANES], qn_ref[...], first)
            qb_ref[rows, j * LANES:(j + 1) * LANES] = (_rope128(q, cb, sb) * qb_scale).astype(BF16)
        o += B_WIDTH
        k = _head_rms(proj[:, o:o + B_KV_WIDTH], kn_ref[...], first)
        kb_ref[rows, :] = _rope128(k, cb, sb).astype(BF16)
        o += B_KV_WIDTH
        vb_ref[rows, :] = proj[:, o:o + B_KV_WIDTH].astype(BF16)
        o += B_KV_WIDTH
        cp_ref[rows, :C_WIDTH] = proj[:, o:o + C_WIDTH]
        cp_ref[rows, C_WIDTH:] = proj[:, o + C_WIDTH:o + 2 * C_WIDTH] * proj[:, o + 2 * C_WIDTH:o + 3 * C_WIDTH]


def _inproj(x, moe, mod_l, g1, w, qn, kn, tabs, seq):
    n, d = x.shape
    tm = TOKEN_TILE
    tiles_per_seq = seq // tm
    row = lambda i: (i, 0)
    const = lambda i: (0, 0)
    pos = lambda i: (i % tiles_per_seq, 0)
    in_specs = [pl.BlockSpec((tm, d), row)]
    args = [x]
    scratch = []
    if moe is not None:
        dest8, y8, gates, mod_prev = moe
        gather_specs, scratch = _moe_gather_specs(tm, n // tm)
        in_specs += gather_specs + [pl.BlockSpec((tm, 2), row),
                                    pl.BlockSpec((None, 6, d), lambda i: (i // tiles_per_seq, 0, 0))]
        args += [dest8, dest8, y8, gates, mod_prev]
    in_specs += [pl.BlockSpec((None, 6, d), lambda i: (i // tiles_per_seq, 0, 0)),
                 pl.BlockSpec((1, d), const),
                 pl.BlockSpec((d, IN_COLS), const),
                 pl.BlockSpec((1, LANES), const), pl.BlockSpec((1, LANES), const)]
    in_specs += [pl.BlockSpec((tm, LANES), pos)] * 4
    args += [mod_l, g1, w, qn, kn, *tabs]
    widths = [(A_WIDTH, BF16), (A_WIDTH, BF16), (A_WIDTH, BF16), (B_WIDTH, BF16), (B_KV_WIDTH, BF16),
              (B_KV_WIDTH, BF16), (2 * C_WIDTH, F32)]
    out_specs = [pl.BlockSpec((tm, wd), row) for wd, _ in widths]
    out_shape = [jax.ShapeDtypeStruct((n, wd), dt) for wd, dt in widths]
    if moe is not None:
        out_specs = [pl.BlockSpec((tm, d), row)] + out_specs
        out_shape = [jax.ShapeDtypeStruct((n, d), F32)] + out_shape
    outs = pl.pallas_call(
        functools.partial(_inproj_kernel, moe is not None),
        grid=(n // tm,),
        in_specs=in_specs, out_specs=out_specs, out_shape=out_shape, scratch_shapes=scratch,
        compiler_params=_cparams(("arbitrary",)),
        name="inproj",
    )(*args)
    if moe is None:
        outs = [x] + list(outs)
    return outs


def _flash(qs, k_ref, v_ref, col, seq):
    rows = qs.shape[0]
    tk = min(SCORE_ELEMS // rows, seq)
    sl = slice(col * LANES, (col + 1) * LANES)

    def body(c, carry):
        m, l, acc = carry
        r0 = pl.multiple_of(c * tk, tk)
        k = k_ref[pl.ds(r0, tk), sl]
        v = v_ref[pl.ds(r0, tk), sl]
        s = lax.dot_general(qs, k, (((1,), (1,)), ((), ())), preferred_element_type=F32)
        m_new = jnp.maximum(m, jnp.max(s, axis=-1, keepdims=True))
        alpha = jnp.exp2(m - m_new)
        p = jnp.exp2(s - m_new)
        l = alpha * l + jnp.sum(p, axis=-1, keepdims=True)
        acc = alpha * acc + jnp.dot(p.astype(BF16), v, preferred_element_type=F32)
        return m_new, l, acc

    init = (jnp.full((rows, 1), -jnp.inf, F32), jnp.zeros((rows, 1), F32), jnp.zeros((rows, LANES), F32))
    _, l, acc = lax.fori_loop(0, seq // tk, body, init, unroll=True)
    return acc, l


def _diff_attn_kernel(lambda_init, seq, q_ref, k_ref, v_ref, lam_ref, g_ref, o_ref):
    lv = lam_ref[...]
    lam = (jnp.exp(jnp.sum(lv[0:1] * lv[1:2], keepdims=True))
           - jnp.exp(jnp.sum(lv[2:3] * lv[3:4], keepdims=True)) + lambda_init)
    tq = q_ref.shape[0]
    lane = lax.broadcasted_iota(jnp.int32, (tq, LANES), 1)
    lo = lane < HEAD_DIM
    for i in range(A_WIDTH // LANES):
        sl = slice(i * LANES, (i + 1) * LANES)
        q = q_ref[:, sl]
        zero = jnp.zeros_like(q)
        slot = (lane % (LANES // 2)) // (A_QK_DIM // 2)
        qs = jnp.concatenate([jnp.where(slot == j, q, zero) for j in range(4)], axis=0)
        acc, l = _flash(qs, k_ref, v_ref, i, seq)
        o = acc / l
        o_even = o[0:tq] - lam * o[tq:2 * tq]
        o_odd = o[2 * tq:3 * tq] - lam * o[3 * tq:4 * tq]
        oo = jnp.where(lo, o_even, o_odd)
        o_ref[:, sl] = (_head_rms(oo, g_ref[...], lo) * (1.0 - lambda_init)).astype(BF16)


def _diff_attn(qa, ka, va, lam_vecs, subln_tiled, lambda_init, batch, seq):
    tq = Q_TILE
    q3, k3, v3 = (a.reshape(batch, seq, A_WIDTH) for a in (qa, ka, va))
    kv_spec = pl.BlockSpec((None, seq, A_WIDTH), lambda b, i: (b, 0, 0))
    out = pl.pallas_call(
        functools.partial(_diff_attn_kernel, lambda_init, seq),
        grid=(batch, seq // tq),
        in_specs=[pl.BlockSpec((None, tq, A_WIDTH), lambda b, i: (b, i, 0)),
                  kv_spec, kv_spec,
                  pl.BlockSpec((4, A_QK_DIM), lambda b, i: (0, 0)),
                  pl.BlockSpec((1, LANES), lambda b, i: (0, 0))],
        out_specs=pl.BlockSpec((None, tq, A_WIDTH), lambda b, i: (b, i, 0)),
        out_shape=jax.ShapeDtypeStruct((batch, seq, A_WIDTH), BF16),
        compiler_params=_cparams(("arbitrary", "arbitrary")),
        name="diff_attn",
    )(q3, k3, v3, lam_vecs, subln_tiled)
    return out.reshape(batch * seq, A_WIDTH)


def _gqa_kernel(seq, q_ref, k_ref, v_ref, o_ref):
    tq = q_ref.shape[0]
    lane = lax.broadcasted_iota(jnp.int32, (tq, LANES), 1)
    lo = lane < HEAD_DIM
    for j in range(B_WIDTH // LANES):
        q = q_ref[:, j * LANES:(j + 1) * LANES]
        zero = jnp.zeros_like(q)
        first = _qk_first_head(q.shape)
        qs = jnp.concatenate([jnp.where(first, q, zero), jnp.where(first, zero, q)], axis=0)
        acc, l = _flash(qs, k_ref, v_ref, 0, seq)
        o = acc / l
        o_ref[:, j * LANES:(j + 1) * LANES] = jnp.where(lo, o[0:tq], o[tq:2 * tq]).astype(BF16)


def _gqa_attn(qb, kb, vb, batch, seq):
    tq = 2 * Q_TILE
    q3 = qb.reshape(batch, seq, B_WIDTH)
    k3 = kb.reshape(batch, seq, B_KV_WIDTH)
    v3 = vb.reshape(batch, seq, B_KV_WIDTH)
    out = pl.pallas_call(
        functools.partial(_gqa_kernel, seq),
        grid=(batch, seq // tq),
        in_specs=[pl.BlockSpec((None, tq, B_WIDTH), lambda b, i: (b, i, 0)),
                  pl.BlockSpec((None, seq, B_KV_WIDTH), lambda b, i: (b, 0, 0)),
                  pl.BlockSpec((None, seq, B_KV_WIDTH), lambda b, i: (b, 0, 0))],
        out_specs=pl.BlockSpec((None, tq, B_WIDTH), lambda b, i: (b, i, 0)),
        out_shape=jax.ShapeDtypeStruct((batch, seq, B_WIDTH), BF16),
        compiler_params=_cparams(("arbitrary", "arbitrary")),
        name="gqa_attn",
    )(q3, k3, v3)
    return out.reshape(batch * seq, B_WIDTH)


def _short_conv(cp_ref, prev_ref, next_ref, w_ref, tile_in_seq, tiles_per_seq):
    tc = cp_ref.shape[0]
    gate = cp_ref[:, :C_WIDTH]
    u = cp_ref[:, C_WIDTH:]
    row = lax.broadcasted_iota(jnp.int32, u.shape, 0)
    before = jnp.where(tile_in_seq == 0, 0.0, prev_ref[SUBLANES - 1:SUBLANES, C_WIDTH:])
    after = jnp.where(tile_in_seq == tiles_per_seq - 1, 0.0, next_ref[0:1, C_WIDTH:])
    u_prev = jnp.where(row == 0, before, pltpu.roll(u, 1, 0))
    u_next = jnp.where(row == tc - 1, after, pltpu.roll(u, tc - 1, 0))
    w = w_ref[...]
    return (gate * (w[0:1] * u_prev + w[1:2] * u + w[2:3] * u_next)).astype(BF16)


def _outproj_kernel(tiles_per_seq, oa_ref, ob_ref, cp_ref, cprev_ref, cnext_ref, cw_ref, x_ref, mod_ref, g_ref,
                    wa_ref, wb_ref, wc_ref, rhi_ref, rlo_ref, rb_ref, tri_ref,
                    x1_ref, h_ref, eid_ref, gate_ref, rank_ref, cnt_ref, run_ref):
    i = pl.program_id(0)
    oc = _short_conv(cp_ref, cprev_ref, cnext_ref, cw_ref, lax.rem(i, tiles_per_seq), tiles_per_seq)
    tm = x_ref.shape[0]

    @pl.when(i == 0)
    def _():
        run_ref[...] = jnp.zeros_like(run_ref)

    sub = tri_ref.shape[0]
    run = run_ref[...]
    for r in range(tm // sub):
        rows = slice(r * sub, (r + 1) * sub)
        mix = (jnp.dot(oa_ref[rows, :], wa_ref[...], preferred_element_type=F32)
               + jnp.dot(ob_ref[rows, :], wb_ref[...], preferred_element_type=F32)
               + jnp.dot(oc[rows, :], wc_ref[...], preferred_element_type=F32))
        x1 = x_ref[rows, :] + mod_ref[2:3, :] * mix
        x1_ref[rows, :] = x1
        h = _rms_mod(x1, g_ref[...], mod_ref[4:5, :], mod_ref[3:4, :])
        _store_token_tiles(h_ref, h, first=r * sub * SUBLANES)

        hi = h.astype(BF16)
        lo = (h - hi.astype(F32)).astype(BF16)
        logits = (jnp.dot(hi, rhi_ref[...], preferred_element_type=F32)
                  + jnp.dot(hi, rlo_ref[...], preferred_element_type=F32)
                  + jnp.dot(lo, rhi_ref[...], preferred_element_type=F32)) + rb_ref[...]
        lt = logits.T
        r8 = lax.broadcasted_iota(jnp.int32, (8, sub), 0)
        neg = -jnp.inf
        gl = jnp.where(r8 < N_GROUPS, lt[0:8], neg)
        gmax = jnp.max(gl, axis=0, keepdims=True)
        gidx = jnp.min(jnp.where(gl == gmax, r8, 8), axis=0, keepdims=True)
        p_sel = 1.0 / jnp.sum(jnp.exp(gl - gmax), axis=0, keepdims=True)
        sel = jnp.zeros((8, sub), F32)
        for g in range(N_GROUPS):
            sel = sel + jnp.where(gidx == g, lt[8 + 8 * g:16 + 8 * g], 0.0)
        v1 = jnp.max(sel, axis=0, keepdims=True)
        i1 = jnp.min(jnp.where(sel == v1, r8, 8), axis=0, keepdims=True)
        sel2 = jnp.where(r8 == i1, neg, sel)
        v2 = jnp.max(sel2, axis=0, keepdims=True)
        i2 = jnp.min(jnp.where(sel2 == v2, r8, 8), axis=0, keepdims=True)
        e = jnp.exp(v2 - v1)
        gate_ref[0:1, rows] = p_sel / (1.0 + e)
        gate_ref[1:2, rows] = p_sel * e / (1.0 + e)
        eid0 = gidx * EXPERTS_PER_GROUP + i1
        eid1 = gidx * EXPERTS_PER_GROUP + i2
        eid_ref[0:1, rows] = eid0
        eid_ref[1:2, rows] = eid1

        r32 = lax.broadcasted_iota(jnp.int32, (N_EXPERTS, sub), 0)
        oh0 = r32 == eid0
        oh1 = r32 == eid1
        cnt = jnp.where(oh0, 1.0, 0.0) + jnp.where(oh1, 1.0, 0.0)
        before = jnp.dot(cnt.astype(BF16), tri_ref[...], preferred_element_type=F32) + run[:, 0:1]
        rank_ref[0:1, rows] = jnp.sum(jnp.where(oh0, before, 0.0), axis=0, keepdims=True).astype(jnp.int32)
        rank_ref[1:2, rows] = jnp.sum(jnp.where(oh1, before, 0.0), axis=0, keepdims=True).astype(jnp.int32)
        run = run + jnp.sum(cnt, axis=1, keepdims=True)
    run_ref[...] = run
    cnt_ref[...] = run


def _outproj(oa, ob, cp, conv_w, x, mod_l, g2, wa, wb, wc, rhi, rlo, rb, tri, seq):
    n, d = x.shape
    tm = TOKEN_TILE
    tiles_per_seq = seq // tm
    halo = tm // SUBLANES
    row = lambda i: (i, 0)
    col = lambda i: (0, i)
    const = lambda i: (0, 0)
    return pl.pallas_call(
        functools.partial(_outproj_kernel, tiles_per_seq),
        grid=(n // tm,),
        in_specs=[pl.BlockSpec((tm, A_WIDTH), row), pl.BlockSpec((tm, B_WIDTH), row),
                  pl.BlockSpec((tm, 2 * C_WIDTH), row),
                  pl.BlockSpec((SUBLANES, 2 * C_WIDTH), lambda i: (jnp.maximum(i * halo - 1, 0), 0)),
                  pl.BlockSpec((SUBLANES, 2 * C_WIDTH),
                               lambda i: (jnp.minimum((i + 1) * halo, n // SUBLANES - 1), 0)),
                  pl.BlockSpec((3, C_WIDTH), const), pl.BlockSpec((tm, d), row),
                  pl.BlockSpec((None, 6, d), lambda i: (i // tiles_per_seq, 0, 0)),
                  pl.BlockSpec((1, d), const),
                  pl.BlockSpec((A_WIDTH, d), const), pl.BlockSpec((B_WIDTH, d), const),
                  pl.BlockSpec((C_WIDTH, d), const),
                  pl.BlockSpec((d, LANES), const), pl.BlockSpec((d, LANES), const),
                  pl.BlockSpec((1, LANES), const), pl.BlockSpec(tri.shape, const)],
        out_specs=[pl.BlockSpec((tm, d), row), pl.BlockSpec((tm * SUBLANES, LANES), row),
                   pl.BlockSpec((2, tm), col), pl.BlockSpec((2, tm), col), pl.BlockSpec((2, tm), col),
                   pl.BlockSpec((N_EXPERTS, LANES), const)],
        out_shape=[jax.ShapeDtypeStruct((n, d), F32), jax.ShapeDtypeStruct((n * SUBLANES, LANES), F32),
                   jax.ShapeDtypeStruct((2, n), jnp.int32), jax.ShapeDtypeStruct((2, n), F32),
                   jax.ShapeDtypeStruct((2, n), jnp.int32),
                   jax.ShapeDtypeStruct((N_EXPERTS, LANES), F32)],
        scratch_shapes=[pltpu.VMEM((N_EXPERTS, LANES), F32)],
        compiler_params=_cparams(("arbitrary",)),
        name="outproj_router",
    )(oa, ob, cp, cp, cp, conv_w, x, mod_l, g2, wa, wb, wc, rhi, rlo, rb, tri)


def _dispatch_kernel(dest_ref, h_ref, buf_in_ref, buf_ref, sem):
    del buf_in_ref
    td = dest_ref.shape[0] // 2

    def body(t, carry):
        src = h_ref.at[pl.ds(pl.multiple_of(t * SUBLANES, SUBLANES), SUBLANES)]
        for k in range(2):
            dst = buf_ref.at[pl.ds(pl.multiple_of(dest_ref[2 * t + k], SUBLANES), SUBLANES)]
            pltpu.make_async_copy(src, dst, sem).start(priority=k)
        return carry

    lax.fori_loop(0, td, body, 0, unroll=DMA_UNROLL)
    for _ in range(2):
        pltpu.make_async_copy(h_ref, buf_ref.at[pl.ds(0, td * SUBLANES)], sem).wait()


def _dispatch(dest8, h8, buf_init):
    n = dest8.shape[0] // 2
    td = DMA_TILE
    return pl.pallas_call(
        _dispatch_kernel,
        grid=(n // td,),
        in_specs=[pl.BlockSpec((2 * td,), lambda i: (i,), memory_space=pltpu.SMEM),
                  pl.BlockSpec((td * SUBLANES, LANES), lambda i: (i, 0)),
                  pl.BlockSpec(memory_space=pl.ANY)],
        out_specs=pl.BlockSpec(memory_space=pl.ANY),
        out_shape=jax.ShapeDtypeStruct(buf_init.shape, F32),
        scratch_shapes=[pltpu.SemaphoreType.DMA(())],
        input_output_aliases={2: 0},
        compiler_params=pltpu.CompilerParams(dimension_semantics=("arbitrary",), has_side_effects=True),
        name="moe_dispatch",
    )(dest8, h8, buf_init)


def _expert_kernel(layer, be_ref, nxt_ref, nu_ref, x_ref, w1_hbm, w3_hbm, w2_hbm, y_ref,
                   w1f, w3f, w2f, w1b, w3b, w2b, wsem, slot_ref):
    i = pl.program_id(0)
    rb = x_ref.shape[0] // SUBLANES
    e = be_ref[i]
    active = i < nu_ref[0]
    changed = jnp.logical_and(active, jnp.logical_or(i == 0, e != be_ref[jnp.maximum(i - 1, 0)]))

    def weight_copies(expert, slot):
        return [pltpu.make_async_copy(src.at[layer, expert], dst.at[slot], wsem.at[slot])
                for src, dst in ((w1_hbm, w1f), (w3_hbm, w3f), (w2_hbm, w2f))]

    @pl.when(i == 0)
    def _():
        slot_ref[0] = 0
        for c in weight_copies(e, 0):
            c.start()

    @pl.when(changed)
    def _():
        slot = slot_ref[0]
        for c in weight_copies(e, slot):
            c.wait()
        w1b[...] = w1f[slot].astype(BF16)
        w3b[...] = w3f[slot].astype(BF16)
        w2b[...] = w2f[slot].astype(BF16)
        nxt = nxt_ref[i]

        @pl.when(nxt != e)
        def _():
            for c in weight_copies(nxt, 1 - slot):
                c.start()

        slot_ref[0] = 1 - slot

    @pl.when(active)
    def _():
        x = _load_token_tiles(x_ref, rb).astype(BF16)
        de = w1b.shape[1]
        y = None
        for c in range(0, de, EXPERT_COL_CHUNK):
            cols = slice(c, c + EXPERT_COL_CHUNK)
            a = jnp.dot(x, w1b[:, cols], preferred_element_type=F32)
            b = jnp.dot(x, w3b[:, cols], preferred_element_type=F32)
            act = (a * jax.nn.sigmoid(a) * b).astype(BF16)
            part = jnp.dot(act, w2b[cols, :], preferred_element_type=F32)
            y = part if y is None else y + part
        _store_token_tiles(y_ref, y)

    @pl.when(i >= nu_ref[0])
    def _():
        y_ref[...] = jnp.zeros_like(y_ref)


def _experts(block_e, next_e, n_used, buf8, w1, w3, w2, layer):
    cap = buf8.shape[0] // SUBLANES
    d = D_MODEL
    rb = EXPERT_ROWS
    de = w1.shape[-1]
    hbm = pl.BlockSpec(memory_space=pl.ANY)
    grid_spec = pltpu.PrefetchScalarGridSpec(
        num_scalar_prefetch=3,
        grid=(cap // rb,),
        in_specs=[pl.BlockSpec((rb * SUBLANES, LANES), lambda i, be, nx, nu: (i, 0)), hbm, hbm, hbm],
        out_specs=pl.BlockSpec((rb * SUBLANES, LANES), lambda i, be, nx, nu: (i, 0)),
        scratch_shapes=[pltpu.VMEM((2, d, de), F32), pltpu.VMEM((2, d, de), F32), pltpu.VMEM((2, de, d), F32),
                        pltpu.VMEM((d, de), BF16), pltpu.VMEM((d, de), BF16), pltpu.VMEM((de, d), BF16),
                        pltpu.SemaphoreType.DMA((2,)), pltpu.SMEM((1,), jnp.int32)],
    )
    return pl.pallas_call(
        functools.partial(_expert_kernel, layer),
        grid_spec=grid_spec,
        out_shape=jax.ShapeDtypeStruct((cap * SUBLANES, LANES), F32),
        compiler_params=_cparams(("arbitrary",)),
        name="moe_experts",
    )(block_e, next_e, n_used, buf8, w1, w3, w2)


def _final_kernel(x_ref, dest_ref, destn_ref, y_hbm, gt_ref, mod_ref, g_ref, o_ref, ybuf, ysem):
    x = x_ref[...] + mod_ref[5:6, :] * _gated_pair(dest_ref, destn_ref, y_hbm, ybuf, ysem, gt_ref[...])
    ms = jnp.mean(x * x, axis=-1, keepdims=True)
    o_ref[...] = x * lax.rsqrt(ms + EPS) * g_ref[...]


def _final(x1, dest8, y8, gates, mod_l, g, seq):
    n, d = x1.shape
    tm = TOKEN_TILE
    tiles_per_seq = seq // tm
    row = lambda i: (i, 0)
    gather_specs, scratch = _moe_gather_specs(tm, n // tm)
    return pl.pallas_call(
        _final_kernel,
        grid=(n // tm,),
        in_specs=[pl.BlockSpec((tm, d), row)] + gather_specs + [
            pl.BlockSpec((tm, 2), row),
            pl.BlockSpec((None, 6, d), lambda i: (i // tiles_per_seq, 0, 0)),
            pl.BlockSpec((1, d), lambda i: (0, 0))],
        out_specs=pl.BlockSpec((tm, d), row),
        out_shape=jax.ShapeDtypeStruct((n, d), F32),
        scratch_shapes=scratch,
        compiler_params=_cparams(("arbitrary",)),
        name="final_norm",
    )(x1, dest8, dest8, y8, gates, mod_l, g)


def _rope_tables(seq):
    freqs = ROPE_THETA ** (-np.arange(0, A_QK_DIM, 2, dtype=np.float32) / A_QK_DIM)
    lane = np.arange(LANES)
    f = freqs[lane % 16][None, :]
    sign = np.where(lane < LANES // 2, -1.0, 1.0)[None, :].astype(np.float32)
    pos = np.arange(seq, dtype=np.float32)[:, None]
    ang_a = pos * f
    axial = np.where((lane % 32) < 16, np.floor(pos / GRID_W), np.mod(pos, GRID_W)).astype(np.float32)
    ang_b = axial * f
    tabs = (np.cos(ang_a), np.sin(ang_a) * sign, np.cos(ang_b), np.sin(ang_b) * sign)
    return tuple(jnp.asarray(t, dtype=F32) for t in tabs)


def _qb_perm():
    cols = []
    for j in range(B_HEADS // B_KV_HEADS):
        cols += list(range(j * HEAD_DIM, (j + 1) * HEAD_DIM))
        cols += list(range((4 + j) * HEAD_DIM, (5 + j) * HEAD_DIM))
    return np.asarray(cols, dtype=np.int32)


def _split_half_perms():
    lane = np.arange(LANES)
    half, f = lane // 64, lane % 16
    slot = (lane % 64) // 16
    perm_a = slot * 32 + half * 16 + f
    head, part = (lane % 64) // 32, (lane % 32) // 16
    perm_b = head * 64 + part * 32 + half * 16 + f
    return perm_a.astype(np.int32), perm_b.astype(np.int32)


def _trunk(x, mod, batch, seq, p):
    n = batch * seq
    d = D_MODEL
    depth = p["w_in"].shape[0]
    tabs = _rope_tables(seq)
    perm = _qb_perm()
    perm_a, perm_b = _split_half_perms()
    sub = min(TOKEN_TILE, ROUTER_SUBTILE)
    tri = jnp.asarray(np.triu(np.ones((sub, sub), np.float32), 1), dtype=BF16)
    rb = EXPERT_ROWS
    cap = (2 * n + N_EXPERTS * (rb - 1) + rb - 1) // rb * rb
    nb = cap // rb
    x = x.reshape(n, d)
    moe = None
    buf8 = None
    for l in range(depth):
        lambda_init = 0.8 - 0.6 * math.exp(-0.3 * l)
        mod_l = mod[l].reshape(batch, 6, d)
        cols = np.arange(IN_COLS, dtype=np.int32)
        for blk in range(2 * A_WIDTH // LANES):
            cols[blk * LANES:(blk + 1) * LANES] = blk * LANES + perm_a
        qb0 = 3 * A_WIDTH
        for blk in range(B_WIDTH // LANES):
            cols[qb0 + blk * LANES:qb0 + (blk + 1) * LANES] = qb0 + perm[blk * LANES + perm_b]
        cols[qb0 + B_WIDTH:qb0 + B_WIDTH + B_KV_WIDTH] = qb0 + B_WIDTH + perm_b
        w_in = p["w_in"][l][:, cols].astype(BF16)
        tile2 = lambda v: jnp.tile(v.reshape(1, HEAD_DIM), (1, 2))
        x, qa, ka, va, qb, kb, vb, cp = _inproj(x, moe, mod_l, p["norm1"][l].reshape(1, d), w_in,
                                                tile2(p["q_norm"][l])[:, perm_b], tile2(p["k_norm"][l])[:, perm_b],
                                                tabs, seq)
        oa = _diff_attn(qa, ka, va, p["diff_lambda"][l], tile2(p["diff_subln"][l]), lambda_init, batch, seq)
        ob = _gqa_attn(qb, kb, vb, batch, seq)
        w_out = p["w_out"][l]
        wa = w_out[:A_WIDTH].astype(BF16)
        wb = w_out[A_WIDTH + perm].astype(BF16)
        wc = w_out[A_WIDTH + B_WIDTH:].astype(BF16)
        wr = jnp.zeros((d, LANES), F32)
        wr = wr.at[:, :N_GROUPS].set(p["router_group_w"][l]).at[:, 8:8 + N_EXPERTS].set(p["router_expert_w"][l])
        rhi = wr.astype(BF16)
        rlo = (wr - rhi.astype(F32)).astype(BF16)
        rbias = jnp.zeros((1, LANES), F32)
        rbias = rbias.at[0, :N_GROUPS].set(p["router_group_b"][l]).at[0, 8:8 + N_EXPERTS].set(p["router_expert_b"][l])
        x1, h8, eid, gate, rank, cnt = _outproj(oa, ob, cp, p["conv_w"][l], x, mod_l, p["norm2"][l].reshape(1, d),
                                                wa, wb, wc, rhi, rlo, rbias, tri, seq)
        counts = cnt[:, 0].astype(jnp.int32)
        expert_range = jnp.arange(N_EXPERTS, dtype=jnp.int32)
        pcounts = (counts + rb - 1) // rb * rb
        pends = jnp.cumsum(pcounts)
        pstarts = pends - pcounts
        block_row = jnp.arange(nb, dtype=jnp.int32)[:, None] * rb
        block_e = jnp.minimum(jnp.sum((pends[None, :] <= block_row).astype(jnp.int32), axis=1), N_EXPERTS - 1)
        n_used = (pends[-1:] // rb).astype(jnp.int32)
        later = (expert_range[None, :] > expert_range[:, None]) & (pcounts[None, :] > 0)
        next_owner = jnp.min(jnp.where(later, expert_range[None, :], N_EXPERTS), axis=1)
        next_owner = jnp.where(next_owner == N_EXPERTS, expert_range, next_owner)
        next_e = jnp.sum(jnp.where(block_e[:, None] == expert_range[None, :], next_owner[None, :], 0), axis=1)
        expert_ids = jnp.arange(N_EXPERTS, dtype=jnp.int32)[:, None, None]
        dest8 = (jnp.sum(jnp.where(eid[None] == expert_ids, pstarts[:, None, None], 0), axis=0) + rank) * SUBLANES
        dest8 = dest8.T.reshape(2 * n)
        if buf8 is None:
            buf8 = jnp.zeros((cap * SUBLANES, LANES), F32)
        buf8 = _dispatch(dest8, h8, buf8)
        y8 = _experts(block_e, next_e.astype(jnp.int32), n_used, buf8,
                      p["expert_w1"], p["expert_w3"], p["expert_w2"], l)
        x = x1
        moe = (dest8, y8, gate.T, mod_l)
    out = _final(x, *moe, p["final_norm"].reshape(1, d), seq)
    return out.reshape(batch, seq, d)


def kernel(x_prompt, x_sample, c_prompt, c_sample, w_ada, b_ada, norm1, norm2, w_in, w_out, diff_lambda,
           diff_subln, q_norm, k_norm, conv_w, router_group_w, router_group_b, router_expert_w,
           router_expert_b, expert_w1, expert_w3, expert_w2, final_norm):
    p = dict(norm1=norm1, norm2=norm2, w_in=w_in, w_out=w_out, diff_lambda=diff_lambda, diff_subln=diff_subln,
             q_norm=q_norm, k_norm=k_norm, conv_w=conv_w, router_group_w=router_group_w,
             router_group_b=router_group_b, router_expert_w=router_expert_w, router_expert_b=router_expert_b,
             expert_w1=expert_w1, expert_w3=expert_w3, expert_w2=expert_w2, final_norm=final_norm)
    bp, sp, _ = x_prompt.shape
    bs, ss, _ = x_sample.shape
    mod = _ada_mod(jnp.concatenate([c_prompt, c_sample], axis=0), w_ada, b_ada)
    y_prompt = _trunk(x_prompt, mod[:, :bp], bp, sp, p)
    y_sample = _trunk(x_sample, mod[:, bp:], bs, ss, p)
    return (y_prompt, y_sample)
```

```python
import functools
import math

import numpy as np
import jax
import jax.numpy as jnp
from jax import lax
from jax.experimental import pallas as pl
from jax.experimental.pallas import tpu as pltpu

F32 = jnp.float32
BF16 = jnp.bfloat16

D_MODEL = 1024
HEAD_DIM = 64
A_HEADS = 4
A_QK_DIM = 32
A_WIDTH = 256
B_HEADS = 8
B_KV_HEADS = 2
B_WIDTH = 512
B_KV_WIDTH = 128
C_WIDTH = 256
IN_COLS = 2304
N_GROUPS = 4
EXPERTS_PER_GROUP = 8
N_EXPERTS = 32
D_EXPERT = 512
GRID_W = 64
ROPE_THETA = 10000.0
EPS = 1e-6
LANES = 128
LOG2E = 1.4426950408889634

VMEM_LIMIT = 56 * 1024 * 1024

TOKEN_TILE = 512
PROJ_SUBTILE = 256
ROUTER_SUBTILE = 512
Q_TILE = 256
SCORE_ELEMS = 2 * 1024 * 1024
EXPERT_ROWS = 512
EXPERT_COL_CHUNK = 256
DMA_TILE = 1024
DMA_UNROLL = 8


def _cparams(sem):
    return pltpu.CompilerParams(dimension_semantics=sem, vmem_limit_bytes=VMEM_LIMIT)


SUBLANES = 8
LANE_BLOCKS = D_MODEL // LANES
assert LANE_BLOCKS == SUBLANES


def _store_token_tiles(ref, val, first=0, per_token=SUBLANES):
    rows = val.shape[0]
    for c in range(LANE_BLOCKS):
        ref[pl.ds(first + c, rows, stride=per_token), :] = val[:, c * LANES:(c + 1) * LANES]


def _load_token_tiles(ref, rows, first=0, per_token=SUBLANES):
    return jnp.concatenate([ref[pl.ds(first + c, rows, stride=per_token), :] for c in range(LANE_BLOCKS)], axis=1)


def _ada_kernel(c_ref, w_ref, b_ref, o_ref):
    c = c_ref[...]
    s = c * jax.nn.sigmoid(c)
    o_ref[...] = jnp.dot(s, w_ref[...], precision=lax.Precision.HIGHEST,
                         preferred_element_type=F32) + b_ref[...]


def _ada_mod(c_all, w_ada, b_ada):
    depth, d, six_d = w_ada.shape
    bt = c_all.shape[0]
    nt = six_d // d
    return pl.pallas_call(
        _ada_kernel,
        grid=(depth, nt),
        in_specs=[pl.BlockSpec((bt, d), lambda l, j: (0, 0)),
                  pl.BlockSpec((None, d, d), lambda l, j: (l, 0, j)),
                  pl.BlockSpec((None, 1, d), lambda l, j: (l, 0, j))],
        out_specs=pl.BlockSpec((None, bt, d), lambda l, j: (l, 0, j)),
        out_shape=jax.ShapeDtypeStruct((depth, bt, six_d), F32),
        compiler_params=_cparams(("arbitrary", "arbitrary")),
        name="ada_mod",
    )(c_all, w_ada, b_ada.reshape(depth, 1, six_d))


def _rope128(x, cos, sin_signed):
    return x * cos + pltpu.roll(x, LANES // 2, 1) * sin_signed


def _qk_first_head(shape):
    return (lax.broadcasted_iota(jnp.int32, shape, 1) % (LANES // 2)) < HEAD_DIM // 2


def _head_rms(x, gain, lo):
    sq = x * x
    s_lo = jnp.sum(jnp.where(lo, sq, 0.0), axis=-1, keepdims=True)
    s_hi = jnp.sum(jnp.where(lo, 0.0, sq), axis=-1, keepdims=True)
    ms = jnp.where(lo, s_lo, s_hi) * (1.0 / HEAD_DIM)
    return x * lax.rsqrt(ms + EPS) * gain


def _start_expert_row_gather(dest_ref, y_hbm, ybuf, sem, slot):
    def body(t, carry):
        for k in range(2):
            src = y_hbm.at[pl.ds(pl.multiple_of(dest_ref[2 * t + k], SUBLANES), SUBLANES)]
            dst = ybuf.at[slot, pl.ds(pl.multiple_of((2 * t + k) * SUBLANES, SUBLANES), SUBLANES)]
            pltpu.make_async_copy(src, dst, sem.at[slot]).start(priority=k)
        return carry

    lax.fori_loop(0, dest_ref.shape[0] // 2, body, 0, unroll=DMA_UNROLL)


def _gated_pair(dest_ref, dest_next_ref, y_hbm, ybuf, sem, gt):
    i = pl.program_id(0)
    rows = gt.shape[0]
    slot = lax.rem(i, 2)

    @pl.when(i == 0)
    def _():
        _start_expert_row_gather(dest_ref, y_hbm, ybuf, sem, 0)

    @pl.when(i + 1 < pl.num_programs(0))
    def _():
        _start_expert_row_gather(dest_next_ref, y_hbm, ybuf, sem, 1 - slot)

    pltpu.make_async_copy(y_hbm.at[pl.ds(0, 2 * rows * SUBLANES)], ybuf.at[slot], sem.at[slot]).wait()
    y_ref = ybuf.at[slot]
    y0 = _load_token_tiles(y_ref, rows, 0, 2 * SUBLANES)
    y1 = _load_token_tiles(y_ref, rows, SUBLANES, 2 * SUBLANES)
    return gt[:, 0:1] * y0 + gt[:, 1:2] * y1


def _moe_gather_specs(tm, n_tiles):
    in_specs = [pl.BlockSpec((2 * tm,), lambda i: (i,), memory_space=pltpu.SMEM),
                pl.BlockSpec((2 * tm,), lambda i: (jnp.minimum(i + 1, n_tiles - 1),), memory_space=pltpu.SMEM),
                pl.BlockSpec(memory_space=pl.ANY)]
    scratch = [pltpu.VMEM((2, 2 * tm * SUBLANES, LANES), F32), pltpu.SemaphoreType.DMA((2,))]
    return in_specs, scratch


def _rms_mod(x, gain, scale, shift):
    ms = jnp.mean(x * x, axis=-1, keepdims=True)
    return (x * lax.rsqrt(ms + EPS) * gain) * (1.0 + scale) + shift


def _inproj_kernel(with_moe, *refs):
    if with_moe:
        (x_ref, dest_ref, destn_ref, y_hbm, gt_ref, modp_ref, mod_ref, g_ref, w_ref, qn_ref, kn_ref,
         ca_ref, sa_ref, cb_ref, sb_ref,
         xo_ref, qa_ref, ka_ref, va_ref, qb_ref, kb_ref, vb_ref, cp_ref, ybuf, ysem) = refs
    else:
        (x_ref, mod_ref, g_ref, w_ref, qn_ref, kn_ref, ca_ref, sa_ref, cb_ref, sb_ref,
         qa_ref, ka_ref, va_ref, qb_ref, kb_ref, vb_ref, cp_ref) = refs
    tm = x_ref.shape[0]
    if with_moe:
        moe_out = _gated_pair(dest_ref, destn_ref, y_hbm, ybuf, ysem, gt_ref[...])
    qa_scale = (A_QK_DIM ** -0.5) * LOG2E
    qb_scale = (HEAD_DIM ** -0.5) * LOG2E
    sub = min(tm, PROJ_SUBTILE)
    first = _qk_first_head((sub, LANES))
    for r in range(tm // sub):
        rows = slice(r * sub, (r + 1) * sub)
        x = x_ref[rows, :]
        if with_moe:
            x = x + modp_ref[5:6, :] * moe_out[rows, :]
            xo_ref[rows, :] = x
        hb = _rms_mod(x, g_ref[...], mod_ref[1:2, :], mod_ref[0:1, :]).astype(BF16)
        proj = jnp.dot(hb, w_ref[...], preferred_element_type=F32)
        ca, sa, cb, sb = ca_ref[rows, :], sa_ref[rows, :], cb_ref[rows, :], sb_ref[rows, :]
        for i in range(A_WIDTH // LANES):
            sl = slice(i * LANES, (i + 1) * LANES)
            qa_ref[rows, sl] = (_rope128(proj[:, sl], ca, sa) * qa_scale).astype(BF16)
            ka_ref[rows, sl] = _rope128(proj[:, A_WIDTH + i * LANES:A_WIDTH + (i + 1) * LANES], ca, sa).astype(BF16)
        va_ref[rows, :] = proj[:, 2 * A_WIDTH:3 * A_WIDTH].astype(BF16)
        o = 3 * A_WIDTH
        for j in range(B_WIDTH // LANES):
            q = _head_rms(proj[:, o + j * LANES:o + (j + 1) * LANES], qn_ref[...], first)
            qb_ref[rows, j * LANES:(j + 1) * LANES] = (_rope128(q, cb, sb) * qb_scale).astype(BF16)
        o += B_WIDTH
        k = _head_rms(proj[:, o:o + B_KV_WIDTH], kn_ref[...], first)
        kb_ref[rows, :] = _rope128(k, cb, sb).astype(BF16)
        o += B_KV_WIDTH
        vb_ref[rows, :] = proj[:, o:o + B_KV_WIDTH].astype(BF16)
        o += B_KV_WIDTH
        cp_ref[rows, :C_WIDTH] = proj[:, o:o + C_WIDTH]
        cp_ref[rows, C_WIDTH:] = proj[:, o + C_WIDTH:o + 2 * C_WIDTH] * proj[:, o + 2 * C_WIDTH:o + 3 * C_WIDTH]


def _inproj(x, moe, mod_l, g1, w, qn, kn, tabs, seq):
    n, d = x.shape
    tm = TOKEN_TILE
    tiles_per_seq = seq // tm
    row = lambda i: (i, 0)
    const = lambda i: (0, 0)
    pos = lambda i: (i % tiles_per_seq, 0)
    in_specs = [pl.BlockSpec((tm, d), row)]
    args = [x]
    scratch = []
    if moe is not None:
        dest8, y8, gates, mod_prev = moe
        gather_specs, scratch = _moe_gather_specs(tm, n // tm)
        in_specs += gather_specs + [pl.BlockSpec((tm, 2), row),
                                    pl.BlockSpec((None, 6, d), lambda i: (i // tiles_per_seq, 0, 0))]
        args += [dest8, dest8, y8, gates, mod_prev]
    in_specs += [pl.BlockSpec((None, 6, d), lambda i: (i // tiles_per_seq, 0, 0)),
                 pl.BlockSpec((1, d), const),
                 pl.BlockSpec((d, IN_COLS), const),
                 pl.BlockSpec((1, LANES), const), pl.BlockSpec((1, LANES), const)]
    in_specs += [pl.BlockSpec((tm, LANES), pos)] * 4
    args += [mod_l, g1, w, qn, kn, *tabs]
    widths = [(A_WIDTH, BF16), (A_WIDTH, BF16), (A_WIDTH, BF16), (B_WIDTH, BF16), (B_KV_WIDTH, BF16),
              (B_KV_WIDTH, BF16), (2 * C_WIDTH, F32)]
    out_specs = [pl.BlockSpec((tm, wd), row) for wd, _ in widths]
    out_shape = [jax.ShapeDtypeStruct((n, wd), dt) for wd, dt in widths]
    if moe is not None:
        out_specs = [pl.BlockSpec((tm, d), row)] + out_specs
        out_shape = [jax.ShapeDtypeStruct((n, d), F32)] + out_shape
    outs = pl.pallas_call(
        functools.partial(_inproj_kernel, moe is not None),
        grid=(n // tm,),
        in_specs=in_specs, out_specs=out_specs, out_shape=out_shape, scratch_shapes=scratch,
        compiler_params=_cparams(("arbitrary",)),
        name="inproj",
    )(*args)
    if moe is None:
        outs = [x] + list(outs)
    return outs


def _flash(qs, k_ref, v_ref, col, seq):
    rows = qs.shape[0]
    tk = min(SCORE_ELEMS // rows, seq)
    sl = slice(col * LANES, (col + 1) * LANES)

    def body(c, carry):
        m, l, acc = carry
        r0 = pl.multiple_of(c * tk, tk)
        k = k_ref[pl.ds(r0, tk), sl]
        v = v_ref[pl.ds(r0, tk), sl]
        s = lax.dot_general(qs, k, (((1,), (1,)), ((), ())), preferred_element_type=F32)
        m_new = jnp.maximum(m, jnp.max(s, axis=-1, keepdims=True))
        alpha = jnp.exp2(m - m_new)
        p = jnp.exp2(s - m_new)
        l = alpha * l + jnp.sum(p, axis=-1, keepdims=True)
        acc = alpha * acc + jnp.dot(p.astype(BF16), v, preferred_element_type=F32)
        return m_new, l, acc

    init = (jnp.full((rows, 1), -jnp.inf, F32), jnp.zeros((rows, 1), F32), jnp.zeros((rows, LANES), F32))
    _, l, acc = lax.fori_loop(0, seq // tk, body, init, unroll=True)
    return acc, l


def _diff_attn_kernel(lambda_init, seq, q_ref, k_ref, v_ref, lam_ref, g_ref, o_ref):
    lv = lam_ref[...]
    lam = (jnp.exp(jnp.sum(lv[0:1] * lv[1:2], keepdims=True))
           - jnp.exp(jnp.sum(lv[2:3] * lv[3:4], keepdims=True)) + lambda_init)
    tq = q_ref.shape[0]
    lane = lax.broadcasted_iota(jnp.int32, (tq, LANES), 1)
    lo = lane < HEAD_DIM
    for i in range(A_WIDTH // LANES):
        sl = slice(i * LANES, (i + 1) * LANES)
        q = q_ref[:, sl]
        zero = jnp.zeros_like(q)
        slot = (lane % (LANES // 2)) // (A_QK_DIM // 2)
        qs = jnp.concatenate([jnp.where(slot == j, q, zero) for j in range(4)], axis=0)
        acc, l = _flash(qs, k_ref, v_ref, i, seq)
        o = acc / l
        o_even = o[0:tq] - lam * o[tq:2 * tq]
        o_odd = o[2 * tq:3 * tq] - lam * o[3 * tq:4 * tq]
        oo = jnp.where(lo, o_even, o_odd)
        o_ref[:, sl] = (_head_rms(oo, g_ref[...], lo) * (1.0 - lambda_init)).astype(BF16)


def _diff_attn(qa, ka, va, lam_vecs, subln_tiled, lambda_init, batch, seq):
    tq = Q_TILE
    q3, k3, v3 = (a.reshape(batch, seq, A_WIDTH) for a in (qa, ka, va))
    kv_spec = pl.BlockSpec((None, seq, A_WIDTH), lambda b, i: (b, 0, 0))
    out = pl.pallas_call(
        functools.partial(_diff_attn_kernel, lambda_init, seq),
        grid=(batch, seq // tq),
        in_specs=[pl.BlockSpec((None, tq, A_WIDTH), lambda b, i: (b, i, 0)),
                  kv_spec, kv_spec,
                  pl.BlockSpec((4, A_QK_DIM), lambda b, i: (0, 0)),
                  pl.BlockSpec((1, LANES), lambda b, i: (0, 0))],
        out_specs=pl.BlockSpec((None, tq, A_WIDTH), lambda b, i: (b, i, 0)),
        out_shape=jax.ShapeDtypeStruct((batch, seq, A_WIDTH), BF16),
        compiler_params=_cparams(("arbitrary", "arbitrary")),
        name="diff_attn",
    )(q3, k3, v3, lam_vecs, subln_tiled)
    return out.reshape(batch * seq, A_WIDTH)


def _gqa_kernel(seq, q_ref, k_ref, v_ref, o_ref):
    tq = q_ref.shape[0]
    lane = lax.broadcasted_iota(jnp.int32, (tq, LANES), 1)
    lo = lane < HEAD_DIM
    for j in range(B_WIDTH // LANES):
        q = q_ref[:, j * LANES:(j + 1) * LANES]
        zero = jnp.zeros_like(q)
        first = _qk_first_head(q.shape)
        qs = jnp.concatenate([jnp.where(first, q, zero), jnp.where(first, zero, q)], axis=0)
        acc, l = _flash(qs, k_ref, v_ref, 0, seq)
        o = acc / l
        o_ref[:, j * LANES:(j + 1) * LANES] = jnp.where(lo, o[0:tq], o[tq:2 * tq]).astype(BF16)


def _gqa_attn(qb, kb, vb, batch, seq):
    tq = 2 * Q_TILE
    q3 = qb.reshape(batch, seq, B_WIDTH)
    k3 = kb.reshape(batch, seq, B_KV_WIDTH)
    v3 = vb.reshape(batch, seq, B_KV_WIDTH)
    out = pl.pallas_call(
        functools.partial(_gqa_kernel, seq),
        grid=(batch, seq // tq),
        in_specs=[pl.BlockSpec((None, tq, B_WIDTH), lambda b, i: (b, i, 0)),
                  pl.BlockSpec((None, seq, B_KV_WIDTH), lambda b, i: (b, 0, 0)),
                  pl.BlockSpec((None, seq, B_KV_WIDTH), lambda b, i: (b, 0, 0))],
        out_specs=pl.BlockSpec((None, tq, B_WIDTH), lambda b, i: (b, i, 0)),
        out_shape=jax.ShapeDtypeStruct((batch, seq, B_WIDTH), BF16),
        compiler_params=_cparams(("arbitrary", "arbitrary")),
        name="gqa_attn",
    )(q3, k3, v3)
    return out.reshape(batch * seq, B_WIDTH)


def _short_conv(cp_ref, prev_ref, next_ref, w_ref, tile_in_seq, tiles_per_seq):
    tc = cp_ref.shape[0]
    gate = cp_ref[:, :C_WIDTH]
    u = cp_ref[:, C_WIDTH:]
    row = lax.broadcasted_iota(jnp.int32, u.shape, 0)
    before = jnp.where(tile_in_seq == 0, 0.0, prev_ref[SUBLANES - 1:SUBLANES, C_WIDTH:])
    after = jnp.where(tile_in_seq == tiles_per_seq - 1, 0.0, next_ref[0:1, C_WIDTH:])
    u_prev = jnp.where(row == 0, before, pltpu.roll(u, 1, 0))
    u_next = jnp.where(row == tc - 1, after, pltpu.roll(u, tc - 1, 0))
    w = w_ref[...]
    return (gate * (w[0:1] * u_prev + w[1:2] * u + w[2:3] * u_next)).astype(BF16)


def _outproj_kernel(tiles_per_seq, oa_ref, ob_ref, cp_ref, cprev_ref, cnext_ref, cw_ref, x_ref, mod_ref, g_ref,
                    wa_ref, wb_ref, wc_ref, rhi_ref, rlo_ref, rb_ref, tri_ref,
                    x1_ref, h_ref, eid_ref, gate_ref, rank_ref, cnt_ref, run_ref):
    i = pl.program_id(0)
    oc = _short_conv(cp_ref, cprev_ref, cnext_ref, cw_ref, lax.rem(i, tiles_per_seq), tiles_per_seq)
    tm = x_ref.shape[0]

    @pl.when(i == 0)
    def _():
        run_ref[...] = jnp.zeros_like(run_ref)

    sub = tri_ref.shape[0]
    run = run_ref[...]
    for r in range(tm // sub):
        rows = slice(r * sub, (r + 1) * sub)
        mix = (jnp.dot(oa_ref[rows, :], wa_ref[...], preferred_element_type=F32)
               + jnp.dot(ob_ref[rows, :], wb_ref[...], preferred_element_type=F32)
               + jnp.dot(oc[rows, :], wc_ref[...], preferred_element_type=F32))
        x1 = x_ref[rows, :] + mod_ref[2:3, :] * mix
        x1_ref[rows, :] = x1
        h = _rms_mod(x1, g_ref[...], mod_ref[4:5, :], mod_ref[3:4, :])
        _store_token_tiles(h_ref, h, first=r * sub * SUBLANES)

        hi = h.astype(BF16)
        lo = (h - hi.astype(F32)).astype(BF16)
        logits = (jnp.dot(hi, rhi_ref[...], preferred_element_type=F32)
                  + jnp.dot(hi, rlo_ref[...], preferred_element_type=F32)
                  + jnp.dot(lo, rhi_ref[...], preferred_element_type=F32)) + rb_ref[...]
        lt = logits.T
        r8 = lax.broadcasted_iota(jnp.int32, (8, sub), 0)
        neg = -jnp.inf
        gl = jnp.where(r8 < N_GROUPS, lt[0:8], neg)
        gmax = jnp.max(gl, axis=0, keepdims=True)
        gidx = jnp.min(jnp.where(gl == gmax, r8, 8), axis=0, keepdims=True)
        p_sel = 1.0 / jnp.sum(jnp.exp(gl - gmax), axis=0, keepdims=True)
        sel = jnp.zeros((8, sub), F32)
        for g in range(N_GROUPS):
            sel = sel + jnp.where(gidx == g, lt[8 + 8 * g:16 + 8 * g], 0.0)
        v1 = jnp.max(sel, axis=0, keepdims=True)
        i1 = jnp.min(jnp.where(sel == v1, r8, 8), axis=0, keepdims=True)
        sel2 = jnp.where(r8 == i1, neg, sel)
        v2 = jnp.max(sel2, axis=0, keepdims=True)
        i2 = jnp.min(jnp.where(sel2 == v2, r8, 8), axis=0, keepdims=True)
        e = jnp.exp(v2 - v1)
        gate_ref[0:1, rows] = p_sel / (1.0 + e)
        gate_ref[1:2, rows] = p_sel * e / (1.0 + e)
        eid0 = gidx * EXPERTS_PER_GROUP + i1
        eid1 = gidx * EXPERTS_PER_GROUP + i2
        eid_ref[0:1, rows] = eid0
        eid_ref[1:2, rows] = eid1

        r32 = lax.broadcasted_iota(jnp.int32, (N_EXPERTS, sub), 0)
        oh0 = r32 == eid0
        oh1 = r32 == eid1
        cnt = jnp.where(oh0, 1.0, 0.0) + jnp.where(oh1, 1.0, 0.0)
        before = jnp.dot(cnt.astype(BF16), tri_ref[...], preferred_element_type=F32) + run[:, 0:1]
        rank_ref[0:1, rows] = jnp.sum(jnp.where(oh0, before, 0.0), axis=0, keepdims=True).astype(jnp.int32)
        rank_ref[1:2, rows] = jnp.sum(jnp.where(oh1, before, 0.0), axis=0, keepdims=True).astype(jnp.int32)
        run = run + jnp.sum(cnt, axis=1, keepdims=True)
    run_ref[...] = run
    cnt_ref[...] = run


def _outproj(oa, ob, cp, conv_w, x, mod_l, g2, wa, wb, wc, rhi, rlo, rb, tri, seq):
    n, d = x.shape
    tm = TOKEN_TILE
    tiles_per_seq = seq // tm
    halo = tm // SUBLANES
    row = lambda i: (i, 0)
    col = lambda i: (0, i)
    const = lambda i: (0, 0)
    return pl.pallas_call(
        functools.partial(_outproj_kernel, tiles_per_seq),
        grid=(n // tm,),
        in_specs=[pl.BlockSpec((tm, A_WIDTH), row), pl.BlockSpec((tm, B_WIDTH), row),
                  pl.BlockSpec((tm, 2 * C_WIDTH), row),
                  pl.BlockSpec((SUBLANES, 2 * C_WIDTH), lambda i: (jnp.maximum(i * halo - 1, 0), 0)),
                  pl.BlockSpec((SUBLANES, 2 * C_WIDTH),
                               lambda i: (jnp.minimum((i + 1) * halo, n // SUBLANES - 1), 0)),
                  pl.BlockSpec((3, C_WIDTH), const), pl.BlockSpec((tm, d), row),
                  pl.BlockSpec((None, 6, d), lambda i: (i // tiles_per_seq, 0, 0)),
                  pl.BlockSpec((1, d), const),
                  pl.BlockSpec((A_WIDTH, d), const), pl.BlockSpec((B_WIDTH, d), const),
                  pl.BlockSpec((C_WIDTH, d), const),
                  pl.BlockSpec((d, LANES), const), pl.BlockSpec((d, LANES), const),
                  pl.BlockSpec((1, LANES), const), pl.BlockSpec(tri.shape, const)],
        out_specs=[pl.BlockSpec((tm, d), row), pl.BlockSpec((tm * SUBLANES, LANES), row),
                   pl.BlockSpec((2, tm), col), pl.BlockSpec((2, tm), col), pl.BlockSpec((2, tm), col),
                   pl.BlockSpec((N_EXPERTS, LANES), const)],
        out_shape=[jax.ShapeDtypeStruct((n, d), F32), jax.ShapeDtypeStruct((n * SUBLANES, LANES), F32),
                   jax.ShapeDtypeStruct((2, n), jnp.int32), jax.ShapeDtypeStruct((2, n), F32),
                   jax.ShapeDtypeStruct((2, n), jnp.int32),
                   jax.ShapeDtypeStruct((N_EXPERTS, LANES), F32)],
        scratch_shapes=[pltpu.VMEM((N_EXPERTS, LANES), F32)],
        compiler_params=_cparams(("arbitrary",)),
        name="outproj_router",
    )(oa, ob, cp, cp, cp, conv_w, x, mod_l, g2, wa, wb, wc, rhi, rlo, rb, tri)


def _dispatch_kernel(dest_ref, h_ref, buf_in_ref, buf_ref, sem):
    del buf_in_ref
    td = dest_ref.shape[0] // 2

    def body(t, carry):
        src = h_ref.at[pl.ds(pl.multiple_of(t * SUBLANES, SUBLANES), SUBLANES)]
        for k in range(2):
            dst = buf_ref.at[pl.ds(pl.multiple_of(dest_ref[2 * t + k], SUBLANES), SUBLANES)]
            pltpu.make_async_copy(src, dst, sem).start(priority=k)
        return carry

    lax.fori_loop(0, td, body, 0, unroll=DMA_UNROLL)
    for _ in range(2):
        pltpu.make_async_copy(h_ref, buf_ref.at[pl.ds(0, td * SUBLANES)], sem).wait()


def _dispatch(dest8, h8, buf_init):
    n = dest8.shape[0] // 2
    td = DMA_TILE
    return pl.pallas_call(
        _dispatch_kernel,
        grid=(n // td,),
        in_specs=[pl.BlockSpec((2 * td,), lambda i: (i,), memory_space=pltpu.SMEM),
                  pl.BlockSpec((td * SUBLANES, LANES), lambda i: (i, 0)),
                  pl.BlockSpec(memory_space=pl.ANY)],
        out_specs=pl.BlockSpec(memory_space=pl.ANY),
        out_shape=jax.ShapeDtypeStruct(buf_init.shape, F32),
        scratch_shapes=[pltpu.SemaphoreType.DMA(())],
        input_output_aliases={2: 0},
        compiler_params=pltpu.CompilerParams(dimension_semantics=("arbitrary",), has_side_effects=True),
        name="moe_dispatch",
    )(dest8, h8, buf_init)


def _expert_kernel(layer, be_ref, nxt_ref, nu_ref, x_ref, w1_hbm, w3_hbm, w2_hbm, y_ref,
                   w1f, w3f, w2f, w1b, w3b, w2b, wsem, slot_ref):
    i = pl.program_id(0)
    rb = x_ref.shape[0] // SUBLANES
    e = be_ref[i]
    active = i < nu_ref[0]
    changed = jnp.logical_and(active, jnp.logical_or(i == 0, e != be_ref[jnp.maximum(i - 1, 0)]))

    def weight_copies(expert, slot):
        return [pltpu.make_async_copy(src.at[layer, expert], dst.at[slot], wsem.at[slot])
                for src, dst in ((w1_hbm, w1f), (w3_hbm, w3f), (w2_hbm, w2f))]

    @pl.when(i == 0)
    def _():
        slot_ref[0] = 0
        for c in weight_copies(e, 0):
            c.start()

    @pl.when(changed)
    def _():
        slot = slot_ref[0]
        for c in weight_copies(e, slot):
            c.wait()
        w1b[...] = w1f[slot].astype(BF16)
        w3b[...] = w3f[slot].astype(BF16)
        w2b[...] = w2f[slot].astype(BF16)
        nxt = nxt_ref[i]

        @pl.when(nxt != e)
        def _():
            for c in weight_copies(nxt, 1 - slot):
                c.start()

        slot_ref[0] = 1 - slot

    @pl.when(active)
    def _():
        x = _load_token_tiles(x_ref, rb).astype(BF16)
        de = w1b.shape[1]
        y = None
        for c in range(0, de, EXPERT_COL_CHUNK):
            cols = slice(c, c + EXPERT_COL_CHUNK)
            a = jnp.dot(x, w1b[:, cols], preferred_element_type=F32)
            b = jnp.dot(x, w3b[:, cols], preferred_element_type=F32)
            act = (a * jax.nn.sigmoid(a) * b).astype(BF16)
            part = jnp.dot(act, w2b[cols, :], preferred_element_type=F32)
            y = part if y is None else y + part
        _store_token_tiles(y_ref, y)

    @pl.when(i >= nu_ref[0])
    def _():
        y_ref[...] = jnp.zeros_like(y_ref)


def _experts(block_e, next_e, n_used, buf8, w1, w3, w2, layer, cap):
    d = D_MODEL
    rb = EXPERT_ROWS
    de = w1.shape[-1]
    hbm = pl.BlockSpec(memory_space=pl.ANY)
    grid_spec = pltpu.PrefetchScalarGridSpec(
        num_scalar_prefetch=3,
        grid=(cap // rb,),
        in_specs=[pl.BlockSpec((rb * SUBLANES, LANES), lambda i, be, nx, nu: (i, 0)), hbm, hbm, hbm],
        out_specs=pl.BlockSpec((rb * SUBLANES, LANES), lambda i, be, nx, nu: (i, 0)),
        scratch_shapes=[pltpu.VMEM((2, d, de), F32), pltpu.VMEM((2, d, de), F32), pltpu.VMEM((2, de, d), F32),
                        pltpu.VMEM((d, de), BF16), pltpu.VMEM((d, de), BF16), pltpu.VMEM((de, d), BF16),
                        pltpu.SemaphoreType.DMA((2,)), pltpu.SMEM((1,), jnp.int32)],
    )
    return pl.pallas_call(
        functools.partial(_expert_kernel, layer),
        grid_spec=grid_spec,
        out_shape=jax.ShapeDtypeStruct((cap * SUBLANES, LANES), F32),
        compiler_params=_cparams(("arbitrary",)),
        name="moe_experts",
    )(block_e, next_e, n_used, buf8, w1, w3, w2)


def _final_kernel(x_ref, dest_ref, destn_ref, y_hbm, gt_ref, mod_ref, g_ref, o_ref, ybuf, ysem):
    x = x_ref[...] + mod_ref[5:6, :] * _gated_pair(dest_ref, destn_ref, y_hbm, ybuf, ysem, gt_ref[...])
    ms = jnp.mean(x * x, axis=-1, keepdims=True)
    o_ref[...] = x * lax.rsqrt(ms + EPS) * g_ref[...]


def _final(x1, dest8, y8, gates, mod_l, g, seq):
    n, d = x1.shape
    tm = TOKEN_TILE
    tiles_per_seq = seq // tm
    row = lambda i: (i, 0)
    gather_specs, scratch = _moe_gather_specs(tm, n // tm)
    return pl.pallas_call(
        _final_kernel,
        grid=(n // tm,),
        in_specs=[pl.BlockSpec((tm, d), row)] + gather_specs + [
            pl.BlockSpec((tm, 2), row),
            pl.BlockSpec((None, 6, d), lambda i: (i // tiles_per_seq, 0, 0)),
            pl.BlockSpec((1, d), lambda i: (0, 0))],
        out_specs=pl.BlockSpec((tm, d), row),
        out_shape=jax.ShapeDtypeStruct((n, d), F32),
        scratch_shapes=scratch,
        compiler_params=_cparams(("arbitrary",)),
        name="final_norm",
    )(x1, dest8, dest8, y8, gates, mod_l, g)


def _rope_tables(seq):
    freqs = ROPE_THETA ** (-np.arange(0, A_QK_DIM, 2, dtype=np.float32) / A_QK_DIM)
    lane = np.arange(LANES)
    f = freqs[lane % 16][None, :]
    sign = np.where(lane < LANES // 2, -1.0, 1.0)[None, :].astype(np.float32)
    pos = np.arange(seq, dtype=np.float32)[:, None]
    ang_a = pos * f
    axial = np.where((lane % 32) < 16, np.floor(pos / GRID_W), np.mod(pos, GRID_W)).astype(np.float32)
    ang_b = axial * f
    tabs = (np.cos(ang_a), np.sin(ang_a) * sign, np.cos(ang_b), np.sin(ang_b) * sign)
    return tuple(jnp.asarray(t, dtype=F32) for t in tabs)


def _qb_perm():
    cols = []
    for j in range(B_HEADS // B_KV_HEADS):
        cols += list(range(j * HEAD_DIM, (j + 1) * HEAD_DIM))
        cols += list(range((4 + j) * HEAD_DIM, (5 + j) * HEAD_DIM))
    return np.asarray(cols, dtype=np.int32)


def _split_half_perms():
    lane = np.arange(LANES)
    half, f = lane // 64, lane % 16
    slot = (lane % 64) // 16
    perm_a = slot * 32 + half * 16 + f
    head, part = (lane % 64) // 32, (lane % 32) // 16
    perm_b = head * 64 + part * 32 + half * 16 + f
    return perm_a.astype(np.int32), perm_b.astype(np.int32)


def _trunk(x, mod, batch, seq, p, spare_buf=None):
    n = batch * seq
    d = D_MODEL
    depth = p["w_in"].shape[0]
    tabs = _rope_tables(seq)
    perm = _qb_perm()
    perm_a, perm_b = _split_half_perms()
    sub = min(TOKEN_TILE, ROUTER_SUBTILE)
    tri = jnp.asarray(np.triu(np.ones((sub, sub), np.float32), 1), dtype=BF16)
    rb = EXPERT_ROWS
    cap = (2 * n + N_EXPERTS * (rb - 1) + rb - 1) // rb * rb
    nb = cap // rb
    x = x.reshape(n, d)
    moe = None
    buf8 = spare_buf if spare_buf is not None and spare_buf.shape[0] >= cap * SUBLANES else None
    for l in range(depth):
        lambda_init = 0.8 - 0.6 * math.exp(-0.3 * l)
        mod_l = mod[l].reshape(batch, 6, d)
        cols = np.arange(IN_COLS, dtype=np.int32)
        for blk in range(2 * A_WIDTH // LANES):
            cols[blk * LANES:(blk + 1) * LANES] = blk * LANES + perm_a
        qb0 = 3 * A_WIDTH
        for blk in range(B_WIDTH // LANES):
            cols[qb0 + blk * LANES:qb0 + (blk + 1) * LANES] = qb0 + perm[blk * LANES + perm_b]
        cols[qb0 + B_WIDTH:qb0 + B_WIDTH + B_KV_WIDTH] = qb0 + B_WIDTH + perm_b
        w_in = p["w_in"][l][:, cols].astype(BF16)
        tile2 = lambda v: jnp.tile(v.reshape(1, HEAD_DIM), (1, 2))
        x, qa, ka, va, qb, kb, vb, cp = _inproj(x, moe, mod_l, p["norm1"][l].reshape(1, d), w_in,
                                                tile2(p["q_norm"][l])[:, perm_b], tile2(p["k_norm"][l])[:, perm_b],
                                                tabs, seq)
        oa = _diff_attn(qa, ka, va, p["diff_lambda"][l], tile2(p["diff_subln"][l]), lambda_init, batch, seq)
        ob = _gqa_attn(qb, kb, vb, batch, seq)
        w_out = p["w_out"][l]
        wa = w_out[:A_WIDTH].astype(BF16)
        wb = w_out[A_WIDTH + perm].astype(BF16)
        wc = w_out[A_WIDTH + B_WIDTH:].astype(BF16)
        wr = jnp.zeros((d, LANES), F32)
        wr = wr.at[:, :N_GROUPS].set(p["router_group_w"][l]).at[:, 8:8 + N_EXPERTS].set(p["router_expert_w"][l])
        rhi = wr.astype(BF16)
        rlo = (wr - rhi.astype(F32)).astype(BF16)
        rbias = jnp.zeros((1, LANES), F32)
        rbias = rbias.at[0, :N_GROUPS].set(p["router_group_b"][l]).at[0, 8:8 + N_EXPERTS].set(p["router_expert_b"][l])
        x1, h8, eid, gate, rank, cnt = _outproj(oa, ob, cp, p["conv_w"][l], x, mod_l, p["norm2"][l].reshape(1, d),
                                                wa, wb, wc, rhi, rlo, rbias, tri, seq)
        counts = cnt[:, 0].astype(jnp.int32)
        expert_range = jnp.arange(N_EXPERTS, dtype=jnp.int32)
        pcounts = (counts + rb - 1) // rb * rb
        pends = jnp.cumsum(pcounts)
        pstarts = pends - pcounts
        block_row = jnp.arange(nb, dtype=jnp.int32)[:, None] * rb
        block_e = jnp.minimum(jnp.sum((pends[None, :] <= block_row).astype(jnp.int32), axis=1), N_EXPERTS - 1)
        n_used = (pends[-1:] // rb).astype(jnp.int32)
        later = (expert_range[None, :] > expert_range[:, None]) & (pcounts[None, :] > 0)
        next_owner = jnp.min(jnp.where(later, expert_range[None, :], N_EXPERTS), axis=1)
        next_owner = jnp.where(next_owner == N_EXPERTS, expert_range, next_owner)
        next_e = jnp.sum(jnp.where(block_e[:, None] == expert_range[None, :], next_owner[None, :], 0), axis=1)
        expert_ids = jnp.arange(N_EXPERTS, dtype=jnp.int32)[:, None, None]
        dest8 = (jnp.sum(jnp.where(eid[None] == expert_ids, pstarts[:, None, None], 0), axis=0) + rank) * SUBLANES
        dest8 = dest8.T.reshape(2 * n)
        if buf8 is None:
            buf8 = jnp.zeros((cap * SUBLANES, LANES), F32)
        buf8 = _dispatch(dest8, h8, buf8)
        y8 = _experts(block_e, next_e.astype(jnp.int32), n_used, buf8,
                      p["expert_w1"], p["expert_w3"], p["expert_w2"], l, cap)
        x = x1
        moe = (dest8, y8, gate.T, mod_l)
    out = _final(x, *moe, p["final_norm"].reshape(1, d), seq)
    return out.reshape(batch, seq, d), buf8


def kernel(x_prompt, x_sample, c_prompt, c_sample, w_ada, b_ada, norm1, norm2, w_in, w_out, diff_lambda,
           diff_subln, q_norm, k_norm, conv_w, router_group_w, router_group_b, router_expert_w,
           router_expert_b, expert_w1, expert_w3, expert_w2, final_norm):
    p = dict(norm1=norm1, norm2=norm2, w_in=w_in, w_out=w_out, diff_lambda=diff_lambda, diff_subln=diff_subln,
             q_norm=q_norm, k_norm=k_norm, conv_w=conv_w, router_group_w=router_group_w,
             router_group_b=router_group_b, router_expert_w=router_expert_w, router_expert_b=router_expert_b,
             expert_w1=expert_w1, expert_w3=expert_w3, expert_w2=expert_w2, final_norm=final_norm)
    bp, sp, _ = x_prompt.shape
    bs, ss, _ = x_sample.shape
    mod = _ada_mod(jnp.concatenate([c_prompt, c_sample], axis=0), w_ada, b_ada)
    y_prompt, spare = _trunk(x_prompt, mod[:, :bp], bp, sp, p)
    y_sample, _ = _trunk(x_sample, mod[:, bp:], bs, ss, p, spare)
    return (y_prompt, y_sample)
```

```python
import functools
import math

import numpy as np
import jax
import jax.numpy as jnp
from jax import lax
from jax.experimental import pallas as pl
from jax.experimental.pallas import tpu as pltpu

F32 = jnp.float32
BF16 = jnp.bfloat16

D_MODEL = 1024
HEAD_DIM = 64
A_HEADS = 4
A_QK_DIM = 32
A_WIDTH = 256
B_HEADS = 8
B_KV_HEADS = 2
B_WIDTH = 512
B_KV_WIDTH = 128
C_WIDTH = 256
IN_COLS = 2304
N_GROUPS = 4
EXPERTS_PER_GROUP = 8
N_EXPERTS = 32
D_EXPERT = 512
GRID_W = 64
ROPE_THETA = 10000.0
EPS = 1e-6
LANES = 128
LOG2E = 1.4426950408889634

VMEM_LIMIT = 56 * 1024 * 1024

TOKEN_TILE = 512
PROJ_SUBTILE = 256
ROUTER_SUBTILE = 512
Q_TILE = 256
SCORE_ELEMS = 2 * 1024 * 1024
EXPERT_ROWS = 512
EXPERT_COL_CHUNK = 256
DMA_TILE = 1024
DMA_UNROLL = 8


def _cparams(sem):
    return pltpu.CompilerParams(dimension_semantics=sem, vmem_limit_bytes=VMEM_LIMIT)


SUBLANES = 8
LANE_BLOCKS = D_MODEL // LANES
assert LANE_BLOCKS == SUBLANES


def _store_token_tiles(ref, val, first=0, per_token=SUBLANES):
    rows = val.shape[0]
    for c in range(LANE_BLOCKS):
        ref[pl.ds(first + c, rows, stride=per_token), :] = val[:, c * LANES:(c + 1) * LANES]


def _load_token_tiles(ref, rows, first=0, per_token=SUBLANES):
    return jnp.concatenate([ref[pl.ds(first + c, rows, stride=per_token), :] for c in range(LANE_BLOCKS)], axis=1)


def _ada_kernel(c_ref, w_ref, b_ref, o_ref):
    c = c_ref[...]
    s = c * jax.nn.sigmoid(c)
    o_ref[...] = jnp.dot(s, w_ref[...], precision=lax.Precision.HIGHEST,
                         preferred_element_type=F32) + b_ref[...]


def _ada_mod(c_all, w_ada, b_ada):
    depth, d, six_d = w_ada.shape
    bt = c_all.shape[0]
    nt = six_d // d
    return pl.pallas_call(
        _ada_kernel,
        grid=(depth, nt),
        in_specs=[pl.BlockSpec((bt, d), lambda l, j: (0, 0)),
                  pl.BlockSpec((None, d, d), lambda l, j: (l, 0, j)),
                  pl.BlockSpec((None, 1, d), lambda l, j: (l, 0, j))],
        out_specs=pl.BlockSpec((None, bt, d), lambda l, j: (l, 0, j)),
        out_shape=jax.ShapeDtypeStruct((depth, bt, six_d), F32),
        compiler_params=_cparams(("arbitrary", "arbitrary")),
        name="ada_mod",
    )(c_all, w_ada, b_ada.reshape(depth, 1, six_d))


def _rope128(x, cos, sin_signed):
    return x * cos + pltpu.roll(x, LANES // 2, 1) * sin_signed


def _qk_first_head(shape):
    return (lax.broadcasted_iota(jnp.int32, shape, 1) % (LANES // 2)) < HEAD_DIM // 2


def _head_rms(x, gain, lo):
    sq = x * x
    s_lo = jnp.sum(jnp.where(lo, sq, 0.0), axis=-1, keepdims=True)
    s_hi = jnp.sum(jnp.where(lo, 0.0, sq), axis=-1, keepdims=True)
    ms = jnp.where(lo, s_lo, s_hi) * (1.0 / HEAD_DIM)
    return x * lax.rsqrt(ms + EPS) * gain


def _start_expert_row_gather(dest_ref, y_hbm, ybuf, sem, slot):
    def body(t, carry):
        for k in range(2):
            src = y_hbm.at[pl.ds(pl.multiple_of(dest_ref[2 * t + k], SUBLANES), SUBLANES)]
            dst = ybuf.at[slot, pl.ds(pl.multiple_of((2 * t + k) * SUBLANES, SUBLANES), SUBLANES)]
            pltpu.make_async_copy(src, dst, sem.at[slot]).start(priority=k)
        return carry

    lax.fori_loop(0, dest_ref.shape[0] // 2, body, 0, unroll=DMA_UNROLL)


def _gated_pair(dest_ref, dest_next_ref, y_hbm, ybuf, sem, gt):
    i = pl.program_id(0)
    rows = gt.shape[0]
    slot = lax.rem(i, 2)

    @pl.when(i == 0)
    def _():
        _start_expert_row_gather(dest_ref, y_hbm, ybuf, sem, 0)

    @pl.when(i + 1 < pl.num_programs(0))
    def _():
        _start_expert_row_gather(dest_next_ref, y_hbm, ybuf, sem, 1 - slot)

    pltpu.make_async_copy(y_hbm.at[pl.ds(0, 2 * rows * SUBLANES)], ybuf.at[slot], sem.at[slot]).wait()
    y_ref = ybuf.at[slot]
    y0 = _load_token_tiles(y_ref, rows, 0, 2 * SUBLANES)
    y1 = _load_token_tiles(y_ref, rows, SUBLANES, 2 * SUBLANES)
    return gt[:, 0:1] * y0 + gt[:, 1:2] * y1


def _moe_gather_specs(tm, n_tiles):
    in_specs = [pl.BlockSpec((2 * tm,), lambda i: (i,), memory_space=pltpu.SMEM),
                pl.BlockSpec((2 * tm,), lambda i: (jnp.minimum(i + 1, n_tiles - 1),), memory_space=pltpu.SMEM),
                pl.BlockSpec(memory_space=pl.ANY)]
    scratch = [pltpu.VMEM((2, 2 * tm * SUBLANES, LANES), F32), pltpu.SemaphoreType.DMA((2,))]
    return in_specs, scratch


def _rms_mod(x, gain, scale, shift):
    ms = jnp.mean(x * x, axis=-1, keepdims=True)
    return (x * lax.rsqrt(ms + EPS) * gain) * (1.0 + scale) + shift


def _inproj_kernel(with_moe, *refs):
    if with_moe:
        (x_ref, dest_ref, destn_ref, y_hbm, gt_ref, modp_ref, mod_ref, g_ref, w_ref, qn_ref, kn_ref,
         ca_ref, sa_ref, cb_ref, sb_ref,
         xo_ref, qa_ref, ka_ref, va_ref, qb_ref, kb_ref, vb_ref, cp_ref, ybuf, ysem) = refs
    else:
        (x_ref, mod_ref, g_ref, w_ref, qn_ref, kn_ref, ca_ref, sa_ref, cb_ref, sb_ref,
         qa_ref, ka_ref, va_ref, qb_ref, kb_ref, vb_ref, cp_ref) = refs
    tm = x_ref.shape[0]
    if with_moe:
        moe_out = _gated_pair(dest_ref, destn_ref, y_hbm, ybuf, ysem, gt_ref[...])
    qa_scale = (A_QK_DIM ** -0.5) * LOG2E
    qb_scale = (HEAD_DIM ** -0.5) * LOG2E
    sub = min(tm, PROJ_SUBTILE)
    first = _qk_first_head((sub, LANES))
    for r in range(tm // sub):
        rows = slice(r * sub, (r + 1) * sub)
        x = x_ref[rows, :]
        if with_moe:
            x = x + modp_ref[5:6, :] * moe_out[rows, :]
            xo_ref[rows, :] = x
        hb = _rms_mod(x, g_ref[...], mod_ref[1:2, :], mod_ref[0:1, :]).astype(BF16)
        proj = jnp.dot(hb, w_ref[...], preferred_element_type=F32)
        ca, sa, cb, sb = ca_ref[rows, :], sa_ref[rows, :], cb_ref[rows, :], sb_ref[rows, :]
        for i in range(A_WIDTH // LANES):
            sl = slice(i * LANES, (i + 1) * LANES)
            qa_ref[rows, sl] = (_rope128(proj[:, sl], ca, sa) * qa_scale).astype(BF16)
            ka_ref[rows, sl] = _rope128(proj[:, A_WIDTH + i * LANES:A_WIDTH + (i + 1) * LANES], ca, sa).astype(BF16)
        va_ref[rows, :] = proj[:, 2 * A_WIDTH:3 * A_WIDTH].astype(BF16)
        o = 3 * A_WIDTH
        for j in range(B_WIDTH // LANES):
            q = _head_rms(proj[:, o + j * LANES:o + (j + 1) * LANES], qn_ref[...], first)
            qb_ref[rows, j * LANES:(j + 1) * LANES] = (_rope128(q, cb, sb) * qb_scale).astype(BF16)
        o += B_WIDTH
        k = _head_rms(proj[:, o:o + B_KV_WIDTH], kn_ref[...], first)
        kb_ref[rows, :] = _rope128(k, cb, sb).astype(BF16)
        o += B_KV_WIDTH
        vb_ref[rows, :] = proj[:, o:o + B_KV_WIDTH].astype(BF16)
        o += B_KV_WIDTH
        cp_ref[rows, :C_WIDTH] = proj[:, o:o + C_WIDTH]
        cp_ref[rows, C_WIDTH:] = proj[:, o + C_WIDTH:o + 2 * C_WIDTH] * proj[:, o + 2 * C_WIDTH:o + 3 * C_WIDTH]


def _inproj(x, moe, mod_l, g1, w, qn, kn, tabs, seq):
    n, d = x.shape
    tm = TOKEN_TILE
    tiles_per_seq = seq // tm
    row = lambda i: (i, 0)
    const = lambda i: (0, 0)
    pos = lambda i: (i % tiles_per_seq, 0)
    in_specs = [pl.BlockSpec((tm, d), row)]
    args = [x]
    scratch = []
    if moe is not None:
        dest8, y8, gates, mod_prev = moe
        gather_specs, scratch = _moe_gather_specs(tm, n // tm)
        in_specs += gather_specs + [pl.BlockSpec((tm, 2), row),
                                    pl.BlockSpec((None, 6, d), lambda i: (i // tiles_per_seq, 0, 0))]
        args += [dest8, dest8, y8, gates, mod_prev]
    in_specs += [pl.BlockSpec((None, 6, d), lambda i: (i // tiles_per_seq, 0, 0)),
                 pl.BlockSpec((1, d), const),
                 pl.BlockSpec((d, IN_COLS), const),
                 pl.BlockSpec((1, LANES), const), pl.BlockSpec((1, LANES), const)]
    in_specs += [pl.BlockSpec((tm, LANES), pos)] * 4
    args += [mod_l, g1, w, qn, kn, *tabs]
    widths = [(A_WIDTH, BF16), (A_WIDTH, BF16), (A_WIDTH, BF16), (B_WIDTH, BF16), (B_KV_WIDTH, BF16),
              (B_KV_WIDTH, BF16), (2 * C_WIDTH, F32)]
    out_specs = [pl.BlockSpec((tm, wd), row) for wd, _ in widths]
    out_shape = [jax.ShapeDtypeStruct((n, wd), dt) for wd, dt in widths]
    if moe is not None:
        out_specs = [pl.BlockSpec((tm, d), row)] + out_specs
        out_shape = [jax.ShapeDtypeStruct((n, d), F32)] + out_shape
    outs = pl.pallas_call(
        functools.partial(_inproj_kernel, moe is not None),
        grid=(n // tm,),
        in_specs=in_specs, out_specs=out_specs, out_shape=out_shape, scratch_shapes=scratch,
        compiler_params=_cparams(("arbitrary",)),
        name="inproj",
    )(*args)
    if moe is None:
        outs = [x] + list(outs)
    return outs


def _flash(qs, k_ref, v_ref, col, seq):
    rows = qs.shape[0]
    tk = min(SCORE_ELEMS // rows, seq)
    sl = slice(col * LANES, (col + 1) * LANES)

    def body(c, carry):
        m, l, acc = carry
        r0 = pl.multiple_of(c * tk, tk)
        k = k_ref[pl.ds(r0, tk), sl]
        v = v_ref[pl.ds(r0, tk), sl]
        s = lax.dot_general(qs, k, (((1,), (1,)), ((), ())), preferred_element_type=F32)
        m_new = jnp.maximum(m, jnp.max(s, axis=-1, keepdims=True))
        alpha = jnp.exp2(m - m_new)
        p = jnp.exp2(s - m_new)
        l = alpha * l + jnp.sum(p, axis=-1, keepdims=True)
        acc = alpha * acc + jnp.dot(p.astype(BF16), v, preferred_element_type=F32)
        return m_new, l, acc

    init = (jnp.full((rows, 1), -jnp.inf, F32), jnp.zeros((rows, 1), F32), jnp.zeros((rows, LANES), F32))
    _, l, acc = lax.fori_loop(0, seq // tk, body, init, unroll=True)
    return acc, l


def _diff_attn_kernel(lambda_init, seq, q_ref, k_ref, v_ref, lam_ref, g_ref, o_ref):
    lv = lam_ref[...]
    lam = (jnp.exp(jnp.sum(lv[0:1] * lv[1:2], keepdims=True))
           - jnp.exp(jnp.sum(lv[2:3] * lv[3:4], keepdims=True)) + lambda_init)
    tq = q_ref.shape[0]
    lane = lax.broadcasted_iota(jnp.int32, (tq, LANES), 1)
    lo = lane < HEAD_DIM
    for i in range(A_WIDTH // LANES):
        sl = slice(i * LANES, (i + 1) * LANES)
        q = q_ref[:, sl]
        zero = jnp.zeros_like(q)
        slot = (lane % (LANES // 2)) // (A_QK_DIM // 2)
        qs = jnp.concatenate([jnp.where(slot == j, q, zero) for j in range(4)], axis=0)
        acc, l = _flash(qs, k_ref, v_ref, i, seq)
        o = acc / l
        o_even = o[0:tq] - lam * o[tq:2 * tq]
        o_odd = o[2 * tq:3 * tq] - lam * o[3 * tq:4 * tq]
        oo = jnp.where(lo, o_even, o_odd)
        o_ref[:, sl] = (_head_rms(oo, g_ref[...], lo) * (1.0 - lambda_init)).astype(BF16)


def _diff_attn(qa, ka, va, lam_vecs, subln_tiled, lambda_init, batch, seq):
    tq = Q_TILE
    q3, k3, v3 = (a.reshape(batch, seq, A_WIDTH) for a in (qa, ka, va))
    kv_spec = pl.BlockSpec((None, seq, A_WIDTH), lambda b, i: (b, 0, 0))
    out = pl.pallas_call(
        functools.partial(_diff_attn_kernel, lambda_init, seq),
        grid=(batch, seq // tq),
        in_specs=[pl.BlockSpec((None, tq, A_WIDTH), lambda b, i: (b, i, 0)),
                  kv_spec, kv_spec,
                  pl.BlockSpec((4, A_QK_DIM), lambda b, i: (0, 0)),
                  pl.BlockSpec((1, LANES), lambda b, i: (0, 0))],
        out_specs=pl.BlockSpec((None, tq, A_WIDTH), lambda b, i: (b, i, 0)),
        out_shape=jax.ShapeDtypeStruct((batch, seq, A_WIDTH), BF16),
        compiler_params=_cparams(("arbitrary", "arbitrary")),
        name="diff_attn",
    )(q3, k3, v3, lam_vecs, subln_tiled)
    return out.reshape(batch * seq, A_WIDTH)


def _gqa_kernel(seq, q_ref, k_ref, v_ref, o_ref):
    tq = q_ref.shape[0]
    lane = lax.broadcasted_iota(jnp.int32, (tq, LANES), 1)
    lo = lane < HEAD_DIM
    for j in range(B_WIDTH // LANES):
        q = q_ref[:, j * LANES:(j + 1) * LANES]
        zero = jnp.zeros_like(q)
        first = _qk_first_head(q.shape)
        qs = jnp.concatenate([jnp.where(first, q, zero), jnp.where(first, zero, q)], axis=0)
        acc, l = _flash(qs, k_ref, v_ref, 0, seq)
        o = acc / l
        o_ref[:, j * LANES:(j + 1) * LANES] = jnp.where(lo, o[0:tq], o[tq:2 * tq]).astype(BF16)


def _gqa_attn(qb, kb, vb, batch, seq):
    tq = 2 * Q_TILE
    q3 = qb.reshape(batch, seq, B_WIDTH)
    k3 = kb.reshape(batch, seq, B_KV_WIDTH)
    v3 = vb.reshape(batch, seq, B_KV_WIDTH)
    out = pl.pallas_call(
        functools.partial(_gqa_kernel, seq),
        grid=(batch, seq // tq),
        in_specs=[pl.BlockSpec((None, tq, B_WIDTH), lambda b, i: (b, i, 0)),
                  pl.BlockSpec((None, seq, B_KV_WIDTH), lambda b, i: (b, 0, 0)),
                  pl.BlockSpec((None, seq, B_KV_WIDTH), lambda b, i: (b, 0, 0))],
        out_specs=pl.BlockSpec((None, tq, B_WIDTH), lambda b, i: (b, i, 0)),
        out_shape=jax.ShapeDtypeStruct((batch, seq, B_WIDTH), BF16),
        compiler_params=_cparams(("arbitrary", "arbitrary")),
        name="gqa_attn",
    )(q3, k3, v3)
    return out.reshape(batch * seq, B_WIDTH)


def _short_conv(cp_ref, prev_ref, next_ref, w_ref, tile_in_seq, tiles_per_seq):
    tc = cp_ref.shape[0]
    gate = cp_ref[:, :C_WIDTH]
    u = cp_ref[:, C_WIDTH:]
    row = lax.broadcasted_iota(jnp.int32, u.shape, 0)
    before = jnp.where(tile_in_seq == 0, 0.0, prev_ref[SUBLANES - 1:SUBLANES, C_WIDTH:])
    after = jnp.where(tile_in_seq == tiles_per_seq - 1, 0.0, next_ref[0:1, C_WIDTH:])
    u_prev = jnp.where(row == 0, before, pltpu.roll(u, 1, 0))
    u_next = jnp.where(row == tc - 1, after, pltpu.roll(u, tc - 1, 0))
    w = w_ref[...]
    return (gate * (w[0:1] * u_prev + w[1:2] * u + w[2:3] * u_next)).astype(BF16)


def _outproj_kernel(tiles_per_seq, oa_ref, ob_ref, cp_ref, cprev_ref, cnext_ref, cw_ref, x_ref, mod_ref, g_ref,
                    wa_ref, wb_ref, wc_ref, rhi_ref, rlo_ref, rb_ref, tri_ref,
                    x1_ref, h_ref, eid_ref, gate_ref, rank_ref, cnt_ref, run_ref):
    i = pl.program_id(0)
    tm = x_ref.shape[0]

    @pl.when(i == 0)
    def _():
        run_ref[...] = jnp.zeros_like(run_ref)

    sub = tri_ref.shape[0]
    run = run_ref[...]
    for r in range(tm // sub):
        rows = slice(r * sub, (r + 1) * sub)
        mix = (jnp.dot(oa_ref[rows, :], wa_ref[...], preferred_element_type=F32)
               + jnp.dot(ob_ref[rows, :], wb_ref[...], preferred_element_type=F32))
        if r == 0:
            oc = _short_conv(cp_ref, cprev_ref, cnext_ref, cw_ref, lax.rem(i, tiles_per_seq), tiles_per_seq)
        mix = mix + jnp.dot(oc[rows, :], wc_ref[...], preferred_element_type=F32)
        x1 = x_ref[rows, :] + mod_ref[2:3, :] * mix
        x1_ref[rows, :] = x1
        h = _rms_mod(x1, g_ref[...], mod_ref[4:5, :], mod_ref[3:4, :])
        _store_token_tiles(h_ref, h, first=r * sub * SUBLANES)

        hi = h.astype(BF16)
        lo = (h - hi.astype(F32)).astype(BF16)
        logits = (jnp.dot(hi, rhi_ref[...], preferred_element_type=F32)
                  + jnp.dot(hi, rlo_ref[...], preferred_element_type=F32)
                  + jnp.dot(lo, rhi_ref[...], preferred_element_type=F32)) + rb_ref[...]
        lt = logits.T
        r8 = lax.broadcasted_iota(jnp.int32, (8, sub), 0)
        neg = -jnp.inf
        gl = jnp.where(r8 < N_GROUPS, lt[0:8], neg)
        gmax = jnp.max(gl, axis=0, keepdims=True)
        gidx = jnp.min(jnp.where(gl == gmax, r8, 8), axis=0, keepdims=True)
        p_sel = 1.0 / jnp.sum(jnp.exp(gl - gmax), axis=0, keepdims=True)
        sel = jnp.zeros((8, sub), F32)
        for g in range(N_GROUPS):
            sel = sel + jnp.where(gidx == g, lt[8 + 8 * g:16 + 8 * g], 0.0)
        v1 = jnp.max(sel, axis=0, keepdims=True)
        i1 = jnp.min(jnp.where(sel == v1, r8, 8), axis=0, keepdims=True)
        sel2 = jnp.where(r8 == i1, neg, sel)
        v2 = jnp.max(sel2, axis=0, keepdims=True)
        i2 = jnp.min(jnp.where(sel2 == v2, r8, 8), axis=0, keepdims=True)
        e = jnp.exp(v2 - v1)
        gate_ref[0:1, rows] = p_sel / (1.0 + e)
        gate_ref[1:2, rows] = p_sel * e / (1.0 + e)
        eid0 = gidx * EXPERTS_PER_GROUP + i1
        eid1 = gidx * EXPERTS_PER_GROUP + i2
        eid_ref[0:1, rows] = eid0
        eid_ref[1:2, rows] = eid1

        r32 = lax.broadcasted_iota(jnp.int32, (N_EXPERTS, sub), 0)
        oh0 = r32 == eid0
        oh1 = r32 == eid1
        cnt = jnp.where(oh0, 1.0, 0.0) + jnp.where(oh1, 1.0, 0.0)
        before = jnp.dot(cnt.astype(BF16), tri_ref[...], preferred_element_type=F32) + run[:, 0:1]
        rank_ref[0:1, rows] = jnp.sum(jnp.where(oh0, before, 0.0), axis=0, keepdims=True).astype(jnp.int32)
        rank_ref[1:2, rows] = jnp.sum(jnp.where(oh1, before, 0.0), axis=0, keepdims=True).astype(jnp.int32)
        run = run + jnp.sum(cnt, axis=1, keepdims=True)
    run_ref[...] = run
    cnt_ref[...] = run


def _outproj(oa, ob, cp, conv_w, x, mod_l, g2, wa, wb, wc, rhi, rlo, rb, tri, seq):
    n, d = x.shape
    tm = TOKEN_TILE
    tiles_per_seq = seq // tm
    halo = tm // SUBLANES
    row = lambda i: (i, 0)
    col = lambda i: (0, i)
    const = lambda i: (0, 0)
    return pl.pallas_call(
        functools.partial(_outproj_kernel, tiles_per_seq),
        grid=(n // tm,),
        in_specs=[pl.BlockSpec((tm, A_WIDTH), row), pl.BlockSpec((tm, B_WIDTH), row),
                  pl.BlockSpec((tm, 2 * C_WIDTH), row),
                  pl.BlockSpec((SUBLANES, 2 * C_WIDTH), lambda i: (jnp.maximum(i * halo - 1, 0), 0)),
                  pl.BlockSpec((SUBLANES, 2 * C_WIDTH),
                               lambda i: (jnp.minimum((i + 1) * halo, n // SUBLANES - 1), 0)),
                  pl.BlockSpec((3, C_WIDTH), const), pl.BlockSpec((tm, d), row),
                  pl.BlockSpec((None, 6, d), lambda i: (i // tiles_per_seq, 0, 0)),
                  pl.BlockSpec((1, d), const),
                  pl.BlockSpec((A_WIDTH, d), const), pl.BlockSpec((B_WIDTH, d), const),
                  pl.BlockSpec((C_WIDTH, d), const),
                  pl.BlockSpec((d, LANES), const), pl.BlockSpec((d, LANES), const),
                  pl.BlockSpec((1, LANES), const), pl.BlockSpec(tri.shape, const)],
        out_specs=[pl.BlockSpec((tm, d), row), pl.BlockSpec((tm * SUBLANES, LANES), row),
                   pl.BlockSpec((2, tm), col), pl.BlockSpec((2, tm), col), pl.BlockSpec((2, tm), col),
                   pl.BlockSpec((N_EXPERTS, LANES), const)],
        out_shape=[jax.ShapeDtypeStruct((n, d), F32), jax.ShapeDtypeStruct((n * SUBLANES, LANES), F32),
                   jax.ShapeDtypeStruct((2, n), jnp.int32), jax.ShapeDtypeStruct((2, n), F32),
                   jax.ShapeDtypeStruct((2, n), jnp.int32),
                   jax.ShapeDtypeStruct((N_EXPERTS, LANES), F32)],
        scratch_shapes=[pltpu.VMEM((N_EXPERTS, LANES), F32)],
        compiler_params=_cparams(("arbitrary",)),
        name="outproj_router",
    )(oa, ob, cp, cp, cp, conv_w, x, mod_l, g2, wa, wb, wc, rhi, rlo, rb, tri)


def _dispatch_kernel(dest_ref, h_ref, buf_in_ref, buf_ref, sem):
    del buf_in_ref
    td = dest_ref.shape[0] // 2

    def body(t, carry):
        src = h_ref.at[pl.ds(pl.multiple_of(t * SUBLANES, SUBLANES), SUBLANES)]
        for k in range(2):
            dst = buf_ref.at[pl.ds(pl.multiple_of(dest_ref[2 * t + k], SUBLANES), SUBLANES)]
            pltpu.make_async_copy(src, dst, sem).start(priority=k)
        return carry

    lax.fori_loop(0, td, body, 0, unroll=DMA_UNROLL)
    for _ in range(2):
        pltpu.make_async_copy(h_ref, buf_ref.at[pl.ds(0, td * SUBLANES)], sem).wait()


def _dispatch(dest8, h8, buf_init):
    n = dest8.shape[0] // 2
    td = DMA_TILE
    return pl.pallas_call(
        _dispatch_kernel,
        grid=(n // td,),
        in_specs=[pl.BlockSpec((2 * td,), lambda i: (i,), memory_space=pltpu.SMEM),
                  pl.BlockSpec((td * SUBLANES, LANES), lambda i: (i, 0)),
                  pl.BlockSpec(memory_space=pl.ANY)],
        out_specs=pl.BlockSpec(memory_space=pl.ANY),
        out_shape=jax.ShapeDtypeStruct(buf_init.shape, F32),
        scratch_shapes=[pltpu.SemaphoreType.DMA(())],
        input_output_aliases={2: 0},
        compiler_params=pltpu.CompilerParams(dimension_semantics=("arbitrary",), has_side_effects=True),
        name="moe_dispatch",
    )(dest8, h8, buf_init)


def _expert_kernel(layer, be_ref, nxt_ref, nu_ref, x_ref, w1_hbm, w3_hbm, w2_hbm, y_ref,
                   w1f, w3f, w2f, w1b, w3b, w2b, wsem, slot_ref):
    i = pl.program_id(0)
    rb = x_ref.shape[0] // SUBLANES
    e = be_ref[i]
    active = i < nu_ref[0]
    changed = jnp.logical_and(active, jnp.logical_or(i == 0, e != be_ref[jnp.maximum(i - 1, 0)]))

    def weight_copies(expert, slot):
        return [pltpu.make_async_copy(src.at[layer, expert], dst.at[slot], wsem.at[slot])
                for src, dst in ((w1_hbm, w1f), (w3_hbm, w3f), (w2_hbm, w2f))]

    @pl.when(i == 0)
    def _():
        slot_ref[0] = 0
        for c in weight_copies(e, 0):
            c.start()

    @pl.when(changed)
    def _():
        slot = slot_ref[0]
        for c in weight_copies(e, slot):
            c.wait()
        w1b[...] = w1f[slot].astype(BF16)
        w3b[...] = w3f[slot].astype(BF16)
        w2b[...] = w2f[slot].astype(BF16)
        nxt = nxt_ref[i]

        @pl.when(nxt != e)
        def _():
            for c in weight_copies(nxt, 1 - slot):
                c.start()

        slot_ref[0] = 1 - slot

    @pl.when(active)
    def _():
        x = _load_token_tiles(x_ref, rb).astype(BF16)
        de = w1b.shape[1]
        y = None
        for c in range(0, de, EXPERT_COL_CHUNK):
            cols = slice(c, c + EXPERT_COL_CHUNK)
            a = jnp.dot(x, w1b[:, cols], preferred_element_type=F32)
            b = jnp.dot(x, w3b[:, cols], preferred_element_type=F32)
            act = (a * jax.nn.sigmoid(a) * b).astype(BF16)
            part = jnp.dot(act, w2b[cols, :], preferred_element_type=F32)
            y = part if y is None else y + part
        _store_token_tiles(y_ref, y)

    @pl.when(i >= nu_ref[0])
    def _():
        y_ref[...] = jnp.zeros_like(y_ref)


def _experts(block_e, next_e, n_used, buf8, w1, w3, w2, layer, cap):
    d = D_MODEL
    rb = EXPERT_ROWS
    de = w1.shape[-1]
    hbm = pl.BlockSpec(memory_space=pl.ANY)
    grid_spec = pltpu.PrefetchScalarGridSpec(
        num_scalar_prefetch=3,
        grid=(cap // rb,),
        in_specs=[pl.BlockSpec((rb * SUBLANES, LANES), lambda i, be, nx, nu: (i, 0)), hbm, hbm, hbm],
        out_specs=pl.BlockSpec((rb * SUBLANES, LANES), lambda i, be, nx, nu: (i, 0)),
        scratch_shapes=[pltpu.VMEM((2, d, de), F32), pltpu.VMEM((2, d, de), F32), pltpu.VMEM((2, de, d), F32),
                        pltpu.VMEM((d, de), BF16), pltpu.VMEM((d, de), BF16), pltpu.VMEM((de, d), BF16),
                        pltpu.SemaphoreType.DMA((2,)), pltpu.SMEM((1,), jnp.int32)],
    )
    return pl.pallas_call(
        functools.partial(_expert_kernel, layer),
        grid_spec=grid_spec,
        out_shape=jax.ShapeDtypeStruct((cap * SUBLANES, LANES), F32),
        compiler_params=_cparams(("arbitrary",)),
        name="moe_experts",
    )(block_e, next_e, n_used, buf8, w1, w3, w2)


def _final_kernel(x_ref, dest_ref, destn_ref, y_hbm, gt_ref, mod_ref, g_ref, o_ref, ybuf, ysem):
    x = x_ref[...] + mod_ref[5:6, :] * _gated_pair(dest_ref, destn_ref, y_hbm, ybuf, ysem, gt_ref[...])
    ms = jnp.mean(x * x, axis=-1, keepdims=True)
    o_ref[...] = x * lax.rsqrt(ms + EPS) * g_ref[...]


def _final(x1, dest8, y8, gates, mod_l, g, seq):
    n, d = x1.shape
    tm = TOKEN_TILE
    tiles_per_seq = seq // tm
    row = lambda i: (i, 0)
    gather_specs, scratch = _moe_gather_specs(tm, n // tm)
    return pl.pallas_call(
        _final_kernel,
        grid=(n // tm,),
        in_specs=[pl.BlockSpec((tm, d), row)] + gather_specs + [
            pl.BlockSpec((tm, 2), row),
            pl.BlockSpec((None, 6, d), lambda i: (i // tiles_per_seq, 0, 0)),
            pl.BlockSpec((1, d), lambda i: (0, 0))],
        out_specs=pl.BlockSpec((tm, d), row),
        out_shape=jax.ShapeDtypeStruct((n, d), F32),
        scratch_shapes=scratch,
        compiler_params=_cparams(("arbitrary",)),
        name="final_norm",
    )(x1, dest8, dest8, y8, gates, mod_l, g)


def _rope_tables(seq):
    freqs = ROPE_THETA ** (-np.arange(0, A_QK_DIM, 2, dtype=np.float32) / A_QK_DIM)
    lane = np.arange(LANES)
    f = freqs[lane % 16][None, :]
    sign = np.where(lane < LANES // 2, -1.0, 1.0)[None, :].astype(np.float32)
    pos = np.arange(seq, dtype=np.float32)[:, None]
    ang_a = pos * f
    axial = np.where((lane % 32) < 16, np.floor(pos / GRID_W), np.mod(pos, GRID_W)).astype(np.float32)
    ang_b = axial * f
    tabs = (np.cos(ang_a), np.sin(ang_a) * sign, np.cos(ang_b), np.sin(ang_b) * sign)
    return tuple(jnp.asarray(t, dtype=F32) for t in tabs)


def _qb_perm():
    cols = []
    for j in range(B_HEADS // B_KV_HEADS):
        cols += list(range(j * HEAD_DIM, (j + 1) * HEAD_DIM))
        cols += list(range((4 + j) * HEAD_DIM, (5 + j) * HEAD_DIM))
    return np.asarray(cols, dtype=np.int32)


def _split_half_perms():
    lane = np.arange(LANES)
    half, f = lane // 64, lane % 16
    slot = (lane % 64) // 16
    perm_a = slot * 32 + half * 16 + f
    head, part = (lane % 64) // 32, (lane % 32) // 16
    perm_b = head * 64 + part * 32 + half * 16 + f
    return perm_a.astype(np.int32), perm_b.astype(np.int32)


def _trunk(x, mod, batch, seq, p, spare_buf=None):
    n = batch * seq
    d = D_MODEL
    depth = p["w_in"].shape[0]
    tabs = _rope_tables(seq)
    perm = _qb_perm()
    perm_a, perm_b = _split_half_perms()
    sub = min(TOKEN_TILE, ROUTER_SUBTILE)
    tri = jnp.asarray(np.triu(np.ones((sub, sub), np.float32), 1), dtype=BF16)
    rb = EXPERT_ROWS
    cap = (2 * n + N_EXPERTS * (rb - 1) + rb - 1) // rb * rb
    nb = cap // rb
    x = x.reshape(n, d)
    moe = None
    buf8 = spare_buf if spare_buf is not None and spare_buf.shape[0] >= cap * SUBLANES else None
    for l in range(depth):
        lambda_init = 0.8 - 0.6 * math.exp(-0.3 * l)
        mod_l = mod[l].reshape(batch, 6, d)
        cols = np.arange(IN_COLS, dtype=np.int32)
        for blk in range(2 * A_WIDTH // LANES):
            cols[blk * LANES:(blk + 1) * LANES] = blk * LANES + perm_a
        qb0 = 3 * A_WIDTH
        for blk in range(B_WIDTH // LANES):
            cols[qb0 + blk * LANES:qb0 + (blk + 1) * LANES] = qb0 + perm[blk * LANES + perm_b]
        cols[qb0 + B_WIDTH:qb0 + B_WIDTH + B_KV_WIDTH] = qb0 + B_WIDTH + perm_b
        w_in = p["w_in"][l][:, cols].astype(BF16)
        tile2 = lambda v: jnp.tile(v.reshape(1, HEAD_DIM), (1, 2))
        x, qa, ka, va, qb, kb, vb, cp = _inproj(x, moe, mod_l, p["norm1"][l].reshape(1, d), w_in,
                                                tile2(p["q_norm"][l])[:, perm_b], tile2(p["k_norm"][l])[:, perm_b],
                                                tabs, seq)
        oa = _diff_attn(qa, ka, va, p["diff_lambda"][l], tile2(p["diff_subln"][l]), lambda_init, batch, seq)
        ob = _gqa_attn(qb, kb, vb, batch, seq)
        w_out = p["w_out"][l]
        wa = w_out[:A_WIDTH].astype(BF16)
        wb = w_out[A_WIDTH + perm].astype(BF16)
        wc = w_out[A_WIDTH + B_WIDTH:].astype(BF16)
        wr = jnp.zeros((d, LANES), F32)
        wr = wr.at[:, :N_GROUPS].set(p["router_group_w"][l]).at[:, 8:8 + N_EXPERTS].set(p["router_expert_w"][l])
        rhi = wr.astype(BF16)
        rlo = (wr - rhi.astype(F32)).astype(BF16)
        rbias = jnp.zeros((1, LANES), F32)
        rbias = rbias.at[0, :N_GROUPS].set(p["router_group_b"][l]).at[0, 8:8 + N_EXPERTS].set(p["router_expert_b"][l])
        x1, h8, eid, gate, rank, cnt = _outproj(oa, ob, cp, p["conv_w"][l], x, mod_l, p["norm2"][l].reshape(1, d),
                                                wa, wb, wc, rhi, rlo, rbias, tri, seq)
        counts = cnt[:, 0].astype(jnp.int32)
        expert_range = jnp.arange(N_EXPERTS, dtype=jnp.int32)
        pcounts = (counts + rb - 1) // rb * rb
        pends = jnp.cumsum(pcounts)
        pstarts = pends - pcounts
        block_row = jnp.arange(nb, dtype=jnp.int32)[:, None] * rb
        block_e = jnp.minimum(jnp.sum((pends[None, :] <= block_row).astype(jnp.int32), axis=1), N_EXPERTS - 1)
        n_used = (pends[-1:] // rb).astype(jnp.int32)
        later = (expert_range[None, :] > expert_range[:, None]) & (pcounts[None, :] > 0)
        next_owner = jnp.min(jnp.where(later, expert_range[None, :], N_EXPERTS), axis=1)
        next_owner = jnp.where(next_owner == N_EXPERTS, expert_range, next_owner)
        next_e = jnp.sum(jnp.where(block_e[:, None] == expert_range[None, :], next_owner[None, :], 0), axis=1)
        expert_ids = jnp.arange(N_EXPERTS, dtype=jnp.int32)[:, None, None]
        dest8 = (jnp.sum(jnp.where(eid[None] == expert_ids, pstarts[:, None, None], 0), axis=0) + rank) * SUBLANES
        dest8 = dest8.T.reshape(2 * n)
        if buf8 is None:
            buf8 = jnp.zeros((cap * SUBLANES, LANES), F32)
        buf8 = _dispatch(dest8, h8, buf8)
        y8 = _experts(block_e, next_e.astype(jnp.int32), n_used, buf8,
                      p["expert_w1"], p["expert_w3"], p["expert_w2"], l, cap)
        x = x1
        moe = (dest8, y8, gate.T, mod_l)
    out = _final(x, *moe, p["final_norm"].reshape(1, d), seq)
    return out.reshape(batch, seq, d), buf8


def kernel(x_prompt, x_sample, c_prompt, c_sample, w_ada, b_ada, norm1, norm2, w_in, w_out, diff_lambda,
           diff_subln, q_norm, k_norm, conv_w, router_group_w, router_group_b, router_expert_w,
           router_expert_b, expert_w1, expert_w3, expert_w2, final_norm):
    p = dict(norm1=norm1, norm2=norm2, w_in=w_in, w_out=w_out, diff_lambda=diff_lambda, diff_subln=diff_subln,
             q_norm=q_norm, k_norm=k_norm, conv_w=conv_w, router_group_w=router_group_w,
             router_group_b=router_group_b, router_expert_w=router_expert_w, router_expert_b=router_expert_b,
             expert_w1=expert_w1, expert_w3=expert_w3, expert_w2=expert_w2, final_norm=final_norm)
    bp, sp, _ = x_prompt.shape
    bs, ss, _ = x_sample.shape
    mod = _ada_mod(jnp.concatenate([c_prompt, c_sample], axis=0), w_ada, b_ada)
    y_prompt, spare = _trunk(x_prompt, mod[:, :bp], bp, sp, p)
    y_sample, _ = _trunk(x_sample, mod[:, bp:], bs, ss, p, spare)
    return (y_prompt, y_sample)
```
